```python
import math
import jax, jax.numpy as jnp
from jax import lax
import numpy as np

D_MODEL = 1024
BATCH = 32
SEQ = 2048
DEPTH = 1
DEC_BATCH = 32
DEC_SEQ = 64
PAST_LEN = 1024

CHUNK = 64
N_HEADS = 8
N_KV_HEADS = 2
HEAD_DIM = 64
Q_GROUP = N_HEADS // N_KV_HEADS
ATTN_WIDTH = N_HEADS * HEAD_DIM
KV_WIDTH = N_KV_HEADS * HEAD_DIM
WINDOW = 128
WINDOW_CHUNKS = WINDOW // CHUNK
CONV_CH = D_MODEL // 2
CONV_K = 31
MIX_WIDTH = ATTN_WIDTH + CONV_CH
IN_WIDTH = ATTN_WIDTH + 2 * KV_WIDTH + 2 * CONV_CH
N_BUCKETS = 32
MAX_DISTANCE = 128
N_GROUPS = 4
EXPERTS_PER_GROUP = 4
N_EXPERTS = N_GROUPS * EXPERTS_PER_GROUP
TOP_K = 2
D_EXPERT = D_MODEL // 4
ALPHA = (2 * DEPTH) ** 0.25
BETA = (8 * DEPTH) ** -0.25
LN_EPS = 1e-5
NEG_INF = -1e30

kernel_name = "hybrid_swa_conformer_hmoe_stream_step"


def layer_norm(x, g, b):
    xf = x.astype(jnp.float32)
    mu = jnp.mean(xf, -1, keepdims=True)
    xc = xf - mu
    var = jnp.mean(xc * xc, -1, keepdims=True)
    return (xc * lax.rsqrt(var + LN_EPS) * g.astype(jnp.float32) + b.astype(jnp.float32)).astype(x.dtype)


def t5_bucket(rel):
    nb = N_BUCKETS // 2
    max_exact = nb // 2
    ret = jnp.where(rel > 0, nb, 0)
    n = jnp.abs(rel)
    large = max_exact + (jnp.log(jnp.maximum(n, 1).astype(jnp.float32) / max_exact)
                         / math.log(MAX_DISTANCE / max_exact) * (nb - max_exact)).astype(jnp.int32)
    large = jnp.minimum(large, nb - 1)
    return ret + jnp.where(n < max_exact, n, large)


def relative_bias(rel, table):
    b = table[t5_bucket(rel)].astype(jnp.float32)
    tq, tk = rel.shape
    return jnp.transpose(b, (2, 0, 1)).reshape(N_KV_HEADS, Q_GROUP, tq, tk)


def sink_attention(q, k, v, bias, mask, sinks):
    logits = jnp.einsum('bcqkgd,bcskd->bckgqs', q, k).astype(jnp.float32) + bias
    logits = jnp.where(mask[None, :, None, None], logits, NEG_INF)
    sink = sinks.astype(jnp.float32).reshape(1, 1, N_KV_HEADS, Q_GROUP, 1, 1)
    m = jnp.maximum(jnp.max(logits, -1, keepdims=True), sink)
    p = jnp.exp(logits - m)
    probs = p / (jnp.sum(p, -1, keepdims=True) + jnp.exp(sink - m))
    out = jnp.einsum('bckgqs,bcskd->bcqkgd', probs.astype(v.dtype), v)
    b_, c_, tq = out.shape[:3]
    return out.reshape(b_, c_, tq, ATTN_WIDTH)


def banded_window_attention(q, k, v, table, sinks):
    B, T = q.shape[:2]
    nc = T // CHUNK
    qc = q.reshape(B, nc, CHUNK, N_KV_HEADS, Q_GROUP, HEAD_DIM)
    pad = jnp.zeros((B, WINDOW_CHUNKS, CHUNK, N_KV_HEADS, HEAD_DIM), k.dtype)
    kp = jnp.concatenate([pad, k.reshape(B, nc, CHUNK, N_KV_HEADS, HEAD_DIM)], 1)
    vp = jnp.concatenate([pad, v.reshape(B, nc, CHUNK, N_KV_HEADS, HEAD_DIM)], 1)
    kb = jnp.concatenate([kp[:, w:w + nc] for w in range(WINDOW_CHUNKS + 1)], axis=2)
    vb = jnp.concatenate([vp[:, w:w + nc] for w in range(WINDOW_CHUNKS + 1)], axis=2)
    tk = (WINDOW_CHUNKS + 1) * CHUNK
    key_chunk = jnp.arange(nc)[:, None] - WINDOW_CHUNKS + jnp.arange(WINDOW_CHUNKS + 1)[None, :]
    mask = jnp.repeat(key_chunk >= 0, CHUNK, axis=1)[:, None, :]
    rel = jnp.arange(tk)[None, :] - WINDOW - jnp.arange(CHUNK)[:, None]
    bias = relative_bias(rel, table)
    out = sink_attention(qc, kb, vb, bias, mask, sinks)
    return out.reshape(B, T, ATTN_WIDTH)


def cached_window_attention(q, k, v, cache_k, cache_v, table, sinks):
    L = cache_k.shape[1]
    T = q.shape[1]
    kf = jnp.concatenate([cache_k, k], 1)
    vf = jnp.concatenate([cache_v, v], 1)
    rel = jnp.arange(L + T)[None, :] - L - jnp.arange(T)[:, None]
    bias = relative_bias(rel, table)
    mask = jnp.ones((1, T, L + T), bool)
    out = sink_attention(q[:, None], kf[:, None], vf[:, None], bias, mask, sinks)[:, 0]
    return out, kf[:, -L:], vf[:, -L:]


def conformer_conv(a, g, hist, w_dw, b_dw, ln_g, ln_b):
    u = a * jax.nn.sigmoid(g)
    up = jnp.concatenate([hist, u], 1)
    y = lax.conv_general_dilated(up, w_dw[:, None, :], window_strides=(1,), padding='VALID',
                                 dimension_numbers=('NWC', 'WIO', 'NWC'),
                                 feature_group_count=CONV_CH) + b_dw
    y = jax.nn.silu(layer_norm(y, ln_g, ln_b))
    return y, up[:, -(CONV_K - 1):]


def hier_moe(h, w_group, b_group, w_erouter, b_erouter, w_gate, w_up, w_down):
    B, T, D = h.shape
    hf = h.reshape(-1, D)
    gl = (hf @ w_group).astype(jnp.float32) + b_group.astype(jnp.float32)
    gidx = jnp.argmax(gl, -1)
    pg = jnp.take_along_axis(jax.nn.softmax(gl, -1), gidx[:, None], -1)
    el = jnp.einsum('nd,dge->nge', hf, w_erouter).astype(jnp.float32) + b_erouter.astype(jnp.float32)
    el = jnp.take_along_axis(el, gidx[:, None, None], axis=1)[:, 0]
    tv, ti = lax.top_k(el, TOP_K)
    wk = jax.nn.softmax(tv, -1) * pg
    eidx = gidx[:, None] * EXPERTS_PER_GROUP + ti
    combine = jnp.einsum('nk,nke->ne', wk, jax.nn.one_hot(eidx, N_EXPERTS, dtype=jnp.float32)).astype(h.dtype)
    hg = jnp.einsum('nd,edf->nef', hf, w_gate)
    hu = jnp.einsum('nd,edf->nef', hf, w_up)
    act = jax.nn.silu(hg) * hu * combine[:, :, None]
    return jnp.einsum('nef,efd->nd', act, w_down).reshape(B, T, D)


def trunk_layer(x, c, hist_k, hist_v, hist_conv, table, p, prompt):
    (w_ada, b_ada, w_in, sinks, w_dw, b_dw, cln_g, cln_b, w_out, ln1_g, ln1_b,
     w_group, b_group, w_erouter, b_erouter, w_gate, w_up, w_down, ln2_g, ln2_b) = p
    B, T, _ = x.shape
    mod = jax.nn.silu(c) @ w_ada + b_ada
    sh1, sc1, g1, sh2, sc2, g2 = [m[:, None, :] for m in jnp.split(mod, 6, -1)]
    h = x * (1 + sc1) + sh1
    proj = h @ w_in
    o1 = ATTN_WIDTH
    o2 = o1 + KV_WIDTH
    o3 = o2 + KV_WIDTH
    o4 = o3 + CONV_CH
    q = proj[..., :o1].reshape(B, T, N_KV_HEADS, Q_GROUP, HEAD_DIM) * (HEAD_DIM ** -0.5)
    k = proj[..., o1:o2].reshape(B, T, N_KV_HEADS, HEAD_DIM)
    v = proj[..., o2:o3].reshape(B, T, N_KV_HEADS, HEAD_DIM)
    a = proj[..., o3:o4]
    g = proj[..., o4:]
    if prompt:
        attn = banded_window_attention(q, k, v, table, sinks)
        new_k, new_v = k[:, -WINDOW:], v[:, -WINDOW:]
        hist_conv = jnp.zeros((B, CONV_K - 1, CONV_CH), x.dtype)
    else:
        attn, new_k, new_v = cached_window_attention(q, k, v, hist_k, hist_v, table, sinks)
    conv, new_conv = conformer_conv(a, g, hist_conv, w_dw, b_dw, cln_g, cln_b)
    mix = jnp.concatenate([attn, conv], -1) @ w_out
    x = layer_norm(ALPHA * x + (1 + g1) * mix, ln1_g, ln1_b)
    h2 = x * (1 + sc2) + sh2
    ff = hier_moe(h2, w_group, b_group, w_erouter, b_erouter, w_gate, w_up, w_down)
    x = layer_norm(ALPHA * x + (1 + g2) * ff, ln2_g, ln2_b)
    return x, new_k, new_v, new_conv


def setup_inputs(seed: int = 0) -> dict:
    key = jax.random.key(seed)
    ks = jax.random.split(key, 32)
    f32 = jnp.float32

    def nrm(k, shape, s):
        return jax.random.normal(k, shape, f32) * s

    L_C = min(WINDOW, PAST_LEN)
    return {
        "x_prompt": nrm(ks[0], (BATCH, SEQ, D_MODEL), 1.0),
        "x_sample": nrm(ks[1], (DEC_BATCH, DEC_SEQ, D_MODEL), 1.0),
        "cache_attn_k": nrm(ks[2], (DEPTH, DEC_BATCH, L_C, N_KV_HEADS, HEAD_DIM), 1.0),
        "cache_attn_v": nrm(ks[3], (DEPTH, DEC_BATCH, L_C, N_KV_HEADS, HEAD_DIM), 1.0),
        "state_conv": nrm(ks[4], (DEPTH, DEC_BATCH, CONV_K - 1, CONV_CH), 0.5),
        "c_prompt": nrm(ks[5], (BATCH, D_MODEL), 1.0),
        "c_sample": nrm(ks[6], (DEC_BATCH, D_MODEL), 1.0),
        "rel_bias": nrm(ks[7], (N_BUCKETS, N_HEADS), 0.5),
        "w_ada": nrm(ks[8], (DEPTH, D_MODEL, 6 * D_MODEL), 0.1 * D_MODEL ** -0.5),
        "b_ada": nrm(ks[9], (DEPTH, 6 * D_MODEL), 0.01),
        "w_in": nrm(ks[10], (DEPTH, D_MODEL, IN_WIDTH), D_MODEL ** -0.5),
        "attn_sinks": nrm(ks[11], (DEPTH, N_HEADS), 0.5),
        "w_dw": nrm(ks[12], (DEPTH, CONV_K, CONV_CH), CONV_K ** -0.5),
        "b_dw": nrm(ks[13], (DEPTH, CONV_CH), 0.01),
        "conv_ln_g": 1.0 + nrm(ks[14], (DEPTH, CONV_CH), 0.01),
        "conv_ln_b": nrm(ks[15], (DEPTH, CONV_CH), 0.01),
        "w_out": nrm(ks[16], (DEPTH, MIX_WIDTH, D_MODEL), BETA * MIX_WIDTH ** -0.5),
        "ln1_g": 1.0 + nrm(ks[17], (DEPTH, D_MODEL), 0.01),
        "ln1_b": nrm(ks[18], (DEPTH, D_MODEL), 0.01),
        "w_group": nrm(ks[19], (DEPTH, D_MODEL, N_GROUPS), D_MODEL ** -0.5),
        "b_group": nrm(ks[20], (DEPTH, N_GROUPS), 0.01),
        "w_erouter": nrm(ks[21], (DEPTH, D_MODEL, N_GROUPS, EXPERTS_PER_GROUP), D_MODEL ** -0.5),
        "b_erouter": nrm(ks[22], (DEPTH, N_GROUPS, EXPERTS_PER_GROUP), 0.01),
        "w_gate": nrm(ks[23], (DEPTH, N_EXPERTS, D_MODEL, D_EXPERT), D_MODEL ** -0.5),
        "w_up": nrm(ks[24], (DEPTH, N_EXPERTS, D_MODEL, D_EXPERT), D_MODEL ** -0.5),
        "w_down": nrm(ks[25], (DEPTH, N_EXPERTS, D_EXPERT, D_MODEL), BETA * D_EXPERT ** -0.5),
        "ln2_g": 1.0 + nrm(ks[26], (DEPTH, D_MODEL), 0.01),
        "ln2_b": nrm(ks[27], (DEPTH, D_MODEL), 0.01),
    }


def reference(x_prompt, x_sample, cache_attn_k, cache_attn_v, state_conv, c_prompt, c_sample,
              rel_bias, w_ada, b_ada, w_in, attn_sinks, w_dw, b_dw, conv_ln_g, conv_ln_b, w_out,
              ln1_g, ln1_b, w_group, b_group, w_erouter, b_erouter, w_gate, w_up, w_down,
              ln2_g, ln2_b):
    yp, ys = x_prompt, x_sample
    pks, pvs, pcs, sks, svs, scs = [], [], [], [], [], []
    for l in range(DEPTH):
        p = (w_ada[l], b_ada[l], w_in[l], attn_sinks[l], w_dw[l], b_dw[l], conv_ln_g[l], conv_ln_b[l],
             w_out[l], ln1_g[l], ln1_b[l], w_group[l], b_group[l], w_erouter[l], b_erouter[l],
             w_gate[l], w_up[l], w_down[l], ln2_g[l], ln2_b[l])
        yp, pk, pv, pc = trunk_layer(yp, c_prompt, None, None, None, rel_bias, p, True)
        ys, sk, sv, sc = trunk_layer(ys, c_sample, cache_attn_k[l], cache_attn_v[l], state_conv[l],
                                     rel_bias, p, False)
        pks.append(pk); pvs.append(pv); pcs.append(pc)
        sks.append(sk); svs.append(sv); scs.append(sc)
    new_k_prompt = jnp.stack(pks)
    new_v_prompt = jnp.stack(pvs)
    new_conv_prompt = jnp.stack(pcs)
    new_k_sample = jnp.stack(sks)
    new_v_sample = jnp.stack(svs)
    new_conv_sample = jnp.stack(scs)
    return (yp, ys, new_k_prompt, new_v_prompt, new_conv_prompt, new_k_sample, new_v_sample, new_conv_sample)
```

```python
import functools
import math

import jax
import jax.numpy as jnp
from jax import lax
from jax.experimental import pallas as pl
from jax.experimental.pallas import tpu as pltpu

D_MODEL = 1024
CHUNK = 64
N_HEADS = 8
N_KV_HEADS = 2
HEAD_DIM = 64
Q_GROUP = N_HEADS // N_KV_HEADS
ATTN_WIDTH = N_HEADS * HEAD_DIM
KV_WIDTH = N_KV_HEADS * HEAD_DIM
WINDOW = 128
WINDOW_CHUNKS = WINDOW // CHUNK
KEYS = WINDOW + CHUNK
CONV_CH = D_MODEL // 2
CONV_K = 31
CONV_HIST = CONV_K - 1
CONV_PAD = 32
MIX_WIDTH = ATTN_WIDTH + CONV_CH
IN_WIDTH = ATTN_WIDTH + 2 * KV_WIDTH + 2 * CONV_CH
N_BUCKETS = 32
MAX_DISTANCE = 128
N_GROUPS = 4
EXPERTS_PER_GROUP = 4
N_EXPERTS = N_GROUPS * EXPERTS_PER_GROUP
D_EXPERT = D_MODEL // 4
DEPTH = 1
ALPHA = (2 * DEPTH) ** 0.25
LN_EPS = 1e-5
NEG_INF = -1e30
ROUTE_LANES = 128
VMEM_LIMIT = 56 * 1024 * 1024

BF16 = jnp.bfloat16
F32 = jnp.float32


def _cparams(*sem):
    return pltpu.CompilerParams(dimension_semantics=sem, vmem_limit_bytes=VMEM_LIMIT)


def _mod_kernel(c_ref, w_ref, b_ref, o_ref):
    c = c_ref[...]
    s = c * jax.nn.sigmoid(c)
    o_ref[...] = jnp.dot(s, w_ref[...], preferred_element_type=F32,
                         precision=lax.Precision.HIGHEST) + b_ref[...]


def _modulation(c, w_ada, b_ada):
    n, d = c.shape
    width = w_ada.shape[1]
    tn = 1536
    return pl.pallas_call(
        _mod_kernel,
        grid=(width // tn,),
        in_specs=[pl.BlockSpec((n, d), lambda j: (0, 0)),
                  pl.BlockSpec((d, tn), lambda j: (0, j)),
                  pl.BlockSpec((1, tn), lambda j: (0, j))],
        out_specs=pl.BlockSpec((n, tn), lambda j: (0, j)),
        out_shape=jax.ShapeDtypeStruct((n, width), F32),
        compiler_params=_cparams("arbitrary"),
        name="modulation",
    )(c, w_ada, b_ada.reshape(1, width))


def _t5_bucket(rel):
    nb = N_BUCKETS // 2
    max_exact = nb // 2
    ret = jnp.where(rel > 0, nb, 0)
    n = jnp.abs(rel)
    large = max_exact + (jnp.log(jnp.maximum(n, 1).astype(jnp.float32) / max_exact)
                         / math.log(MAX_DISTANCE / max_exact) * (nb - max_exact)).astype(jnp.int32)
    large = jnp.minimum(large, nb - 1)
    return ret + jnp.where(n < max_exact, n, large)


def _bias_kernel(table_ref, bucket_ref, o_ref):
    bucket = bucket_ref[...]
    for h in range(N_HEADS):
        acc = jnp.zeros(bucket.shape, F32)
        for b in range(N_BUCKETS):
            acc = jnp.where(bucket == b, table_ref[b, h], acc)
        o_ref[h] = acc


def _relative_bias(table):
    rel = jnp.arange(KEYS)[None, :] - WINDOW - jnp.arange(CHUNK)[:, None]
    bucket = _t5_bucket(rel).astype(jnp.int32)
    bias = pl.pallas_call(
        _bias_kernel,
        in_specs=[pl.BlockSpec(memory_space=pltpu.SMEM),
                  pl.BlockSpec((CHUNK, KEYS), lambda: (0, 0))],
        out_specs=pl.BlockSpec((N_HEADS, CHUNK, KEYS), lambda: (0, 0, 0)),
        out_shape=jax.ShapeDtypeStruct((N_HEADS, CHUNK, KEYS), F32),
        name="relative_bias",
    )(table, bucket)
    return bias.reshape(N_KV_HEADS, Q_GROUP * CHUNK, KEYS)


def _layer_norm_rows(x, g, b):
    mu = jnp.mean(x, -1, keepdims=True)
    xc = x - mu
    var = jnp.mean(xc * xc, -1, keepdims=True)
    return xc * lax.rsqrt(var + LN_EPS) * g + b


def _mixer_kernel(x_ref, mod_ref, win_ref, bias_ref, sink_ref, wdw_ref, bdw_ref, clg_ref, clb_ref,
                  k0_ref, v0_ref, u0_ref,
                  mix_ref, nk_ref, nv_ref, nu_ref,
                  kcat, vcat, ucat, *, tm, masked):
    t = pl.program_id(1)
    nt = pl.num_programs(1)
    n_chunks = tm // CHUNK

    @pl.when(t == 0)
    def _():
        kcat[0:WINDOW, :] = k0_ref[0].astype(BF16)
        vcat[0:WINDOW, :] = v0_ref[0].astype(BF16)
        ucat[0:CONV_PAD, :] = u0_ref[0]

    sh1 = mod_ref[0, 0:1, :]
    sc1 = mod_ref[0, 1:2, :]
    h = (x_ref[0] * (1.0 + sc1) + sh1).astype(BF16)
    o1 = ATTN_WIDTH
    o3 = o1 + 2 * KV_WIDTH
    q = (jnp.dot(h, win_ref[:, 0:o1], preferred_element_type=F32) * (HEAD_DIM ** -0.5)).astype(BF16)
    kv = jnp.dot(h, win_ref[:, o1:o3], preferred_element_type=F32)
    ag = jnp.dot(h, win_ref[:, o3:IN_WIDTH], preferred_element_type=F32)
    u = ag[:, 0:CONV_CH] * jax.nn.sigmoid(ag[:, CONV_CH:])
    kcat[WINDOW:WINDOW + tm, :] = kv[:, 0:KV_WIDTH].astype(BF16)
    vcat[WINDOW:WINDOW + tm, :] = kv[:, KV_WIDTH:].astype(BF16)
    ucat[CONV_PAD:CONV_PAD + tm, :] = u

    @pl.when(t == nt - 1)
    def _():
        if tm >= WINDOW:
            nk_ref[0] = kv[tm - WINDOW:, 0:KV_WIDTH]
            nv_ref[0] = kv[tm - WINDOW:, KV_WIDTH:]
        else:
            nk_ref[0, 0:WINDOW - tm, :] = k0_ref[0, tm:WINDOW, :]
            nv_ref[0, 0:WINDOW - tm, :] = v0_ref[0, tm:WINDOW, :]
            nk_ref[0, WINDOW - tm:, :] = kv[:, 0:KV_WIDTH]
            nv_ref[0, WINDOW - tm:, :] = kv[:, KV_WIDTH:]
        nu_ref[0] = ucat[CONV_PAD + tm - CONV_HIST:CONV_PAD + tm, :]

    rb = 64
    off = CONV_PAD - CONV_HIST
    for r in range(tm // rb):
        acc = jnp.zeros((rb, CONV_CH), F32) + bdw_ref[...]
        for k in range(CONV_K):
            acc = acc + ucat[r * rb + off + k:r * rb + off + k + rb, :] * wdw_ref[k:k + 1, :]
        y = _layer_norm_rows(acc, clg_ref[...], clb_ref[...])
        y = y * jax.nn.sigmoid(y)
        mix_ref[0, r * rb:(r + 1) * rb, ATTN_WIDTH:] = y.astype(BF16)

    gw = Q_GROUP * HEAD_DIM
    for c in range(n_chunks):
        qc = q[c * CHUNK:(c + 1) * CHUNK, :]
        if masked:
            first_valid = WINDOW - (t * n_chunks + c) * CHUNK
            valid = lax.broadcasted_iota(jnp.int32, (1, KEYS), 1) >= first_valid
        outs = []
        for kvh in range(N_KV_HEADS):
            kw = kcat[c * CHUNK:c * CHUNK + KEYS, kvh * HEAD_DIM:(kvh + 1) * HEAD_DIM]
            vw = vcat[c * CHUNK:c * CHUNK + KEYS, kvh * HEAD_DIM:(kvh + 1) * HEAD_DIM]
            qg = jnp.concatenate(
                [qc[:, kvh * gw + g * HEAD_DIM:kvh * gw + (g + 1) * HEAD_DIM] for g in range(Q_GROUP)], axis=0)
            s = lax.dot_general(qg, kw, (((1,), (1,)), ((), ())), preferred_element_type=F32)
            s = s + bias_ref[kvh]
            if masked:
                s = jnp.where(valid, s, NEG_INF)
            sink = sink_ref[kvh]
            m = jnp.maximum(jnp.max(s, -1, keepdims=True), sink)
            p = jnp.exp(s - m)
            den = jnp.sum(p, -1, keepdims=True) + jnp.exp(sink - m)
            o = jnp.dot(p.astype(BF16), vw, preferred_element_type=F32) * (1.0 / den)
            outs.extend(o[g * CHUNK:(g + 1) * CHUNK, :] for g in range(Q_GROUP))
        mix_ref[0, c * CHUNK:(c + 1) * CHUNK, 0:ATTN_WIDTH] = jnp.concatenate(outs, axis=1).astype(BF16)

    @pl.when(t < nt - 1)
    def _():
        if tm >= WINDOW:
            kcat[0:WINDOW, :] = kcat[tm:tm + WINDOW, :]
            vcat[0:WINDOW, :] = vcat[tm:tm + WINDOW, :]
            ucat[0:CONV_PAD, :] = ucat[tm:tm + CONV_PAD, :]


def _mixer(x, mod, w_in, bias, sink, w_dw, b_dw, cln_g, cln_b, k0, v0, u0, *, tm, masked):
    bsz, seq, d = x.shape
    assert seq % tm == 0 and tm % CHUNK == 0 and (tm >= WINDOW or seq == tm)
    nt = seq // tm
    kern = functools.partial(_mixer_kernel, tm=tm, masked=masked)
    const2 = lambda b, t: (0, 0)
    const3 = lambda b, t: (0, 0, 0)
    per_b = lambda b, t: (b, 0, 0)
    return pl.pallas_call(
        kern,
        grid=(bsz, nt),
        in_specs=[pl.BlockSpec((1, tm, d), lambda b, t: (b, t, 0)),
                  pl.BlockSpec((1, 6, d), per_b),
                  pl.BlockSpec((d, IN_WIDTH), const2),
                  pl.BlockSpec((N_KV_HEADS, Q_GROUP * CHUNK, KEYS), const3),
                  pl.BlockSpec((N_KV_HEADS, Q_GROUP * CHUNK, 1), const3),
                  pl.BlockSpec((CONV_K, CONV_CH), const2),
                  pl.BlockSpec((1, CONV_CH), const2),
                  pl.BlockSpec((1, CONV_CH), const2),
                  pl.BlockSpec((1, CONV_CH), const2),
                  pl.BlockSpec((1, WINDOW, KV_WIDTH), per_b),
                  pl.BlockSpec((1, WINDOW, KV_WIDTH), per_b),
                  pl.BlockSpec((1, CONV_PAD, CONV_CH), per_b)],
        out_specs=[pl.BlockSpec((1, tm, MIX_WIDTH), lambda b, t: (b, t, 0)),
                   pl.BlockSpec((1, WINDOW, KV_WIDTH), per_b),
                   pl.BlockSpec((1, WINDOW, KV_WIDTH), per_b),
                   pl.BlockSpec((1, CONV_HIST, CONV_CH), per_b)],
        out_shape=[jax.ShapeDtypeStruct((bsz, seq, MIX_WIDTH), BF16),
                   jax.ShapeDtypeStruct((bsz, WINDOW, KV_WIDTH), F32),
                   jax.ShapeDtypeStruct((bsz, WINDOW, KV_WIDTH), F32),
                   jax.ShapeDtypeStruct((bsz, CONV_HIST, CONV_CH), F32)],
        scratch_shapes=[pltpu.VMEM((WINDOW + tm, KV_WIDTH), BF16),
                        pltpu.VMEM((WINDOW + tm, KV_WIDTH), BF16),
                        pltpu.VMEM((CONV_PAD + tm, CONV_CH), F32)],
        compiler_params=_cparams("parallel", "arbitrary"),
        name="mixer",
    )(x, mod, w_in, bias, sink, w_dw, b_dw, cln_g, cln_b, k0, v0, u0)


def _post_kernel(x_ref, mix_ref, mod_ref, wout_ref, g_ref, b_ref, wr_ref, br_ref,
                 x1_ref, h2_ref, comb_ref, *, bb, tt):
    rows = bb * tt
    mix = jnp.dot(mix_ref[...].reshape(rows, MIX_WIDTH), wout_ref[...], preferred_element_type=F32)
    g1 = mod_ref[:, 2:3, :]
    sh2 = mod_ref[:, 3:4, :]
    sc2 = mod_ref[:, 4:5, :]
    r = ALPHA * x_ref[...] + (1.0 + g1) * mix.reshape(bb, tt, D_MODEL)
    x1 = _layer_norm_rows(r, g_ref[...], b_ref[...])
    x1_ref[...] = x1
    h2 = (x1 * (1.0 + sc2) + sh2).reshape(rows, D_MODEL)
    h2_ref[...] = h2.astype(BF16).reshape(bb, tt, D_MODEL)

    logits = jnp.dot(h2, wr_ref[...], preferred_element_type=F32,
                     precision=lax.Precision.HIGHEST) + br_ref[...]
    lane = lax.broadcasted_iota(jnp.int32, (rows, ROUTE_LANES), 1).astype(F32)
    far = float(ROUTE_LANES)
    is_group = (lane >= N_EXPERTS) & (lane < N_EXPERTS + N_GROUPS)
    gl = jnp.where(is_group, logits, NEG_INF)
    gmax = jnp.max(gl, -1, keepdims=True)
    gidx = jnp.min(jnp.where(is_group & (gl == gmax), lane, far), -1, keepdims=True) - N_EXPERTS
    pg = 1.0 / jnp.sum(jnp.where(is_group, jnp.exp(gl - gmax), 0.0), -1, keepdims=True)
    in_group = (lane >= gidx * EXPERTS_PER_GROUP) & (lane < (gidx + 1) * EXPERTS_PER_GROUP)
    el = jnp.where(in_group, logits, NEG_INF)
    v1 = jnp.max(el, -1, keepdims=True)
    i1 = jnp.min(jnp.where(in_group & (el == v1), lane, far), -1, keepdims=True)
    rest = in_group & (lane != i1)
    el2 = jnp.where(rest, logits, NEG_INF)
    v2 = jnp.max(el2, -1, keepdims=True)
    i2 = jnp.min(jnp.where(rest & (el2 == v2), lane, far), -1, keepdims=True)
    e2 = jnp.exp(v2 - v1)
    w1 = pg / (1.0 + e2)
    w2 = pg * e2 / (1.0 + e2)
    comb = jnp.where(lane == i1, w1, jnp.where(lane == i2, w2, 0.0))
    comb_ref[...] = comb.reshape(bb, tt, ROUTE_LANES)


def _post(x, mixin, mod, w_out, ln_g, ln_b, w_route, b_route, *, bb, tt):
    bsz, seq, d = x.shape
    assert bsz % bb == 0 and seq % tt == 0
    kern = functools.partial(_post_kernel, bb=bb, tt=tt)
    tile = lambda i, j: (i, j, 0)
    const2 = lambda i, j: (0, 0)
    return pl.pallas_call(
        kern,
        grid=(bsz // bb, seq // tt),
        in_specs=[pl.BlockSpec((bb, tt, d), tile),
                  pl.BlockSpec((bb, tt, MIX_WIDTH), tile),
                  pl.BlockSpec((bb, 6, d), lambda i, j: (i, 0, 0)),
                  pl.BlockSpec((MIX_WIDTH, d), const2),
                  pl.BlockSpec((1, d), const2),
                  pl.BlockSpec((1, d), const2),
                  pl.BlockSpec((d, ROUTE_LANES), const2),
                  pl.BlockSpec((1, ROUTE_LANES), const2)],
        out_specs=[pl.BlockSpec((bb, tt, d), tile),
                   pl.BlockSpec((bb, tt, d), tile),
                   pl.BlockSpec((bb, tt, ROUTE_LANES), tile)],
        out_shape=[jax.ShapeDtypeStruct((bsz, seq, d), F32),
                   jax.ShapeDtypeStruct((bsz, seq, d), BF16),
                   jax.ShapeDtypeStruct((bsz, seq, ROUTE_LANES), F32)],
        compiler_params=_cparams("parallel", "parallel"),
        name="post",
    )(x, mixin, mod, w_out, ln_g, ln_b, w_route, b_route)


def _moe_kernel(x1_ref, h2_ref, comb_ref, mod_ref, wgu_ref, wd_ref, g_ref, b_ref, y_ref, acc_ref, *, bb, tt):
    e = pl.program_id(2)
    rows = bb * tt

    @pl.when(e == 0)
    def _():
        acc_ref[...] = jnp.zeros_like(acc_ref)

    h2 = h2_ref[...].reshape(rows, D_MODEL)
    gu = jnp.dot(h2, wgu_ref[0], preferred_element_type=F32)
    hg = gu[:, 0:D_EXPERT]
    hu = gu[:, D_EXPERT:]
    comb = comb_ref[...].reshape(rows, ROUTE_LANES)
    lane = lax.broadcasted_iota(jnp.int32, (rows, ROUTE_LANES), 1)
    ce = jnp.sum(jnp.where(lane == e, comb, 0.0), -1, keepdims=True)
    act = (hg * jax.nn.sigmoid(hg) * hu * ce).astype(BF16)
    acc_ref[...] += jnp.dot(act, wd_ref[0], preferred_element_type=F32)

    @pl.when(e == N_EXPERTS - 1)
    def _():
        g2 = mod_ref[:, 5:6, :]
        r = ALPHA * x1_ref[...] + (1.0 + g2) * acc_ref[...].reshape(bb, tt, D_MODEL)
        y_ref[...] = _layer_norm_rows(r, g_ref[...], b_ref[...])


def _moe(x1, h2, comb, mod, w_gu, w_down, ln_g, ln_b, *, bb, tt):
    bsz, seq, d = x1.shape
    kern = functools.partial(_moe_kernel, bb=bb, tt=tt)
    tile = lambda i, j, e: (i, j, 0)
    const2 = lambda i, j, e: (0, 0)
    return pl.pallas_call(
        kern,
        grid=(bsz // bb, seq // tt, N_EXPERTS),
        in_specs=[pl.BlockSpec((bb, tt, d), tile),
                  pl.BlockSpec((bb, tt, d), tile),
                  pl.BlockSpec((bb, tt, ROUTE_LANES), tile),
                  pl.BlockSpec((bb, 6, d), lambda i, j, e: (i, 0, 0)),
                  pl.BlockSpec((1, d, 2 * D_EXPERT), lambda i, j, e: (e, 0, 0)),
                  pl.BlockSpec((1, D_EXPERT, d), lambda i, j, e: (e, 0, 0)),
                  pl.BlockSpec((1, d), const2),
                  pl.BlockSpec((1, d), const2)],
        out_specs=pl.BlockSpec((bb, tt, d), tile),
        out_shape=jax.ShapeDtypeStruct((bsz, seq, d), F32),
        scratch_shapes=[pltpu.VMEM((bb * tt, d), F32)],
        compiler_params=_cparams("parallel", "parallel", "arbitrary"),
        name="moe",
    )(x1, h2, comb, mod, w_gu, w_down, ln_g, ln_b)


def _stream(x, mod, k0, v0, u0, p, *, mixer_tm, masked, bb, tt, moe_tt):
    mixin, nk, nv, nu = _mixer(x, mod, p["w_in"], p["bias"], p["sink"], p["w_dw"], p["b_dw"],
                               p["cln_g"], p["cln_b"], k0, v0, u0, tm=mixer_tm, masked=masked)
    x1, h2, comb = _post(x, mixin, mod, p["w_out"], p["ln1_g"], p["ln1_b"], p["w_route"], p["b_route"],
                         bb=bb, tt=tt)
    y = _moe(x1, h2, comb, mod, p["w_gu"], p["w_down"], p["ln2_g"], p["ln2_b"], bb=bb, tt=moe_tt)
    bsz = x.shape[0]
    cache_shape = (1, bsz, WINDOW, N_KV_HEADS, HEAD_DIM)
    return y, nk.reshape(cache_shape), nv.reshape(cache_shape), nu[None]


def kernel(x_prompt, x_sample, cache_attn_k, cache_attn_v, state_conv, c_prompt, c_sample, rel_bias, w_ada, b_ada, w_in, attn_sinks, w_dw, b_dw, conv_ln_g, conv_ln_b, w_out, ln1_g, ln1_b, w_group, b_group, w_erouter, b_erouter, w_gate, w_up, w_down, ln2_g, ln2_b):
    bp = x_prompt.shape[0]
    bs = x_sample.shape[0]
    mod = _modulation(jnp.concatenate([c_prompt, c_sample], 0), w_ada[0], b_ada[0])
    mod = mod.reshape(bp + bs, 6, D_MODEL)

    w_route = jnp.concatenate([w_erouter[0].reshape(D_MODEL, N_EXPERTS), w_group[0]], 1)
    w_route = jnp.pad(w_route, ((0, 0), (0, ROUTE_LANES - N_EXPERTS - N_GROUPS)))
    b_route = jnp.concatenate([b_erouter[0].reshape(N_EXPERTS), b_group[0]])
    b_route = jnp.pad(b_route, (0, ROUTE_LANES - N_EXPERTS - N_GROUPS)).reshape(1, ROUTE_LANES)
    p = {
        "w_in": w_in[0].astype(BF16),
        "bias": _relative_bias(rel_bias),
        "sink": jnp.repeat(attn_sinks[0], CHUNK).reshape(N_KV_HEADS, Q_GROUP * CHUNK, 1),
        "w_dw": w_dw[0], "b_dw": b_dw[0].reshape(1, CONV_CH),
        "cln_g": conv_ln_g[0].reshape(1, CONV_CH), "cln_b": conv_ln_b[0].reshape(1, CONV_CH),
        "w_out": w_out[0].astype(BF16),
        "ln1_g": ln1_g[0].reshape(1, D_MODEL), "ln1_b": ln1_b[0].reshape(1, D_MODEL),
        "w_route": w_route, "b_route": b_route,
        "w_gu": jnp.concatenate([w_gate[0], w_up[0]], -1).astype(BF16),
        "w_down": w_down[0].astype(BF16),
        "ln2_g": ln2_g[0].reshape(1, D_MODEL), "ln2_b": ln2_b[0].reshape(1, D_MODEL),
    }

    zk = jnp.zeros((bp, WINDOW, KV_WIDTH), F32)
    zu = jnp.zeros((bp, CONV_PAD, CONV_CH), F32)
    yp, pk, pv, pc = _stream(x_prompt, mod[:bp], zk, zk, zu, p, mixer_tm=512, masked=True, bb=1, tt=512,
                             moe_tt=1024)

    k0 = cache_attn_k[0].reshape(bs, WINDOW, KV_WIDTH)
    v0 = cache_attn_v[0].reshape(bs, WINDOW, KV_WIDTH)
    u0 = jnp.pad(state_conv[0], ((0, 0), (CONV_PAD - CONV_HIST, 0), (0, 0)))
    ts = x_sample.shape[1]
    ys, sk, sv, sc = _stream(x_sample, mod[bp:], k0, v0, u0, p, mixer_tm=ts, masked=False, bb=8, tt=ts,
                             moe_tt=ts)
    return yp, ys, pk, pv, pc, sk, sv, sc
```

```python
import functools
import math

import jax
import jax.numpy as jnp
from jax import lax
from jax.experimental import pallas as pl
from jax.experimental.pallas import tpu as pltpu

D_MODEL = 1024
CHUNK = 64
N_HEADS = 8
N_KV_HEADS = 2
HEAD_DIM = 64
Q_GROUP = N_HEADS // N_KV_HEADS
ATTN_WIDTH = N_HEADS * HEAD_DIM
KV_WIDTH = N_KV_HEADS * HEAD_DIM
WINDOW = 128
WINDOW_CHUNKS = WINDOW // CHUNK
KEYS = WINDOW + CHUNK
CONV_CH = D_MODEL // 2
CONV_K = 31
CONV_HIST = CONV_K - 1
CONV_PAD = 32
MIX_WIDTH = ATTN_WIDTH + CONV_CH
IN_WIDTH = ATTN_WIDTH + 2 * KV_WIDTH + 2 * CONV_CH
N_BUCKETS = 32
MAX_DISTANCE = 128
N_GROUPS = 4
EXPERTS_PER_GROUP = 4
N_EXPERTS = N_GROUPS * EXPERTS_PER_GROUP
D_EXPERT = D_MODEL // 4
DEPTH = 1
ALPHA = (2 * DEPTH) ** 0.25
LN_EPS = 1e-5
NEG_INF = -1e30
ROUTE_LANES = 128
COMB_LO_LANE = 16
GROUP_LANE = 32
MOE_BLK = 128
VMEM_LIMIT = 56 * 1024 * 1024

BF16 = jnp.bfloat16
F32 = jnp.float32


def _cparams(*sem):
    return pltpu.CompilerParams(dimension_semantics=sem, vmem_limit_bytes=VMEM_LIMIT)


def _mod_kernel(c_ref, w_ref, b_ref, o_ref):
    c = c_ref[...]
    s = c * jax.nn.sigmoid(c)
    o_ref[...] = jnp.dot(s, w_ref[...], preferred_element_type=F32,
                         precision=lax.Precision.HIGHEST) + b_ref[...]


def _modulation(c, w_ada, b_ada):
    n, d = c.shape
    width = w_ada.shape[1]
    tn = 1536
    return pl.pallas_call(
        _mod_kernel,
        grid=(width // tn,),
        in_specs=[pl.BlockSpec((n, d), lambda j: (0, 0)),
                  pl.BlockSpec((d, tn), lambda j: (0, j)),
                  pl.BlockSpec((1, tn), lambda j: (0, j))],
        out_specs=pl.BlockSpec((n, tn), lambda j: (0, j)),
        out_shape=jax.ShapeDtypeStruct((n, width), F32),
        compiler_params=_cparams("arbitrary"),
        name="modulation",
    )(c, w_ada, b_ada.reshape(1, width))


def _t5_bucket(rel):
    nb = N_BUCKETS // 2
    max_exact = nb // 2
    ret = jnp.where(rel > 0, nb, 0)
    n = jnp.abs(rel)
    large = max_exact + (jnp.log(jnp.maximum(n, 1).astype(jnp.float32) / max_exact)
                         / math.log(MAX_DISTANCE / max_exact) * (nb - max_exact)).astype(jnp.int32)
    large = jnp.minimum(large, nb - 1)
    return ret + jnp.where(n < max_exact, n, large)


def _bias_kernel(table_ref, bucket_ref, o_ref):
    bucket = bucket_ref[...]
    for h in range(N_HEADS):
        acc = jnp.zeros(bucket.shape, F32)
        for b in range(N_BUCKETS):
            acc = jnp.where(bucket == b, table_ref[b, h], acc)
        o_ref[h] = acc


def _relative_bias(table):
    rel = jnp.arange(KEYS)[None, :] - WINDOW - jnp.arange(CHUNK)[:, None]
    bucket = _t5_bucket(rel).astype(jnp.int32)
    bias = pl.pallas_call(
        _bias_kernel,
        in_specs=[pl.BlockSpec(memory_space=pltpu.SMEM),
                  pl.BlockSpec((CHUNK, KEYS), lambda: (0, 0))],
        out_specs=pl.BlockSpec((N_HEADS, CHUNK, KEYS), lambda: (0, 0, 0)),
        out_shape=jax.ShapeDtypeStruct((N_HEADS, CHUNK, KEYS), F32),
        name="relative_bias",
    )(table, bucket)
    return bias.reshape(N_KV_HEADS, Q_GROUP * CHUNK, KEYS)


def _layer_norm_rows(x, g, b):
    mu = jnp.mean(x, -1, keepdims=True)
    xc = x - mu
    var = jnp.mean(xc * xc, -1, keepdims=True)
    return xc * lax.rsqrt(var + LN_EPS) * g + b


def _mixer_kernel(x_ref, mod_ref, win_ref, bias_ref, sink_ref, wdw_ref, bdw_ref, clg_ref, clb_ref,
                  k0_ref, v0_ref, u0_ref,
                  mix_ref, nk_ref, nv_ref, nu_ref,
                  kcat, vcat, ucat, *, tm, masked):
    t = pl.program_id(1)
    nt = pl.num_programs(1)
    n_chunks = tm // CHUNK

    @pl.when(t == 0)
    def _():
        kcat[0:WINDOW, :] = k0_ref[0].astype(BF16)
        vcat[0:WINDOW, :] = v0_ref[0].astype(BF16)
        ucat[0:CONV_PAD, :] = u0_ref[0]

    sh1 = mod_ref[0, 0:1, :]
    sc1 = mod_ref[0, 1:2, :]
    h = (x_ref[0] * (1.0 + sc1) + sh1).astype(BF16)
    o1 = ATTN_WIDTH
    o3 = o1 + 2 * KV_WIDTH
    q = (jnp.dot(h, win_ref[:, 0:o1], preferred_element_type=F32) * (HEAD_DIM ** -0.5)).astype(BF16)
    kv = jnp.dot(h, win_ref[:, o1:o3], preferred_element_type=F32)
    ag = jnp.dot(h, win_ref[:, o3:IN_WIDTH], preferred_element_type=F32)
    u = ag[:, 0:CONV_CH] * jax.nn.sigmoid(ag[:, CONV_CH:])
    kcat[WINDOW:WINDOW + tm, :] = kv[:, 0:KV_WIDTH].astype(BF16)
    vcat[WINDOW:WINDOW + tm, :] = kv[:, KV_WIDTH:].astype(BF16)
    ucat[CONV_PAD:CONV_PAD + tm, :] = u

    @pl.when(t == nt - 1)
    def _():
        if tm >= WINDOW:
            nk_ref[0] = kv[tm - WINDOW:, 0:KV_WIDTH]
            nv_ref[0] = kv[tm - WINDOW:, KV_WIDTH:]
        else:
            nk_ref[0, 0:WINDOW - tm, :] = k0_ref[0, tm:WINDOW, :]
            nv_ref[0, 0:WINDOW - tm, :] = v0_ref[0, tm:WINDOW, :]
            nk_ref[0, WINDOW - tm:, :] = kv[:, 0:KV_WIDTH]
            nv_ref[0, WINDOW - tm:, :] = kv[:, KV_WIDTH:]
        nu_ref[0] = ucat[CONV_PAD + tm - CONV_HIST:CONV_PAD + tm, :]

    rb = 64
    off = CONV_PAD - CONV_HIST
    for r in range(tm // rb):
        acc = jnp.zeros((rb, CONV_CH), F32) + bdw_ref[...]
        for k in range(CONV_K):
            acc = acc + ucat[r * rb + off + k:r * rb + off + k + rb, :] * wdw_ref[k:k + 1, :]
        y = _layer_norm_rows(acc, clg_ref[...], clb_ref[...])
        y = y * jax.nn.sigmoid(y)
        mix_ref[0, r * rb:(r + 1) * rb, ATTN_WIDTH:] = y.astype(BF16)

    gw = Q_GROUP * HEAD_DIM
    for c in range(n_chunks):
        qc = q[c * CHUNK:(c + 1) * CHUNK, :]
        if masked:
            first_valid = WINDOW - (t * n_chunks + c) * CHUNK
            valid = lax.broadcasted_iota(jnp.int32, (1, KEYS), 1) >= first_valid
        outs = []
        for kvh in range(N_KV_HEADS):
            kw = kcat[c * CHUNK:c * CHUNK + KEYS, kvh * HEAD_DIM:(kvh + 1) * HEAD_DIM]
            vw = vcat[c * CHUNK:c * CHUNK + KEYS, kvh * HEAD_DIM:(kvh + 1) * HEAD_DIM]
            qg = jnp.concatenate(
                [qc[:, kvh * gw + g * HEAD_DIM:kvh * gw + (g + 1) * HEAD_DIM] for g in range(Q_GROUP)], axis=0)
            s = lax.dot_general(qg, kw, (((1,), (1,)), ((), ())), preferred_element_type=F32)
            s = s + bias_ref[kvh]
            if masked:
                s = jnp.where(valid, s, NEG_INF)
            sink = sink_ref[kvh]
            m = jnp.maximum(jnp.max(s, -1, keepdims=True), sink)
            p = jnp.exp(s - m)
            den = jnp.sum(p, -1, keepdims=True) + jnp.exp(sink - m)
            o = jnp.dot(p.astype(BF16), vw, preferred_element_type=F32) * (1.0 / den)
            outs.extend(o[g * CHUNK:(g + 1) * CHUNK, :] for g in range(Q_GROUP))
        mix_ref[0, c * CHUNK:(c + 1) * CHUNK, 0:ATTN_WIDTH] = jnp.concatenate(outs, axis=1).astype(BF16)

    @pl.when(t < nt - 1)
    def _():
        if tm >= WINDOW:
            kcat[0:WINDOW, :] = kcat[tm:tm + WINDOW, :]
            vcat[0:WINDOW, :] = vcat[tm:tm + WINDOW, :]
            ucat[0:CONV_PAD, :] = ucat[tm:tm + CONV_PAD, :]


def _mixer(x, mod, w_in, bias, sink, w_dw, b_dw, cln_g, cln_b, k0, v0, u0, *, tm, masked):
    bsz, seq, d = x.shape
    assert seq % tm == 0 and tm % CHUNK == 0 and (tm >= WINDOW or seq == tm)
    nt = seq // tm
    kern = functools.partial(_mixer_kernel, tm=tm, masked=masked)
    const2 = lambda b, t: (0, 0)
    const3 = lambda b, t: (0, 0, 0)
    per_b = lambda b, t: (b, 0, 0)
    return pl.pallas_call(
        kern,
        grid=(bsz, nt),
        in_specs=[pl.BlockSpec((1, tm, d), lambda b, t: (b, t, 0)),
                  pl.BlockSpec((1, 6, d), per_b),
                  pl.BlockSpec((d, IN_WIDTH), const2),
                  pl.BlockSpec((N_KV_HEADS, Q_GROUP * CHUNK, KEYS), const3),
                  pl.BlockSpec((N_KV_HEADS, Q_GROUP * CHUNK, 1), const3),
                  pl.BlockSpec((CONV_K, CONV_CH), const2),
                  pl.BlockSpec((1, CONV_CH), const2),
                  pl.BlockSpec((1, CONV_CH), const2),
                  pl.BlockSpec((1, CONV_CH), const2),
                  pl.BlockSpec((1, WINDOW, KV_WIDTH), per_b),
                  pl.BlockSpec((1, WINDOW, KV_WIDTH), per_b),
                  pl.BlockSpec((1, CONV_PAD, CONV_CH), per_b)],
        out_specs=[pl.BlockSpec((1, tm, MIX_WIDTH), lambda b, t: (b, t, 0)),
                   pl.BlockSpec((1, WINDOW, KV_WIDTH), per_b),
                   pl.BlockSpec((1, WINDOW, KV_WIDTH), per_b),
                   pl.BlockSpec((1, CONV_HIST, CONV_CH), per_b)],
        out_shape=[jax.ShapeDtypeStruct((bsz, seq, MIX_WIDTH), BF16),
                   jax.ShapeDtypeStruct((bsz, WINDOW, KV_WIDTH), F32),
                   jax.ShapeDtypeStruct((bsz, WINDOW, KV_WIDTH), F32),
                   jax.ShapeDtypeStruct((bsz, CONV_HIST, CONV_CH), F32)],
        scratch_shapes=[pltpu.VMEM((WINDOW + tm, KV_WIDTH), BF16),
                        pltpu.VMEM((WINDOW + tm, KV_WIDTH), BF16),
                        pltpu.VMEM((CONV_PAD + tm, CONV_CH), F32)],
        compiler_params=_cparams("parallel", "arbitrary"),
        name="mixer",
    )(x, mod, w_in, bias, sink, w_dw, b_dw, cln_g, cln_b, k0, v0, u0)


def _post_kernel(x_ref, mix_ref, mod_ref, wout_ref, g_ref, b_ref, wr_split_ref, wr_hi_ref, br_ref,
                 x1_ref, h2e_ref, *, bb, tt):
    rows = bb * tt
    mix = jnp.dot(mix_ref[...].reshape(rows, MIX_WIDTH), wout_ref[...], preferred_element_type=F32)
    g1 = mod_ref[:, 2:3, :]
    sh2 = mod_ref[:, 3:4, :]
    sc2 = mod_ref[:, 4:5, :]
    r = ALPHA * x_ref[...] + (1.0 + g1) * mix.reshape(bb, tt, D_MODEL)
    x1 = _layer_norm_rows(r, g_ref[...], b_ref[...])
    x1_ref[...] = x1
    h2 = (x1 * (1.0 + sc2) + sh2).reshape(rows, D_MODEL)
    h2_hi = h2.astype(BF16)
    h2e_ref[:, :, 0:D_MODEL] = h2_hi.reshape(bb, tt, D_MODEL)

    h2_lo = (h2 - h2_hi.astype(F32)).astype(BF16)
    hi_terms = jnp.dot(h2_hi, wr_split_ref[...], preferred_element_type=F32)
    logits = (hi_terms[:, 0:ROUTE_LANES] + hi_terms[:, ROUTE_LANES:]
              + jnp.dot(h2_lo, wr_hi_ref[...], preferred_element_type=F32) + br_ref[...])
    lane = lax.broadcasted_iota(jnp.int32, (rows, ROUTE_LANES), 1).astype(F32)
    far = float(ROUTE_LANES)
    is_group = (lane >= N_EXPERTS) & (lane < N_EXPERTS + N_GROUPS)
    gl = jnp.where(is_group, logits, NEG_INF)
    gmax = jnp.max(gl, -1, keepdims=True)
    gidx = jnp.min(jnp.where(is_group & (gl == gmax), lane, far), -1, keepdims=True) - N_EXPERTS
    pg = 1.0 / jnp.sum(jnp.where(is_group, jnp.exp(gl - gmax), 0.0), -1, keepdims=True)
    in_group = (lane >= gidx * EXPERTS_PER_GROUP) & (lane < (gidx + 1) * EXPERTS_PER_GROUP)
    el = jnp.where(in_group, logits, NEG_INF)
    v1 = jnp.max(el, -1, keepdims=True)
    i1 = jnp.min(jnp.where(in_group & (el == v1), lane, far), -1, keepdims=True)
    rest = in_group & (lane != i1)
    el2 = jnp.where(rest, logits, NEG_INF)
    v2 = jnp.max(el2, -1, keepdims=True)
    i2 = jnp.min(jnp.where(rest & (el2 == v2), lane, far), -1, keepdims=True)
    e2 = jnp.exp(v2 - v1)
    w1 = pg / (1.0 + e2)
    w2 = pg * e2 / (1.0 + e2)
    comb = jnp.where(lane == i1, w1, jnp.where(lane == i2, w2, 0.0))

    c_hi = comb.astype(BF16).astype(F32)
    c_lo = (comb - c_hi).astype(BF16).astype(F32)
    rec = c_hi + pltpu.roll(c_lo, COMB_LO_LANE, 1) + jnp.where(lane == gidx + GROUP_LANE, 1.0, 0.0)
    h2e_ref[:, :, D_MODEL:] = rec.astype(BF16).reshape(bb, tt, ROUTE_LANES)


def _post(x, mixin, mod, w_out, ln_g, ln_b, w_route_split, w_route_hi, b_route, *, bb, tt):
    bsz, seq, d = x.shape
    assert bsz % bb == 0 and seq % tt == 0
    kern = functools.partial(_post_kernel, bb=bb, tt=tt)
    tile = lambda i, j: (i, j, 0)
    const2 = lambda i, j: (0, 0)
    return pl.pallas_call(
        kern,
        grid=(bsz // bb, seq // tt),
        in_specs=[pl.BlockSpec((bb, tt, d), tile),
                  pl.BlockSpec((bb, tt, MIX_WIDTH), tile),
                  pl.BlockSpec((bb, 6, d), lambda i, j: (i, 0, 0)),
                  pl.BlockSpec((MIX_WIDTH, d), const2),
                  pl.BlockSpec((1, d), const2),
                  pl.BlockSpec((1, d), const2),
                  pl.BlockSpec((d, 2 * ROUTE_LANES), const2),
                  pl.BlockSpec((d, ROUTE_LANES), const2),
                  pl.BlockSpec((1, ROUTE_LANES), const2)],
        out_specs=[pl.BlockSpec((bb, tt, d), tile),
                   pl.BlockSpec((bb, tt, d + ROUTE_LANES), tile)],
        out_shape=[jax.ShapeDtypeStruct((bsz, seq, d), F32),
                   jax.ShapeDtypeStruct((bsz, seq, d + ROUTE_LANES), BF16)],
        compiler_params=_cparams("parallel", "parallel"),
        name="post",
    )(x, mixin, mod, w_out, ln_g, ln_b, w_route_split, w_route_hi, b_route)


def _moe_kernel(x1_ref, h2e_ref, mod_ref, tri_ref, wgu_ref, wd_ref, g_ref, b_ref, y_ref,
                xs_ref, cw_ref, ys_ref, *, bb, tt):
    rows = bb * tt
    srows = rows + N_GROUPS * MOE_BLK

    @pl.when((pl.program_id(0) == 0) & (pl.program_id(1) == 0))
    def _():
        ys_ref[...] = jnp.zeros_like(ys_ref)

    ext = h2e_ref[...].reshape(rows, D_MODEL + ROUTE_LANES)
    rec = ext[:, D_MODEL:]
    cum = jnp.dot(tri_ref[...], rec, preferred_element_type=F32)
    lane1 = lax.broadcasted_iota(jnp.int32, (1, ROUTE_LANES), 1)
    cnt = cum[rows - 1:rows, :]
    off = jnp.int32(0)
    offs = []
    off_vec = jnp.zeros((1, ROUTE_LANES), F32)
    for g in range(N_GROUPS):
        n_g = jnp.sum(jnp.where(lane1 == GROUP_LANE + g, cnt, 0.0)).astype(jnp.int32)
        offs.append(off)
        off_vec = jnp.where(lane1 == GROUP_LANE + g, off.astype(F32), off_vec)
        off = off + ((n_g + (MOE_BLK - 1)) // MOE_BLK) * MOE_BLK

    lane = lax.broadcasted_iota(jnp.int32, (rows, ROUTE_LANES), 1)
    mine = (lane >= GROUP_LANE) & (lane < GROUP_LANE + N_GROUPS) & (rec.astype(F32) > 0.0)
    pos = jnp.sum(jnp.where(mine, cum - 1.0 + off_vec, 0.0), -1, keepdims=True)
    scatter = jnp.where(pos == lax.broadcasted_iota(jnp.int32, (rows, srows), 1).astype(F32), 1.0, 0.0).astype(BF16)
    pos_row = jnp.transpose(jnp.broadcast_to(pos, (rows, ROUTE_LANES)))[0:1, :]
    gather = jnp.where(lax.broadcasted_iota(jnp.int32, (srows, rows), 0).astype(F32) == pos_row, 1.0, 0.0).astype(BF16)
    xs_ref[...] = jnp.dot(gather, ext[:, 0:D_MODEL], preferred_element_type=F32).astype(BF16)
    rs = jnp.dot(gather, rec, preferred_element_type=F32)
    cw_ref[...] = rs + pltpu.roll(rs, ROUTE_LANES - COMB_LO_LANE, 1)

    def block(i, carry):
        r0 = pl.multiple_of(i * MOE_BLK, MOE_BLK)
        g = ((r0 >= offs[1]).astype(jnp.int32) + (r0 >= offs[2]).astype(jnp.int32)
             + (r0 >= offs[3]).astype(jnp.int32))
        gu = jnp.dot(xs_ref[pl.ds(r0, MOE_BLK), :], wgu_ref[g], preferred_element_type=F32)
        cwb = cw_ref[pl.ds(r0, MOE_BLK), :]
        lane_b = lax.broadcasted_iota(jnp.int32, (MOE_BLK, ROUTE_LANES), 1)
        width = EXPERTS_PER_GROUP * D_EXPERT
        parts = []
        for j in range(EXPERTS_PER_GROUP):
            cj = jnp.sum(jnp.where(lane_b == g * EXPERTS_PER_GROUP + j, cwb, 0.0), -1, keepdims=True)
            hg = gu[:, j * D_EXPERT:(j + 1) * D_EXPERT]
            hu = gu[:, width + j * D_EXPERT:width + (j + 1) * D_EXPERT]
            parts.append((hg * jax.nn.sigmoid(hg) * hu * cj).astype(BF16))
        act = jnp.concatenate(parts, axis=1)
        ys_ref[pl.ds(r0, MOE_BLK), :] = jnp.dot(act, wd_ref[g], preferred_element_type=F32).astype(BF16)
        return carry

    lax.fori_loop(0, off // MOE_BLK, block, 0)

    ff = jnp.dot(scatter, ys_ref[...], preferred_element_type=F32)
    g2 = mod_ref[:, 5:6, :]
    r = ALPHA * x1_ref[...] + (1.0 + g2) * ff.reshape(bb, tt, D_MODEL)
    y_ref[...] = _layer_norm_rows(r, g_ref[...], b_ref[...])


def _moe(x1, h2e, mod, w_gu, w_down, ln_g, ln_b, *, bb, tt):
    bsz, seq, d = x1.shape
    rows = bb * tt
    srows = rows + N_GROUPS * MOE_BLK
    tri = jnp.tril(jnp.ones((rows, rows), BF16))
    kern = functools.partial(_moe_kernel, bb=bb, tt=tt)
    tile = lambda i, j: (i, j, 0)
    const2 = lambda i, j: (0, 0)
    const3 = lambda i, j: (0, 0, 0)
    resident = pl.Buffered(1)
    return pl.pallas_call(
        kern,
        grid=(bsz // bb, seq // tt),
        in_specs=[pl.BlockSpec((bb, tt, d), tile),
                  pl.BlockSpec((bb, tt, d + ROUTE_LANES), tile),
                  pl.BlockSpec((bb, 6, d), lambda i, j: (i, 0, 0)),
                  pl.BlockSpec((rows, rows), const2, pipeline_mode=resident),
                  pl.BlockSpec(w_gu.shape, const3, pipeline_mode=resident),
                  pl.BlockSpec(w_down.shape, const3, pipeline_mode=resident),
                  pl.BlockSpec((1, d), const2),
                  pl.BlockSpec((1, d), const2)],
        out_specs=pl.BlockSpec((bb, tt, d), tile),
        out_shape=jax.ShapeDtypeStruct((bsz, seq, d), F32),
        scratch_shapes=[pltpu.VMEM((srows, d), BF16),
                        pltpu.VMEM((srows, ROUTE_LANES), F32),
                        pltpu.VMEM((srows, d), BF16)],
        compiler_params=_cparams("arbitrary", "arbitrary"),
        name="moe",
    )(x1, h2e, mod, tri, w_gu, w_down, ln_g, ln_b)


def _stream(x, mod, k0, v0, u0, p, *, mixer_tm, masked, bb, tt, moe_tt):
    mixin, nk, nv, nu = _mixer(x, mod, p["w_in"], p["bias"], p["sink"], p["w_dw"], p["b_dw"],
                               p["cln_g"], p["cln_b"], k0, v0, u0, tm=mixer_tm, masked=masked)
    x1, h2e = _post(x, mixin, mod, p["w_out"], p["ln1_g"], p["ln1_b"], p["w_route_split"], p["w_route_hi"],
                    p["b_route"], bb=bb, tt=tt)
    y = _moe(x1, h2e, mod, p["w_gu"], p["w_down"], p["ln2_g"], p["ln2_b"], bb=bb, tt=moe_tt)
    bsz = x.shape[0]
    cache_shape = (1, bsz, WINDOW, N_KV_HEADS, HEAD_DIM)
    return y, nk.reshape(cache_shape), nv.reshape(cache_shape), nu[None]


def kernel(x_prompt, x_sample, cache_attn_k, cache_attn_v, state_conv, c_prompt, c_sample, rel_bias, w_ada, b_ada, w_in, attn_sinks, w_dw, b_dw, conv_ln_g, conv_ln_b, w_out, ln1_g, ln1_b, w_group, b_group, w_erouter, b_erouter, w_gate, w_up, w_down, ln2_g, ln2_b):
    bp = x_prompt.shape[0]
    bs = x_sample.shape[0]
    mod = _modulation(jnp.concatenate([c_prompt, c_sample], 0), w_ada[0], b_ada[0])
    mod = mod.reshape(bp + bs, 6, D_MODEL)

    w_route = jnp.concatenate([w_erouter[0].reshape(D_MODEL, N_EXPERTS), w_group[0]], 1)
    w_route = jnp.pad(w_route, ((0, 0), (0, ROUTE_LANES - N_EXPERTS - N_GROUPS)))
    b_route = jnp.concatenate([b_erouter[0].reshape(N_EXPERTS), b_group[0]])
    b_route = jnp.pad(b_route, (0, ROUTE_LANES - N_EXPERTS - N_GROUPS)).reshape(1, ROUTE_LANES)
    w_route_hi = w_route.astype(BF16)
    w_route_lo = (w_route - w_route_hi.astype(F32)).astype(BF16)

    def group_cols(w):
        w = w.reshape(N_GROUPS, EXPERTS_PER_GROUP, D_MODEL, D_EXPERT)
        return jnp.transpose(w, (0, 2, 1, 3)).reshape(N_GROUPS, D_MODEL, EXPERTS_PER_GROUP * D_EXPERT)

    p = {
        "w_in": w_in[0].astype(BF16),
        "bias": _relative_bias(rel_bias),
        "sink": jnp.repeat(attn_sinks[0], CHUNK).reshape(N_KV_HEADS, Q_GROUP * CHUNK, 1),
        "w_dw": w_dw[0], "b_dw": b_dw[0].reshape(1, CONV_CH),
        "cln_g": conv_ln_g[0].reshape(1, CONV_CH), "cln_b": conv_ln_b[0].reshape(1, CONV_CH),
        "w_out": w_out[0].astype(BF16),
        "ln1_g": ln1_g[0].reshape(1, D_MODEL), "ln1_b": ln1_b[0].reshape(1, D_MODEL),
        "w_route_split": jnp.concatenate([w_route_hi, w_route_lo], 1), "w_route_hi": w_route_hi,
        "b_route": b_route,
        "w_gu": jnp.concatenate([group_cols(w_gate[0]), group_cols(w_up[0])], -1).astype(BF16),
        "w_down": w_down[0].reshape(N_GROUPS, EXPERTS_PER_GROUP * D_EXPERT, D_MODEL).astype(BF16),
        "ln2_g": ln2_g[0].reshape(1, D_MODEL), "ln2_b": ln2_b[0].reshape(1, D_MODEL),
    }

    zk = jnp.zeros((bp, WINDOW, KV_WIDTH), F32)
    zu = jnp.zeros((bp, CONV_PAD, CONV_CH), F32)
    yp, pk, pv, pc = _stream(x_prompt, mod[:bp], zk, zk, zu, p, mixer_tm=512, masked=True, bb=1, tt=512,
                             moe_tt=512)

    k0 = cache_attn_k[0].reshape(bs, WINDOW, KV_WIDTH)
    v0 = cache_attn_v[0].reshape(bs, WINDOW, KV_WIDTH)
    u0 = jnp.pad(state_conv[0], ((0, 0), (CONV_PAD - CONV_HIST, 0), (0, 0)))
    ts = x_sample.shape[1]
    ys, sk, sv, sc = _stream(x_sample, mod[bp:], k0, v0, u0, p, mixer_tm=ts, masked=False, bb=8, tt=ts,
                             moe_tt=ts)
    return yp, ys, pk, pv, pc, sk, sv, sc
```

```python
import functools
import math

import jax
import jax.numpy as jnp
from jax import lax
from jax.experimental import pallas as pl
from jax.experimental.pallas import tpu as pltpu

D_MODEL = 1024
CHUNK = 64
N_HEADS = 8
N_KV_HEADS = 2
HEAD_DIM = 64
Q_GROUP = N_HEADS // N_KV_HEADS
ATTN_WIDTH = N_HEADS * HEAD_DIM
KV_WIDTH = N_KV_HEADS * HEAD_DIM
WINDOW = 128
WINDOW_CHUNKS = WINDOW // CHUNK
KEYS = WINDOW + CHUNK
CONV_CH = D_MODEL // 2
CONV_K = 31
CONV_HIST = CONV_K - 1
CONV_PAD = 32
SUBLANES = 8
CONV_ROWS = 64
CONV_LANES = 128
MIX_WIDTH = ATTN_WIDTH + CONV_CH
IN_WIDTH = ATTN_WIDTH + 2 * KV_WIDTH + 2 * CONV_CH
N_BUCKETS = 32
MAX_DISTANCE = 128
N_GROUPS = 4
EXPERTS_PER_GROUP = 4
N_EXPERTS = N_GROUPS * EXPERTS_PER_GROUP
D_EXPERT = D_MODEL // 4
DEPTH = 1
ALPHA = (2 * DEPTH) ** 0.25
LN_EPS = 1e-5
NEG_INF = -1e30
ROUTE_LANES = 128
COMB_LO_LANE = 16
GROUP_LANE = 32
MOE_BLK = 128
VMEM_LIMIT = 56 * 1024 * 1024

BF16 = jnp.bfloat16
F32 = jnp.float32


def _cparams(*sem):
    return pltpu.CompilerParams(dimension_semantics=sem, vmem_limit_bytes=VMEM_LIMIT)


def _mod_kernel(c_ref, w_ref, b_ref, o_ref):
    c = c_ref[...]
    s = c * jax.nn.sigmoid(c)
    o_ref[...] = jnp.dot(s, w_ref[...], preferred_element_type=F32,
                         precision=lax.Precision.HIGHEST) + b_ref[...]


def _modulation(c, w_ada, b_ada):
    n, d = c.shape
    width = w_ada.shape[1]
    tn = 1536
    return pl.pallas_call(
        _mod_kernel,
        grid=(width // tn,),
        in_specs=[pl.BlockSpec((n, d), lambda j: (0, 0)),
                  pl.BlockSpec((d, tn), lambda j: (0, j)),
                  pl.BlockSpec((1, tn), lambda j: (0, j))],
        out_specs=pl.BlockSpec((n, tn), lambda j: (0, j)),
        out_shape=jax.ShapeDtypeStruct((n, width), F32),
        compiler_params=_cparams("arbitrary"),
        name="modulation",
    )(c, w_ada, b_ada.reshape(1, width))


def _t5_bucket(rel):
    nb = N_BUCKETS // 2
    max_exact = nb // 2
    ret = jnp.where(rel > 0, nb, 0)
    n = jnp.abs(rel)
    large = max_exact + (jnp.log(jnp.maximum(n, 1).astype(jnp.float32) / max_exact)
                         / math.log(MAX_DISTANCE / max_exact) * (nb - max_exact)).astype(jnp.int32)
    large = jnp.minimum(large, nb - 1)
    return ret + jnp.where(n < max_exact, n, large)


def _bias_kernel(table_ref, bucket_ref, o_ref):
    bucket = bucket_ref[...]
    col = lax.broadcasted_iota(jnp.int32, bucket.shape, 1)
    for h in range(N_HEADS):
        acc = jnp.zeros(bucket.shape, F32)
        for b in range(N_BUCKETS):
            acc = jnp.where(bucket == b, table_ref[b, h], acc)
        for v in range(WINDOW_CHUNKS + 1):
            o_ref[v, h] = jnp.where(col >= v * CHUNK, acc, NEG_INF)


def _relative_bias(table):
    rel = jnp.arange(KEYS)[None, :] - WINDOW - jnp.arange(CHUNK)[:, None]
    bucket = _t5_bucket(rel).astype(jnp.int32)
    nv = WINDOW_CHUNKS + 1
    bias = pl.pallas_call(
        _bias_kernel,
        in_specs=[pl.BlockSpec(memory_space=pltpu.SMEM),
                  pl.BlockSpec((CHUNK, KEYS), lambda: (0, 0))],
        out_specs=pl.BlockSpec((nv, N_HEADS, CHUNK, KEYS), lambda: (0, 0, 0, 0)),
        out_shape=jax.ShapeDtypeStruct((nv, N_HEADS, CHUNK, KEYS), F32),
        name="relative_bias",
    )(table, bucket)
    bias = bias.reshape(nv, N_HEADS // 2, 2, CHUNK, KEYS)
    return jnp.transpose(bias, (0, 1, 3, 2, 4)).reshape(nv, N_HEADS // 2, CHUNK, 2 * KEYS)


def _layer_norm_rows(x, g, b):
    mu = jnp.mean(x, -1, keepdims=True)
    xc = x - mu
    var = jnp.mean(xc * xc, -1, keepdims=True)
    return xc * lax.rsqrt(var + LN_EPS) * g + b


def _mixer_kernel(x_ref, mod_ref, win_ref, bias_ref, sink_ref, wdw_ref, bdw_ref, clg_ref, clb_ref,
                  k0_ref, v0_ref, u0_ref,
                  mix_ref, nk_ref, nv_ref, nu_ref,
                  kz, vz, ush, yconv, s_buf, p_buf, e_buf, *, tm, masked):
    t = pl.program_id(1)
    nt = pl.num_programs(1)
    n_chunks = tm // CHUNK
    ucat = ush.at[0]

    left = lax.broadcasted_iota(jnp.int32, (1, KV_WIDTH), 1) < HEAD_DIM

    def store_kv(row0, k, v):
        n = k.shape[0]
        for dst, val in ((kz, k), (vz, v)):
            swapped = pltpu.roll(val, HEAD_DIM, 1)
            dst[0, row0:row0 + n, 0:KV_WIDTH] = jnp.where(left, val, 0.0).astype(BF16)
            dst[1, row0:row0 + n, 0:KV_WIDTH] = jnp.where(left, 0.0, swapped).astype(BF16)
            dst[2, row0:row0 + n, 0:KV_WIDTH] = jnp.where(left, swapped, 0.0).astype(BF16)
            dst[3, row0:row0 + n, 0:KV_WIDTH] = jnp.where(left, 0.0, val).astype(BF16)

    @pl.when(t == 0)
    def _():
        ones_left = jnp.broadcast_to(jnp.where(left, 1.0, 0.0).astype(BF16), (WINDOW + tm, KV_WIDTH))
        ones_right = jnp.broadcast_to(jnp.where(left, 0.0, 1.0).astype(BF16), (WINDOW + tm, KV_WIDTH))
        for i in range(2 * N_KV_HEADS):
            vz[i, :, KV_WIDTH:] = ones_left if i % 2 == 0 else ones_right
        store_kv(0, k0_ref[0], v0_ref[0])
        ucat[0:CONV_PAD, :] = u0_ref[0]

    sh1 = mod_ref[0, 0:1, :]
    sc1 = mod_ref[0, 1:2, :]
    h = (x_ref[0] * (1.0 + sc1) + sh1).astype(BF16)
    o1 = ATTN_WIDTH
    o3 = o1 + 2 * KV_WIDTH
    q = (jnp.dot(h, win_ref[:, 0:o1], preferred_element_type=F32) * (HEAD_DIM ** -0.5)).astype(BF16)
    kv = jnp.dot(h, win_ref[:, o1:o3], preferred_element_type=F32)
    ag = jnp.dot(h, win_ref[:, o3:IN_WIDTH], preferred_element_type=F32)
    u = ag[:, 0:CONV_CH] * jax.nn.sigmoid(ag[:, CONV_CH:])
    store_kv(WINDOW, kv[:, 0:KV_WIDTH], kv[:, KV_WIDTH:])
    ucat[CONV_PAD:CONV_PAD + tm, :] = u

    @pl.when(t == nt - 1)
    def _():
        if tm >= WINDOW:
            nk_ref[0] = kv[tm - WINDOW:, 0:KV_WIDTH]
            nv_ref[0] = kv[tm - WINDOW:, KV_WIDTH:]
        else:
            nk_ref[0, 0:WINDOW - tm, :] = k0_ref[0, tm:WINDOW, :]
            nv_ref[0, 0:WINDOW - tm, :] = v0_ref[0, tm:WINDOW, :]
            nk_ref[0, WINDOW - tm:, :] = kv[:, 0:KV_WIDTH]
            nv_ref[0, WINDOW - tm:, :] = kv[:, KV_WIDTH:]
        nu_ref[0] = ucat[CONV_PAD + tm - CONV_HIST:CONV_PAD + tm, :]

    span = tm + CONV_PAD - SUBLANES
    for s in range(1, SUBLANES):
        ush[s, 0:span, :] = ucat[s:s + span, :]
    off = CONV_PAD - CONV_HIST
    rb = min(CONV_ROWS, tm)
    groups = rb // SUBLANES

    def conv_rows(lanes, r, carry):
        r0 = pl.multiple_of(r * rb, rb)
        partial = []
        for shift in range(SUBLANES):
            taps = [k for k in range(CONV_K) if (off + k) % SUBLANES == shift]
            top = max(off + k - shift for k in taps)
            slab = ush[shift, pl.ds(r0, rb + top), lanes].reshape(groups + top // SUBLANES, SUBLANES, CONV_LANES)
            acc = None
            for k in taps:
                g0 = (off + k - shift) // SUBLANES
                term = slab[g0:g0 + groups] * wdw_ref[k, :, lanes]
                acc = term if acc is None else acc + term
            partial.append(acc)
        while len(partial) > 1:
            partial = [a + b for a, b in zip(partial[0::2], partial[1::2])]
        yconv[pl.ds(r0, rb), lanes] = (partial[0] + bdw_ref[:, lanes]).reshape(rb, CONV_LANES)
        return carry

    for part in range(CONV_CH // CONV_LANES):
        lanes = slice(part * CONV_LANES, (part + 1) * CONV_LANES)
        lax.fori_loop(0, tm // rb, functools.partial(conv_rows, lanes), 0, unroll=min(2, tm // rb))
    y = _layer_norm_rows(yconv[...], clg_ref[...], clb_ref[...])
    mix_ref[0, :, ATTN_WIDTH:] = (y * jax.nn.sigmoid(y)).astype(BF16)

    n_pairs = N_HEADS // 2
    pair_w = 2 * HEAD_DIM
    first_head = lax.broadcasted_iota(jnp.int32, (CHUNK, pair_w), 1) < HEAD_DIM
    nt_dims = (((1,), (1,)), ((), ()))

    def window(ref, j, c):
        kvh = (2 * j) // Q_GROUP
        rows = slice(c * CHUNK, c * CHUNK + KEYS)
        return jnp.concatenate([ref[2 * kvh, rows, :], ref[2 * kvh + 1, rows, :]], axis=0)

    for c in range(n_chunks):
        if masked and c < WINDOW_CHUNKS:
            variant = jnp.where(t == 0, WINDOW_CHUNKS - c, 0)
        else:
            variant = 0
        for j in range(n_pairs):
            qp = q[c * CHUNK:(c + 1) * CHUNK, j * pair_w:(j + 1) * pair_w]
            s_buf[c, j] = (lax.dot_general(qp, window(kz, j, c), nt_dims, preferred_element_type=F32)
                           + bias_ref[variant, j])

    for c in range(n_chunks):
        for j in range(n_pairs):
            s0 = s_buf[c, j, :, 0:pair_w]
            s1 = s_buf[c, j, :, pair_w:2 * pair_w]
            s2 = s_buf[c, j, :, 2 * pair_w:]
            sink_a = sink_ref[2 * j]
            sink_b = sink_ref[2 * j + 1]
            m_a = jnp.maximum(jnp.max(jnp.maximum(s0, jnp.where(first_head, s1, NEG_INF)), -1, keepdims=True), sink_a)
            m_b = jnp.maximum(jnp.max(jnp.maximum(jnp.where(first_head, NEG_INF, s1), s2), -1, keepdims=True), sink_b)
            p_buf[c, j, :, 0:pair_w] = jnp.exp(s0 - m_a).astype(BF16)
            p_buf[c, j, :, pair_w:2 * pair_w] = jnp.exp(s1 - jnp.where(first_head, m_a, m_b)).astype(BF16)
            p_buf[c, j, :, 2 * pair_w:] = jnp.exp(s2 - m_b).astype(BF16)
            e_buf[c, j] = jnp.where(first_head, jnp.exp(sink_a - m_a), jnp.exp(sink_b - m_b))

    for c in range(n_chunks):
        for j in range(n_pairs):
            res = jnp.dot(p_buf[c, j], window(vz, j, c), preferred_element_type=F32)
            out = res[:, 0:pair_w] * (1.0 / (res[:, pair_w:] + e_buf[c, j]))
            mix_ref[0, c * CHUNK:(c + 1) * CHUNK, j * pair_w:(j + 1) * pair_w] = out.astype(BF16)

    @pl.when(t < nt - 1)
    def _():
        if tm >= WINDOW:
            for i in range(2 * N_KV_HEADS):
                kz[i, 0:WINDOW, :] = kz[i, tm:tm + WINDOW, :]
                vz[i, 0:WINDOW, 0:KV_WIDTH] = vz[i, tm:tm + WINDOW, 0:KV_WIDTH]
            ucat[0:CONV_PAD, :] = ucat[tm:tm + CONV_PAD, :]


def _mixer(x, mod, w_in, bias, sink, w_dw, b_dw, cln_g, cln_b, k0, v0, u0, *, tm, masked):
    bsz, seq, d = x.shape
    assert seq % tm == 0 and tm % CHUNK == 0 and (tm >= WINDOW or seq == tm)
    nt = seq // tm
    kern = functools.partial(_mixer_kernel, tm=tm, masked=masked)
    const2 = lambda b, t: (0, 0)
    const3 = lambda b, t: (0, 0, 0)
    per_b = lambda b, t: (b, 0, 0)
    return pl.pallas_call(
        kern,
        grid=(bsz, nt),
        in_specs=[pl.BlockSpec((1, tm, d), lambda b, t: (b, t, 0)),
                  pl.BlockSpec((1, 6, d), per_b),
                  pl.BlockSpec((d, IN_WIDTH), const2),
                  pl.BlockSpec((WINDOW_CHUNKS + 1, N_HEADS // 2, CHUNK, 2 * KEYS), lambda b, t: (0, 0, 0, 0)),
                  pl.BlockSpec(memory_space=pltpu.SMEM),
                  pl.BlockSpec((CONV_K, SUBLANES, CONV_CH), const3),
                  pl.BlockSpec((1, CONV_CH), const2),
                  pl.BlockSpec((1, CONV_CH), const2),
                  pl.BlockSpec((1, CONV_CH), const2),
                  pl.BlockSpec((1, WINDOW, KV_WIDTH), per_b),
                  pl.BlockSpec((1, WINDOW, KV_WIDTH), per_b),
                  pl.BlockSpec((1, CONV_PAD, CONV_CH), per_b)],
        out_specs=[pl.BlockSpec((1, tm, MIX_WIDTH), lambda b, t: (b, t, 0)),
                   pl.BlockSpec((1, WINDOW, KV_WIDTH), per_b),
                   pl.BlockSpec((1, WINDOW, KV_WIDTH), per_b),
                   pl.BlockSpec((1, CONV_HIST, CONV_CH), per_b)],
        out_shape=[jax.ShapeDtypeStruct((bsz, seq, MIX_WIDTH), BF16),
                   jax.ShapeDtypeStruct((bsz, WINDOW, KV_WIDTH), F32),
                   jax.ShapeDtypeStruct((bsz, WINDOW, KV_WIDTH), F32),
                   jax.ShapeDtypeStruct((bsz, CONV_HIST, CONV_CH), F32)],
        scratch_shapes=[pltpu.VMEM((2 * N_KV_HEADS, WINDOW + tm, KV_WIDTH), BF16),
                        pltpu.VMEM((2 * N_KV_HEADS, WINDOW + tm, 2 * KV_WIDTH), BF16),
                        pltpu.VMEM((SUBLANES, CONV_PAD + tm, CONV_CH), F32),
                        pltpu.VMEM((tm, CONV_CH), F32),
                        pltpu.VMEM((tm // CHUNK, N_HEADS // 2, CHUNK, 2 * KEYS), F32),
                        pltpu.VMEM((tm // CHUNK, N_HEADS // 2, CHUNK, 2 * KEYS), BF16),
                        pltpu.VMEM((tm // CHUNK, N_HEADS // 2, CHUNK, 2 * HEAD_DIM), F32)],
        compiler_params=_cparams("parallel", "arbitrary"),
        name="mixer",
    )(x, mod, w_in, bias, sink, w_dw, b_dw, cln_g, cln_b, k0, v0, u0)


def _post_kernel(x_ref, mix_ref, mod_ref, wout_ref, g_ref, b_ref, wr_split_ref, wr_hi_ref, br_ref,
                 x1_ref, h2e_ref, *, bb, tt):
    rows = bb * tt
    mix = jnp.dot(mix_ref[...].reshape(rows, MIX_WIDTH), wout_ref[...], preferred_element_type=F32)
    g1 = mod_ref[:, 2:3, :]
    sh2 = mod_ref[:, 3:4, :]
    sc2 = mod_ref[:, 4:5, :]
    r = ALPHA * x_ref[...] + (1.0 + g1) * mix.reshape(bb, tt, D_MODEL)
    x1 = _layer_norm_rows(r, g_ref[...], b_ref[...])
    x1_ref[...] = x1
    h2 = (x1 * (1.0 + sc2) + sh2).reshape(rows, D_MODEL)
    h2_hi = h2.astype(BF16)
    h2e_ref[:, :, 0:D_MODEL] = h2_hi.reshape(bb, tt, D_MODEL)

    h2_lo = (h2 - h2_hi.astype(F32)).astype(BF16)
    hi_terms = jnp.dot(h2_hi, wr_split_ref[...], preferred_element_type=F32)
    logits = (hi_terms[:, 0:ROUTE_LANES] + hi_terms[:, ROUTE_LANES:]
              + jnp.dot(h2_lo, wr_hi_ref[...], preferred_element_type=F32) + br_ref[...])
    lane = lax.broadcasted_iota(jnp.int32, (rows, ROUTE_LANES), 1).astype(F32)
    far = float(ROUTE_LANES)
    is_group = (lane >= N_EXPERTS) & (lane < N_EXPERTS + N_GROUPS)
    gl = jnp.where(is_group, logits, NEG_INF)
    gmax = jnp.max(gl, -1, keepdims=True)
    gidx = jnp.min(jnp.where(is_group & (gl == gmax), lane, far), -1, keepdims=True) - N_EXPERTS
    pg = 1.0 / jnp.sum(jnp.where(is_group, jnp.exp(gl - gmax), 0.0), -1, keepdims=True)
    in_group = (lane >= gidx * EXPERTS_PER_GROUP) & (lane < (gidx + 1) * EXPERTS_PER_GROUP)
    el = jnp.where(in_group, logits, NEG_INF)
    v1 = jnp.max(el, -1, keepdims=True)
    i1 = jnp.min(jnp.where(in_group & (el == v1), lane, far), -1, keepdims=True)
    rest = in_group & (lane != i1)
    el2 = jnp.where(rest, logits, NEG_INF)
    v2 = jnp.max(el2, -1, keepdims=True)
    i2 = jnp.min(jnp.where(rest & (el2 == v2), lane, far), -1, keepdims=True)
    e2 = jnp.exp(v2 - v1)
    w1 = pg / (1.0 + e2)
    w2 = pg * e2 / (1.0 + e2)
    comb = jnp.where(lane == i1, w1, jnp.where(lane == i2, w2, 0.0))

    c_hi = comb.astype(BF16).astype(F32)
    c_lo = (comb - c_hi).astype(BF16).astype(F32)
    rec = c_hi + pltpu.roll(c_lo, COMB_LO_LANE, 1) + jnp.where(lane == gidx + GROUP_LANE, 1.0, 0.0)
    h2e_ref[:, :, D_MODEL:] = rec.astype(BF16).reshape(bb, tt, ROUTE_LANES)


def _post(x, mixin, mod, w_out, ln_g, ln_b, w_route_split, w_route_hi, b_route, *, bb, tt):
    bsz, seq, d = x.shape
    assert bsz % bb == 0 and seq % tt == 0
    kern = functools.partial(_post_kernel, bb=bb, tt=tt)
    tile = lambda i, j: (i, j, 0)
    const2 = lambda i, j: (0, 0)
    return pl.pallas_call(
        kern,
        grid=(bsz // bb, seq // tt),
        in_specs=[pl.BlockSpec((bb, tt, d), tile),
                  pl.BlockSpec((bb, tt, MIX_WIDTH), tile),
                  pl.BlockSpec((bb, 6, d), lambda i, j: (i, 0, 0)),
                  pl.BlockSpec((MIX_WIDTH, d), const2),
                  pl.BlockSpec((1, d), const2),
                  pl.BlockSpec((1, d), const2),
                  pl.BlockSpec((d, 2 * ROUTE_LANES), const2),
                  pl.BlockSpec((d, ROUTE_LANES), const2),
                  pl.BlockSpec((1, ROUTE_LANES), const2)],
        out_specs=[pl.BlockSpec((bb, tt, d), tile),
                   pl.BlockSpec((bb, tt, d + ROUTE_LANES), tile)],
        out_shape=[jax.ShapeDtypeStruct((bsz, seq, d), F32),
                   jax.ShapeDtypeStruct((bsz, seq, d + ROUTE_LANES), BF16)],
        compiler_params=_cparams("parallel", "parallel"),
        name="post",
    )(x, mixin, mod, w_out, ln_g, ln_b, w_route_split, w_route_hi, b_route)


def _moe_kernel(x1_ref, h2e_ref, mod_ref, tri_ref, wgu_ref, wd_ref, g_ref, b_ref, y_ref,
                xs_ref, cw_ref, ys_ref, *, bb, tt):
    rows = bb * tt
    srows = rows + N_GROUPS * MOE_BLK

    @pl.when((pl.program_id(0) == 0) & (pl.program_id(1) == 0))
    def _():
        ys_ref[...] = jnp.zeros_like(ys_ref)

    ext = h2e_ref[...].reshape(rows, D_MODEL + ROUTE_LANES)
    rec = ext[:, D_MODEL:]
    cum = jnp.dot(tri_ref[...], rec, preferred_element_type=F32)
    lane1 = lax.broadcasted_iota(jnp.int32, (1, ROUTE_LANES), 1)
    cnt = cum[rows - 1:rows, :]
    off = jnp.int32(0)
    offs = []
    off_vec = jnp.zeros((1, ROUTE_LANES), F32)
    for g in range(N_GROUPS):
        n_g = jnp.sum(jnp.where(lane1 == GROUP_LANE + g, cnt, 0.0)).astype(jnp.int32)
        offs.append(off)
        off_vec = jnp.where(lane1 == GROUP_LANE + g, off.astype(F32), off_vec)
        off = off + ((n_g + (MOE_BLK - 1)) // MOE_BLK) * MOE_BLK

    lane = lax.broadcasted_iota(jnp.int32, (rows, ROUTE_LANES), 1)
    mine = (lane >= GROUP_LANE) & (lane < GROUP_LANE + N_GROUPS) & (rec.astype(F32) > 0.0)
    pos = jnp.sum(jnp.where(mine, cum - 1.0 + off_vec, 0.0), -1, keepdims=True)
    scatter = jnp.where(pos == lax.broadcasted_iota(jnp.int32, (rows, srows), 1).astype(F32), 1.0, 0.0).astype(BF16)
    pos_row = jnp.transpose(jnp.broadcast_to(pos, (rows, ROUTE_LANES)))[0:1, :]
    gather = jnp.where(lax.broadcasted_iota(jnp.int32, (srows, rows), 0).astype(F32) == pos_row, 1.0, 0.0).astype(BF16)
    xs_ref[...] = jnp.dot(gather, ext[:, 0:D_MODEL], preferred_element_type=F32).astype(BF16)
    rs = jnp.dot(gather, rec, preferred_element_type=F32)
    cw_ref[...] = rs + pltpu.roll(rs, ROUTE_LANES - COMB_LO_LANE, 1)

    def block(i, carry):
        r0 = pl.multiple_of(i * MOE_BLK, MOE_BLK)
        g = ((r0 >= offs[1]).astype(jnp.int32) + (r0 >= offs[2]).astype(jnp.int32)
             + (r0 >= offs[3]).astype(jnp.int32))
        gu = jnp.dot(xs_ref[pl.ds(r0, MOE_BLK), :], wgu_ref[g], preferred_element_type=F32)
        cwb = cw_ref[pl.ds(r0, MOE_BLK), :]
        lane_b = lax.broadcasted_iota(jnp.int32, (MOE_BLK, ROUTE_LANES), 1)
        width = EXPERTS_PER_GROUP * D_EXPERT
        parts = []
        for j in range(EXPERTS_PER_GROUP):
            cj = jnp.sum(jnp.where(lane_b == g * EXPERTS_PER_GROUP + j, cwb, 0.0), -1, keepdims=True)
            hg = gu[:, j * D_EXPERT:(j + 1) * D_EXPERT]
            hu = gu[:, width + j * D_EXPERT:width + (j + 1) * D_EXPERT]
            parts.append((hg * jax.nn.sigmoid(hg) * hu * cj).astype(BF16))
        act = jnp.concatenate(parts, axis=1)
        ys_ref[pl.ds(r0, MOE_BLK), :] = jnp.dot(act, wd_ref[g], preferred_element_type=F32).astype(BF16)
        return carry

    lax.fori_loop(0, off // MOE_BLK, block, 0)

    ff = jnp.dot(scatter, ys_ref[...], preferred_element_type=F32)
    g2 = mod_ref[:, 5:6, :]
    r = ALPHA * x1_ref[...] + (1.0 + g2) * ff.reshape(bb, tt, D_MODEL)
    y_ref[...] = _layer_norm_rows(r, g_ref[...], b_ref[...])


def _moe(x1, h2e, mod, w_gu, w_down, ln_g, ln_b, *, bb, tt):
    bsz, seq, d = x1.shape
    rows = bb * tt
    srows = rows + N_GROUPS * MOE_BLK
    tri = jnp.tril(jnp.ones((rows, rows), BF16))
    kern = functools.partial(_moe_kernel, bb=bb, tt=tt)
    tile = lambda i, j: (i, j, 0)
    const2 = lambda i, j: (0, 0)
    const3 = lambda i, j: (0, 0, 0)
    resident = pl.Buffered(1)
    return pl.pallas_call(
        kern,
        grid=(bsz // bb, seq // tt),
        in_specs=[pl.BlockSpec((bb, tt, d), tile),
                  pl.BlockSpec((bb, tt, d + ROUTE_LANES), tile),
                  pl.BlockSpec((bb, 6, d), lambda i, j: (i, 0, 0)),
                  pl.BlockSpec((rows, rows), const2, pipeline_mode=resident),
                  pl.BlockSpec(w_gu.shape, const3, pipeline_mode=resident),
                  pl.BlockSpec(w_down.shape, const3, pipeline_mode=resident),
                  pl.BlockSpec((1, d), const2),
                  pl.BlockSpec((1, d), const2)],
        out_specs=pl.BlockSpec((bb, tt, d), tile),
        out_shape=jax.ShapeDtypeStruct((bsz, seq, d), F32),
        scratch_shapes=[pltpu.VMEM((srows, d), BF16),
                        pltpu.VMEM((srows, ROUTE_LANES), F32),
                        pltpu.VMEM((srows, d), BF16)],
        compiler_params=_cparams("arbitrary", "arbitrary"),
        name="moe",
    )(x1, h2e, mod, tri, w_gu, w_down, ln_g, ln_b)


def _stream(x, mod, k0, v0, u0, p, *, mixer_tm, masked, bb, tt, moe_tt):
    mixin, nk, nv, nu = _mixer(x, mod, p["w_in"], p["bias"], p["sink"], p["w_dw"], p["b_dw"],
                               p["cln_g"], p["cln_b"], k0, v0, u0, tm=mixer_tm, masked=masked)
    x1, h2e = _post(x, mixin, mod, p["w_out"], p["ln1_g"], p["ln1_b"], p["w_route_split"], p["w_route_hi"],
                    p["b_route"], bb=bb, tt=tt)
    y = _moe(x1, h2e, mod, p["w_gu"], p["w_down"], p["ln2_g"], p["ln2_b"], bb=bb, tt=moe_tt)
    bsz = x.shape[0]
    cache_shape = (1, bsz, WINDOW, N_KV_HEADS, HEAD_DIM)
    return y, nk.reshape(cache_shape), nv.reshape(cache_shape), nu[None]


def kernel(x_prompt, x_sample, cache_attn_k, cache_attn_v, state_conv, c_prompt, c_sample, rel_bias, w_ada, b_ada, w_in, attn_sinks, w_dw, b_dw, conv_ln_g, conv_ln_b, w_out, ln1_g, ln1_b, w_group, b_group, w_erouter, b_erouter, w_gate, w_up, w_down, ln2_g, ln2_b):
    bp = x_prompt.shape[0]
    bs = x_sample.shape[0]
    mod = _modulation(jnp.concatenate([c_prompt, c_sample], 0), w_ada[0], b_ada[0])
    mod = mod.reshape(bp + bs, 6, D_MODEL)

    w_route = jnp.concatenate([w_erouter[0].reshape(D_MODEL, N_EXPERTS), w_group[0]], 1)
    w_route = jnp.pad(w_route, ((0, 0), (0, ROUTE_LANES - N_EXPERTS - N_GROUPS)))
    b_route = jnp.concatenate([b_erouter[0].reshape(N_EXPERTS), b_group[0]])
    b_route = jnp.pad(b_route, (0, ROUTE_LANES - N_EXPERTS - N_GROUPS)).reshape(1, ROUTE_LANES)
    w_route_hi = w_route.astype(BF16)
    w_route_lo = (w_route - w_route_hi.astype(F32)).astype(BF16)

    def group_cols(w):
        w = w.reshape(N_GROUPS, EXPERTS_PER_GROUP, D_MODEL, D_EXPERT)
        return jnp.transpose(w, (0, 2, 1, 3)).reshape(N_GROUPS, D_MODEL, EXPERTS_PER_GROUP * D_EXPERT)

    p = {
        "w_in": w_in[0].astype(BF16),
        "bias": _relative_bias(rel_bias),
        "sink": attn_sinks[0],
        "w_dw": jnp.broadcast_to(w_dw[0][:, None, :], (CONV_K, SUBLANES, CONV_CH)),
        "b_dw": b_dw[0].reshape(1, CONV_CH),
        "cln_g": conv_ln_g[0].reshape(1, CONV_CH), "cln_b": conv_ln_b[0].reshape(1, CONV_CH),
        "w_out": w_out[0].astype(BF16),
        "ln1_g": ln1_g[0].reshape(1, D_MODEL), "ln1_b": ln1_b[0].reshape(1, D_MODEL),
        "w_route_split": jnp.concatenate([w_route_hi, w_route_lo], 1), "w_route_hi": w_route_hi,
        "b_route": b_route,
        "w_gu": jnp.concatenate([group_cols(w_gate[0]), group_cols(w_up[0])], -1).astype(BF16),
        "w_down": w_down[0].reshape(N_GROUPS, EXPERTS_PER_GROUP * D_EXPERT, D_MODEL).astype(BF16),
        "ln2_g": ln2_g[0].reshape(1, D_MODEL), "ln2_b": ln2_b[0].reshape(1, D_MODEL),
    }

    zk = jnp.zeros((bp, WINDOW, KV_WIDTH), F32)
    zu = jnp.zeros((bp, CONV_PAD, CONV_CH), F32)
    yp, pk, pv, pc = _stream(x_prompt, mod[:bp], zk, zk, zu, p, mixer_tm=512, masked=True, bb=1, tt=512,
                             moe_tt=512)

    k0 = cache_attn_k[0].reshape(bs, WINDOW, KV_WIDTH)
    v0 = cache_attn_v[0].reshape(bs, WINDOW, KV_WIDTH)
    u0 = jnp.pad(state_conv[0], ((0, 0), (CONV_PAD - CONV_HIST, 0), (0, 0)))
    ts = x_sample.shape[1]
    ys, sk, sv, sc = _stream(x_sample, mod[bp:], k0, v0, u0, p, mixer_tm=ts, masked=False, bb=8, tt=ts,
                             moe_tt=ts)
    return yp, ys, pk, pv, pc, sk, sv, sc
```

```python
import functools
import math

import jax
import jax.numpy as jnp
from jax import lax
from jax.experimental import pallas as pl
from jax.experimental.pallas import tpu as pltpu

D_MODEL = 1024
CHUNK = 64
N_HEADS = 8
N_KV_HEADS = 2
HEAD_DIM = 64
Q_GROUP = N_HEADS // N_KV_HEADS
ATTN_WIDTH = N_HEADS * HEAD_DIM
KV_WIDTH = N_KV_HEADS * HEAD_DIM
WINDOW = 128
WINDOW_CHUNKS = WINDOW // CHUNK
KEYS = WINDOW + CHUNK
CONV_CH = D_MODEL // 2
CONV_K = 31
CONV_HIST = CONV_K - 1
CONV_PAD = 32
SUBLANES = 8
CONV_ROWS = 64
CONV_LANES = 128
MIX_WIDTH = ATTN_WIDTH + CONV_CH
IN_WIDTH = ATTN_WIDTH + 2 * KV_WIDTH + 2 * CONV_CH
N_BUCKETS = 32
MAX_DISTANCE = 128
N_GROUPS = 4
EXPERTS_PER_GROUP = 4
N_EXPERTS = N_GROUPS * EXPERTS_PER_GROUP
D_EXPERT = D_MODEL // 4
DEPTH = 1
ALPHA = (2 * DEPTH) ** 0.25
LN_EPS = 1e-5
NEG_INF = -1e30
ROUTE_LANES = 128
COMB_LO_LANE = 16
GROUP_LANE = 32
MOE_BLK = 160
MOE_ALIGN = 16
VMEM_LIMIT = 56 * 1024 * 1024

BF16 = jnp.bfloat16
F32 = jnp.float32


def _cparams(*sem):
    return pltpu.CompilerParams(dimension_semantics=sem, vmem_limit_bytes=VMEM_LIMIT)


def _mod_kernel(c_ref, w_ref, b_ref, o_ref):
    c = c_ref[...]
    s = c * jax.nn.sigmoid(c)
    o_ref[...] = jnp.dot(s, w_ref[...], preferred_element_type=F32,
                         precision=lax.Precision.HIGHEST) + b_ref[...]


def _modulation(c, w_ada, b_ada):
    n, d = c.shape
    width = w_ada.shape[1]
    tn = 1536
    return pl.pallas_call(
        _mod_kernel,
        grid=(width // tn,),
        in_specs=[pl.BlockSpec((n, d), lambda j: (0, 0)),
                  pl.BlockSpec((d, tn), lambda j: (0, j)),
                  pl.BlockSpec((1, tn), lambda j: (0, j))],
        out_specs=pl.BlockSpec((n, tn), lambda j: (0, j)),
        out_shape=jax.ShapeDtypeStruct((n, width), F32),
        compiler_params=_cparams("arbitrary"),
        name="modulation",
    )(c, w_ada, b_ada.reshape(1, width))


def _t5_bucket(rel):
    nb = N_BUCKETS // 2
    max_exact = nb // 2
    ret = jnp.where(rel > 0, nb, 0)
    n = jnp.abs(rel)
    large = max_exact + (jnp.log(jnp.maximum(n, 1).astype(jnp.float32) / max_exact)
                         / math.log(MAX_DISTANCE / max_exact) * (nb - max_exact)).astype(jnp.int32)
    large = jnp.minimum(large, nb - 1)
    return ret + jnp.where(n < max_exact, n, large)


def _bias_kernel(table_ref, bucket_ref, o_ref):
    bucket = bucket_ref[...]
    col = lax.broadcasted_iota(jnp.int32, bucket.shape, 1)
    for h in range(N_HEADS):
        acc = jnp.zeros(bucket.shape, F32)
        for b in range(N_BUCKETS):
            acc = jnp.where(bucket == b, table_ref[b, h], acc)
        for v in range(WINDOW_CHUNKS + 1):
            o_ref[v, h] = jnp.where(col >= v * CHUNK, acc, NEG_INF)


def _relative_bias(table):
    rel = jnp.arange(KEYS)[None, :] - WINDOW - jnp.arange(CHUNK)[:, None]
    bucket = _t5_bucket(rel).astype(jnp.int32)
    nv = WINDOW_CHUNKS + 1
    bias = pl.pallas_call(
        _bias_kernel,
        in_specs=[pl.BlockSpec(memory_space=pltpu.SMEM),
                  pl.BlockSpec((CHUNK, KEYS), lambda: (0, 0))],
        out_specs=pl.BlockSpec((nv, N_HEADS, CHUNK, KEYS), lambda: (0, 0, 0, 0)),
        out_shape=jax.ShapeDtypeStruct((nv, N_HEADS, CHUNK, KEYS), F32),
        name="relative_bias",
    )(table, bucket)
    bias = bias.reshape(nv, N_HEADS // 2, 2, CHUNK, KEYS)
    return jnp.transpose(bias, (0, 1, 3, 2, 4)).reshape(nv, N_HEADS // 2, CHUNK, 2 * KEYS)


def _layer_norm_rows(x, g, b):
    mu = jnp.mean(x, -1, keepdims=True)
    xc = x - mu
    var = jnp.mean(xc * xc, -1, keepdims=True)
    return xc * lax.rsqrt(var + LN_EPS) * g + b


def _mixer_kernel(x_ref, mod_ref, win_ref, bias_ref, sink_ref, wdw_ref, bdw_ref, clg_ref, clb_ref,
                  k0_ref, v0_ref, u0_ref,
                  mix_ref, nk_ref, nv_ref, nu_ref,
                  kz, vz, ush, yconv, s_buf, p_buf, e_buf, *, tm, masked):
    t = pl.program_id(1)
    nt = pl.num_programs(1)
    n_chunks = tm // CHUNK
    ucat = ush.at[0]

    left = lax.broadcasted_iota(jnp.int32, (1, KV_WIDTH), 1) < HEAD_DIM

    def store_kv(row0, k, v):
        n = k.shape[0]
        for dst, val in ((kz, k), (vz, v)):
            swapped = pltpu.roll(val, HEAD_DIM, 1)
            dst[0, row0:row0 + n, 0:KV_WIDTH] = jnp.where(left, val, 0.0).astype(BF16)
            dst[1, row0:row0 + n, 0:KV_WIDTH] = jnp.where(left, 0.0, swapped).astype(BF16)
            dst[2, row0:row0 + n, 0:KV_WIDTH] = jnp.where(left, swapped, 0.0).astype(BF16)
            dst[3, row0:row0 + n, 0:KV_WIDTH] = jnp.where(left, 0.0, val).astype(BF16)

    @pl.when(t == 0)
    def _():
        ones_left = jnp.broadcast_to(jnp.where(left, 1.0, 0.0).astype(BF16), (WINDOW + tm, KV_WIDTH))
        ones_right = jnp.broadcast_to(jnp.where(left, 0.0, 1.0).astype(BF16), (WINDOW + tm, KV_WIDTH))
        for i in range(2 * N_KV_HEADS):
            vz[i, :, KV_WIDTH:] = ones_left if i % 2 == 0 else ones_right
        store_kv(0, k0_ref[0], v0_ref[0])
        ucat[0:CONV_PAD, :] = u0_ref[0]

    sh1 = mod_ref[0, 0:1, :]
    sc1 = mod_ref[0, 1:2, :]
    h = (x_ref[0] * (1.0 + sc1) + sh1).astype(BF16)
    o1 = ATTN_WIDTH
    o3 = o1 + 2 * KV_WIDTH
    q = (jnp.dot(h, win_ref[:, 0:o1], preferred_element_type=F32) * (HEAD_DIM ** -0.5)).astype(BF16)
    kv = jnp.dot(h, win_ref[:, o1:o3], preferred_element_type=F32)
    ag = jnp.dot(h, win_ref[:, o3:IN_WIDTH], preferred_element_type=F32)
    u = ag[:, 0:CONV_CH] * jax.nn.sigmoid(ag[:, CONV_CH:])
    store_kv(WINDOW, kv[:, 0:KV_WIDTH], kv[:, KV_WIDTH:])
    ucat[CONV_PAD:CONV_PAD + tm, :] = u

    @pl.when(t == nt - 1)
    def _():
        if tm >= WINDOW:
            nk_ref[0] = kv[tm - WINDOW:, 0:KV_WIDTH]
            nv_ref[0] = kv[tm - WINDOW:, KV_WIDTH:]
        else:
            nk_ref[0, 0:WINDOW - tm, :] = k0_ref[0, tm:WINDOW, :]
            nv_ref[0, 0:WINDOW - tm, :] = v0_ref[0, tm:WINDOW, :]
            nk_ref[0, WINDOW - tm:, :] = kv[:, 0:KV_WIDTH]
            nv_ref[0, WINDOW - tm:, :] = kv[:, KV_WIDTH:]
        nu_ref[0] = ucat[CONV_PAD + tm - CONV_HIST:CONV_PAD + tm, :]

    span = tm + CONV_PAD - SUBLANES
    for s in range(1, SUBLANES):
        ush[s, 0:span, :] = ucat[s:s + span, :]
    off = CONV_PAD - CONV_HIST
    rb = min(CONV_ROWS, tm)
    groups = rb // SUBLANES

    def conv_rows(lanes, r, carry):
        r0 = pl.multiple_of(r * rb, rb)
        partial = []
        for shift in range(SUBLANES):
            taps = [k for k in range(CONV_K) if (off + k) % SUBLANES == shift]
            top = max(off + k - shift for k in taps)
            slab = ush[shift, pl.ds(r0, rb + top), lanes].reshape(groups + top // SUBLANES, SUBLANES, CONV_LANES)
            acc = None
            for k in taps:
                g0 = (off + k - shift) // SUBLANES
                term = slab[g0:g0 + groups] * wdw_ref[k, :, lanes]
                acc = term if acc is None else acc + term
            partial.append(acc)
        while len(partial) > 1:
            partial = [a + b for a, b in zip(partial[0::2], partial[1::2])]
        yconv[pl.ds(r0, rb), lanes] = (partial[0] + bdw_ref[:, lanes]).reshape(rb, CONV_LANES)
        return carry

    for part in range(CONV_CH // CONV_LANES):
        lanes = slice(part * CONV_LANES, (part + 1) * CONV_LANES)
        lax.fori_loop(0, tm // rb, functools.partial(conv_rows, lanes), 0, unroll=min(2, tm // rb))
    y = _layer_norm_rows(yconv[...], clg_ref[...], clb_ref[...])
    mix_ref[0, :, ATTN_WIDTH:] = (y * jax.nn.sigmoid(y)).astype(BF16)

    n_pairs = N_HEADS // 2
    pair_w = 2 * HEAD_DIM
    first_head = lax.broadcasted_iota(jnp.int32, (CHUNK, pair_w), 1) < HEAD_DIM
    nt_dims = (((1,), (1,)), ((), ()))

    def window(ref, j, c):
        kvh = (2 * j) // Q_GROUP
        rows = slice(c * CHUNK, c * CHUNK + KEYS)
        return jnp.concatenate([ref[2 * kvh, rows, :], ref[2 * kvh + 1, rows, :]], axis=0)

    for c in range(n_chunks):
        if masked and c < WINDOW_CHUNKS:
            variant = jnp.where(t == 0, WINDOW_CHUNKS - c, 0)
        else:
            variant = 0
        for j in range(n_pairs):
            qp = q[c * CHUNK:(c + 1) * CHUNK, j * pair_w:(j + 1) * pair_w]
            s_buf[c, j] = (lax.dot_general(qp, window(kz, j, c), nt_dims, preferred_element_type=F32)
                           + bias_ref[variant, j])

    for c in range(n_chunks):
        for j in range(n_pairs):
            s0 = s_buf[c, j, :, 0:pair_w]
            s1 = s_buf[c, j, :, pair_w:2 * pair_w]
            s2 = s_buf[c, j, :, 2 * pair_w:]
            sink_a = sink_ref[2 * j]
            sink_b = sink_ref[2 * j + 1]
            m_a = jnp.maximum(jnp.max(jnp.maximum(s0, jnp.where(first_head, s1, NEG_INF)), -1, keepdims=True), sink_a)
            m_b = jnp.maximum(jnp.max(jnp.maximum(jnp.where(first_head, NEG_INF, s1), s2), -1, keepdims=True), sink_b)
            p_buf[c, j, :, 0:pair_w] = jnp.exp(s0 - m_a).astype(BF16)
            p_buf[c, j, :, pair_w:2 * pair_w] = jnp.exp(s1 - jnp.where(first_head, m_a, m_b)).astype(BF16)
            p_buf[c, j, :, 2 * pair_w:] = jnp.exp(s2 - m_b).astype(BF16)
            e_buf[c, j] = jnp.where(first_head, jnp.exp(sink_a - m_a), jnp.exp(sink_b - m_b))

    for c in range(n_chunks):
        for j in range(n_pairs):
            res = jnp.dot(p_buf[c, j], window(vz, j, c), preferred_element_type=F32)
            out = res[:, 0:pair_w] * (1.0 / (res[:, pair_w:] + e_buf[c, j]))
            mix_ref[0, c * CHUNK:(c + 1) * CHUNK, j * pair_w:(j + 1) * pair_w] = out.astype(BF16)

    @pl.when(t < nt - 1)
    def _():
        if tm >= WINDOW:
            for i in range(2 * N_KV_HEADS):
                kz[i, 0:WINDOW, :] = kz[i, tm:tm + WINDOW, :]
                vz[i, 0:WINDOW, 0:KV_WIDTH] = vz[i, tm:tm + WINDOW, 0:KV_WIDTH]
            ucat[0:CONV_PAD, :] = ucat[tm:tm + CONV_PAD, :]


def _mixer(x, mod, w_in, bias, sink, w_dw, b_dw, cln_g, cln_b, k0, v0, u0, *, tm, masked):
    bsz, seq, d = x.shape
    assert seq % tm == 0 and tm % CHUNK == 0 and (tm >= WINDOW or seq == tm)
    nt = seq // tm
    kern = functools.partial(_mixer_kernel, tm=tm, masked=masked)
    const2 = lambda b, t: (0, 0)
    const3 = lambda b, t: (0, 0, 0)
    per_b = lambda b, t: (b, 0, 0)
    return pl.pallas_call(
        kern,
        grid=(bsz, nt),
        in_specs=[pl.BlockSpec((1, tm, d), lambda b, t: (b, t, 0)),
                  pl.BlockSpec((1, 6, d), per_b),
                  pl.BlockSpec((d, IN_WIDTH), const2),
                  pl.BlockSpec((WINDOW_CHUNKS + 1, N_HEADS // 2, CHUNK, 2 * KEYS), lambda b, t: (0, 0, 0, 0)),
                  pl.BlockSpec(memory_space=pltpu.SMEM),
                  pl.BlockSpec((CONV_K, SUBLANES, CONV_CH), const3),
                  pl.BlockSpec((1, CONV_CH), const2),
                  pl.BlockSpec((1, CONV_CH), const2),
                  pl.BlockSpec((1, CONV_CH), const2),
                  pl.BlockSpec((1, WINDOW, KV_WIDTH), per_b),
                  pl.BlockSpec((1, WINDOW, KV_WIDTH), per_b),
                  pl.BlockSpec((1, CONV_PAD, CONV_CH), per_b)],
        out_specs=[pl.BlockSpec((1, tm, MIX_WIDTH), lambda b, t: (b, t, 0)),
                   pl.BlockSpec((1, WINDOW, KV_WIDTH), per_b),
                   pl.BlockSpec((1, WINDOW, KV_WIDTH), per_b),
                   pl.BlockSpec((1, CONV_HIST, CONV_CH), per_b)],
        out_shape=[jax.ShapeDtypeStruct((bsz, seq, MIX_WIDTH), BF16),
                   jax.ShapeDtypeStruct((bsz, WINDOW, KV_WIDTH), F32),
                   jax.ShapeDtypeStruct((bsz, WINDOW, KV_WIDTH), F32),
                   jax.ShapeDtypeStruct((bsz, CONV_HIST, CONV_CH), F32)],
        scratch_shapes=[pltpu.VMEM((2 * N_KV_HEADS, WINDOW + tm, KV_WIDTH), BF16),
                        pltpu.VMEM((2 * N_KV_HEADS, WINDOW + tm, 2 * KV_WIDTH), BF16),
                        pltpu.VMEM((SUBLANES, CONV_PAD + tm, CONV_CH), F32),
                        pltpu.VMEM((tm, CONV_CH), F32),
                        pltpu.VMEM((tm // CHUNK, N_HEADS // 2, CHUNK, 2 * KEYS), F32),
                        pltpu.VMEM((tm // CHUNK, N_HEADS // 2, CHUNK, 2 * KEYS), BF16),
                        pltpu.VMEM((tm // CHUNK, N_HEADS // 2, CHUNK, 2 * HEAD_DIM), F32)],
        compiler_params=_cparams("parallel", "arbitrary"),
        name="mixer",
    )(x, mod, w_in, bias, sink, w_dw, b_dw, cln_g, cln_b, k0, v0, u0)


def _post_kernel(x_ref, mix_ref, mod_ref, wout_ref, g_ref, b_ref, wr_split_ref, wr_hi_ref, br_ref,
                 x1_ref, h2e_ref, *, bb, tt):
    rows = bb * tt
    mix = jnp.dot(mix_ref[...].reshape(rows, MIX_WIDTH), wout_ref[...], preferred_element_type=F32)
    g1 = mod_ref[:, 2:3, :]
    sh2 = mod_ref[:, 3:4, :]
    sc2 = mod_ref[:, 4:5, :]
    r = ALPHA * x_ref[...] + (1.0 + g1) * mix.reshape(bb, tt, D_MODEL)
    x1 = _layer_norm_rows(r, g_ref[...], b_ref[...])
    x1_ref[...] = x1
    h2 = (x1 * (1.0 + sc2) + sh2).reshape(rows, D_MODEL)
    h2_hi = h2.astype(BF16)
    h2e_ref[:, :, 0:D_MODEL] = h2_hi.reshape(bb, tt, D_MODEL)

    h2_lo = (h2 - h2_hi.astype(F32)).astype(BF16)
    hi_terms = jnp.dot(h2_hi, wr_split_ref[...], preferred_element_type=F32)
    logits = (hi_terms[:, 0:ROUTE_LANES] + hi_terms[:, ROUTE_LANES:]
              + jnp.dot(h2_lo, wr_hi_ref[...], preferred_element_type=F32) + br_ref[...])
    lane = lax.broadcasted_iota(jnp.int32, (rows, ROUTE_LANES), 1).astype(F32)
    far = float(ROUTE_LANES)
    is_group = (lane >= N_EXPERTS) & (lane < N_EXPERTS + N_GROUPS)
    gl = jnp.where(is_group, logits, NEG_INF)
    gmax = jnp.max(gl, -1, keepdims=True)
    gidx = jnp.min(jnp.where(is_group & (gl == gmax), lane, far), -1, keepdims=True) - N_EXPERTS
    pg = 1.0 / jnp.sum(jnp.where(is_group, jnp.exp(gl - gmax), 0.0), -1, keepdims=True)
    in_group = (lane >= gidx * EXPERTS_PER_GROUP) & (lane < (gidx + 1) * EXPERTS_PER_GROUP)
    el = jnp.where(in_group, logits, NEG_INF)
    v1 = jnp.max(el, -1, keepdims=True)
    i1 = jnp.min(jnp.where(in_group & (el == v1), lane, far), -1, keepdims=True)
    rest = in_group & (lane != i1)
    el2 = jnp.where(rest, logits, NEG_INF)
    v2 = jnp.max(el2, -1, keepdims=True)
    i2 = jnp.min(jnp.where(rest & (el2 == v2), lane, far), -1, keepdims=True)
    e2 = jnp.exp(v2 - v1)
    w1 = pg / (1.0 + e2)
    w2 = pg * e2 / (1.0 + e2)
    comb = jnp.where(lane == i1, w1, jnp.where(lane == i2, w2, 0.0))

    c_hi = comb.astype(BF16).astype(F32)
    c_lo = (comb - c_hi).astype(BF16).astype(F32)
    rec = c_hi + pltpu.roll(c_lo, COMB_LO_LANE, 1) + jnp.where(lane == gidx + GROUP_LANE, 1.0, 0.0)
    h2e_ref[:, :, D_MODEL:] = rec.astype(BF16).reshape(bb, tt, ROUTE_LANES)


def _post(x, mixin, mod, w_out, ln_g, ln_b, w_route_split, w_route_hi, b_route, *, bb, tt):
    bsz, seq, d = x.shape
    assert bsz % bb == 0 and seq % tt == 0
    kern = functools.partial(_post_kernel, bb=bb, tt=tt)
    tile = lambda i, j: (i, j, 0)
    const2 = lambda i, j: (0, 0)
    return pl.pallas_call(
        kern,
        grid=(bsz // bb, seq // tt),
        in_specs=[pl.BlockSpec((bb, tt, d), tile),
                  pl.BlockSpec((bb, tt, MIX_WIDTH), tile),
                  pl.BlockSpec((bb, 6, d), lambda i, j: (i, 0, 0)),
                  pl.BlockSpec((MIX_WIDTH, d), const2),
                  pl.BlockSpec((1, d), const2),
                  pl.BlockSpec((1, d), const2),
                  pl.BlockSpec((d, 2 * ROUTE_LANES), const2),
                  pl.BlockSpec((d, ROUTE_LANES), const2),
                  pl.BlockSpec((1, ROUTE_LANES), const2)],
        out_specs=[pl.BlockSpec((bb, tt, d), tile),
                   pl.BlockSpec((bb, tt, d + ROUTE_LANES), tile)],
        out_shape=[jax.ShapeDtypeStruct((bsz, seq, d), F32),
                   jax.ShapeDtypeStruct((bsz, seq, d + ROUTE_LANES), BF16)],
        compiler_params=_cparams("parallel", "parallel"),
        name="post",
    )(x, mixin, mod, w_out, ln_g, ln_b, w_route_split, w_route_hi, b_route)


def _moe_rows(rows):
    return rows + N_GROUPS * MOE_ALIGN + MOE_BLK


def _moe_kernel(x1_ref, h2e_ref, mod_ref, tri_ref, wg_ref, wu_ref, wd_ref, g_ref, b_ref, y_ref,
                xs_ref, cw_ref, ys_ref, *, bb, tt):
    rows = bb * tt
    srows = _moe_rows(rows)

    @pl.when((pl.program_id(0) == 0) & (pl.program_id(1) == 0))
    def _():
        ys_ref[...] = jnp.zeros_like(ys_ref)

    ext = h2e_ref[...].reshape(rows, D_MODEL + ROUTE_LANES)
    rec = ext[:, D_MODEL:]
    cum = jnp.dot(tri_ref[...], rec, preferred_element_type=F32)
    lane1 = lax.broadcasted_iota(jnp.int32, (1, ROUTE_LANES), 1)
    cnt = cum[rows - 1:rows, :]
    off = jnp.int32(0)
    offs, counts = [], []
    off_vec = jnp.zeros((1, ROUTE_LANES), F32)
    for g in range(N_GROUPS):
        n_g = jnp.sum(jnp.where(lane1 == GROUP_LANE + g, cnt, 0.0)).astype(jnp.int32)
        offs.append(off)
        counts.append(n_g)
        off_vec = jnp.where(lane1 == GROUP_LANE + g, off.astype(F32), off_vec)
        off = off + ((n_g + (MOE_ALIGN - 1)) // MOE_ALIGN) * MOE_ALIGN

    lane = lax.broadcasted_iota(jnp.int32, (rows, ROUTE_LANES), 1)
    mine = (lane >= GROUP_LANE) & (lane < GROUP_LANE + N_GROUPS) & (rec.astype(F32) > 0.0)
    pos = jnp.sum(jnp.where(mine, cum - 1.0 + off_vec, 0.0), -1, keepdims=True)
    scatter = jnp.where(pos == lax.broadcasted_iota(jnp.int32, (rows, srows), 1).astype(F32), 1.0, 0.0).astype(BF16)
    pos_row = jnp.transpose(jnp.broadcast_to(pos, (rows, ROUTE_LANES)))[0:1, :]
    gather = jnp.where(lax.broadcasted_iota(jnp.int32, (srows, rows), 0).astype(F32) == pos_row, 1.0, 0.0).astype(BF16)
    xs_ref[...] = jnp.dot(gather, ext[:, 0:D_MODEL], preferred_element_type=F32).astype(BF16)
    rs = jnp.dot(gather, rec, preferred_element_type=F32)
    cw_ref[...] = rs + pltpu.roll(rs, ROUTE_LANES - COMB_LO_LANE, 1)

    def block(g, i, carry):
        r0 = pl.multiple_of(offs[g] + i * MOE_BLK, MOE_ALIGN)
        xb = xs_ref[pl.ds(r0, MOE_BLK), :]
        cwb = cw_ref[pl.ds(r0, MOE_BLK), :]
        parts = []
        for j in range(EXPERTS_PER_GROUP):
            e = g * EXPERTS_PER_GROUP + j
            hg = jnp.dot(xb, wg_ref[e], preferred_element_type=F32)
            hu = jnp.dot(xb, wu_ref[e], preferred_element_type=F32)
            parts.append((hg * jax.nn.sigmoid(hg) * hu * cwb[:, e:e + 1]).astype(BF16))
        act = jnp.concatenate(parts, axis=1)
        ys_ref[pl.ds(r0, MOE_BLK), :] = jnp.dot(act, wd_ref[g], preferred_element_type=F32).astype(BF16)
        return carry

    for g in range(N_GROUPS):
        lax.fori_loop(0, (counts[g] + (MOE_BLK - 1)) // MOE_BLK, functools.partial(block, g), 0)

    ff = jnp.dot(scatter, ys_ref[...], preferred_element_type=F32)
    g2 = mod_ref[:, 5:6, :]
    r = ALPHA * x1_ref[...] + (1.0 + g2) * ff.reshape(bb, tt, D_MODEL)
    y_ref[...] = _layer_norm_rows(r, g_ref[...], b_ref[...])


def _moe(x1, h2e, mod, w_gate, w_up, w_down, ln_g, ln_b, *, bb, tt):
    bsz, seq, d = x1.shape
    rows = bb * tt
    srows = _moe_rows(rows)
    tri = jnp.tril(jnp.ones((rows, rows), BF16))
    kern = functools.partial(_moe_kernel, bb=bb, tt=tt)
    tile = lambda i, j: (i, j, 0)
    const2 = lambda i, j: (0, 0)
    const3 = lambda i, j: (0, 0, 0)
    resident = pl.Buffered(1)
    return pl.pallas_call(
        kern,
        grid=(bsz // bb, seq // tt),
        in_specs=[pl.BlockSpec((bb, tt, d), tile),
                  pl.BlockSpec((bb, tt, d + ROUTE_LANES), tile),
                  pl.BlockSpec((bb, 6, d), lambda i, j: (i, 0, 0)),
                  pl.BlockSpec((rows, rows), const2, pipeline_mode=resident),
                  pl.BlockSpec(w_gate.shape, const3, pipeline_mode=resident),
                  pl.BlockSpec(w_up.shape, const3, pipeline_mode=resident),
                  pl.BlockSpec(w_down.shape, const3, pipeline_mode=resident),
                  pl.BlockSpec((1, d), const2),
                  pl.BlockSpec((1, d), const2)],
        out_specs=pl.BlockSpec((bb, tt, d), tile),
        out_shape=jax.ShapeDtypeStruct((bsz, seq, d), F32),
        scratch_shapes=[pltpu.VMEM((srows, d), BF16),
                        pltpu.VMEM((srows, ROUTE_LANES), F32),
                        pltpu.VMEM((srows, d), BF16)],
        compiler_params=_cparams("arbitrary", "arbitrary"),
        name="moe",
    )(x1, h2e, mod, tri, w_gate, w_up, w_down, ln_g, ln_b)


def _stream(x, mod, k0, v0, u0, p, *, mixer_tm, masked, bb, tt, moe_tt):
    mixin, nk, nv, nu = _mixer(x, mod, p["w_in"], p["bias"], p["sink"], p["w_dw"], p["b_dw"],
                               p["cln_g"], p["cln_b"], k0, v0, u0, tm=mixer_tm, masked=masked)
    x1, h2e = _post(x, mixin, mod, p["w_out"], p["ln1_g"], p["ln1_b"], p["w_route_split"], p["w_route_hi"],
                    p["b_route"], bb=bb, tt=tt)
    y = _moe(x1, h2e, mod, p["w_gate"], p["w_up"], p["w_down"], p["ln2_g"], p["ln2_b"], bb=bb, tt=moe_tt)
    bsz = x.shape[0]
    cache_shape = (1, bsz, WINDOW, N_KV_HEADS, HEAD_DIM)
    return y, nk.reshape(cache_shape), nv.reshape(cache_shape), nu[None]


def kernel(x_prompt, x_sample, cache_attn_k, cache_attn_v, state_conv, c_prompt, c_sample, rel_bias, w_ada, b_ada, w_in, attn_sinks, w_dw, b_dw, conv_ln_g, conv_ln_b, w_out, ln1_g, ln1_b, w_group, b_group, w_erouter, b_erouter, w_gate, w_up, w_down, ln2_g, ln2_b):
    bp = x_prompt.shape[0]
    bs = x_sample.shape[0]
    mod = _modulation(jnp.concatenate([c_prompt, c_sample], 0), w_ada[0], b_ada[0])
    mod = mod.reshape(bp + bs, 6, D_MODEL)

    w_route = jnp.concatenate([w_erouter[0].reshape(D_MODEL, N_EXPERTS), w_group[0]], 1)
    w_route = jnp.pad(w_route, ((0, 0), (0, ROUTE_LANES - N_EXPERTS - N_GROUPS)))
    b_route = jnp.concatenate([b_erouter[0].reshape(N_EXPERTS), b_group[0]])
    b_route = jnp.pad(b_route, (0, ROUTE_LANES - N_EXPERTS - N_GROUPS)).reshape(1, ROUTE_LANES)
    w_route_hi = w_route.astype(BF16)
    w_route_lo = (w_route - w_route_hi.astype(F32)).astype(BF16)

    p = {
        "w_in": w_in[0].astype(BF16),
        "bias": _relative_bias(rel_bias),
        "sink": attn_sinks[0],
        "w_dw": jnp.broadcast_to(w_dw[0][:, None, :], (CONV_K, SUBLANES, CONV_CH)),
        "b_dw": b_dw[0].reshape(1, CONV_CH),
        "cln_g": conv_ln_g[0].reshape(1, CONV_CH), "cln_b": conv_ln_b[0].reshape(1, CONV_CH),
        "w_out": w_out[0].astype(BF16),
        "ln1_g": ln1_g[0].reshape(1, D_MODEL), "ln1_b": ln1_b[0].reshape(1, D_MODEL),
        "w_route_split": jnp.concatenate([w_route_hi, w_route_lo], 1), "w_route_hi": w_route_hi,
        "b_route": b_route,
        "w_gate": w_gate[0].astype(BF16), "w_up": w_up[0].astype(BF16),
        "w_down": w_down[0].astype(BF16).reshape(N_GROUPS, EXPERTS_PER_GROUP * D_EXPERT, D_MODEL),
        "ln2_g": ln2_g[0].reshape(1, D_MODEL), "ln2_b": ln2_b[0].reshape(1, D_MODEL),
    }

    zk = jnp.zeros((bp, WINDOW, KV_WIDTH), F32)
    zu = jnp.zeros((bp, CONV_PAD, CONV_CH), F32)
    yp, pk, pv, pc = _stream(x_prompt, mod[:bp], zk, zk, zu, p, mixer_tm=512, masked=True, bb=1, tt=512,
                             moe_tt=512)

    k0 = cache_attn_k[0].reshape(bs, WINDOW, KV_WIDTH)
    v0 = cache_attn_v[0].reshape(bs, WINDOW, KV_WIDTH)
    u0 = jnp.pad(state_conv[0], ((0, 0), (CONV_PAD - CONV_HIST, 0), (0, 0)))
    ts = x_sample.shape[1]
    ys, sk, sv, sc = _stream(x_sample, mod[bp:], k0, v0, u0, p, mixer_tm=ts, masked=False, bb=8, tt=ts,
                             moe_tt=ts)
    return yp, ys, pk, pv, pc, sk, sv, sc
```

```python
import functools
import math

import jax
import jax.numpy as jnp
from jax import lax
from jax.experimental import pallas as pl
from jax.experimental.pallas import tpu as pltpu

D_MODEL = 1024
CHUNK = 64
N_HEADS = 8
N_KV_HEADS = 2
HEAD_DIM = 64
Q_GROUP = N_HEADS // N_KV_HEADS
ATTN_WIDTH = N_HEADS * HEAD_DIM
KV_WIDTH = N_KV_HEADS * HEAD_DIM
WINDOW = 128
WINDOW_CHUNKS = WINDOW // CHUNK
KEYS = WINDOW + CHUNK
CONV_CH = D_MODEL // 2
CONV_K = 31
CONV_HIST = CONV_K - 1
CONV_PAD = 32
SUBLANES = 8
CONV_ROWS = 256
CONV_LANES = 128
MIX_WIDTH = ATTN_WIDTH + CONV_CH
IN_WIDTH = ATTN_WIDTH + 2 * KV_WIDTH + 2 * CONV_CH
N_BUCKETS = 32
MAX_DISTANCE = 128
N_GROUPS = 4
EXPERTS_PER_GROUP = 4
N_EXPERTS = N_GROUPS * EXPERTS_PER_GROUP
D_EXPERT = D_MODEL // 4
DEPTH = 1
ALPHA = (2 * DEPTH) ** 0.25
LN_EPS = 1e-5
NEG_INF = -1e30
ROUTE_LANES = 128
COMB_LO_LANE = 16
GROUP_LANE = 32
MOE_BLK = 144
MOE_ALIGN = 16
VMEM_LIMIT = 56 * 1024 * 1024

BF16 = jnp.bfloat16
F32 = jnp.float32


def _cparams(*sem):
    return pltpu.CompilerParams(dimension_semantics=sem, vmem_limit_bytes=VMEM_LIMIT)


def _mod_kernel(c_ref, w_ref, b_ref, o_ref):
    c = c_ref[...]
    s = c * jax.nn.sigmoid(c)
    o_ref[...] = jnp.dot(s, w_ref[...], preferred_element_type=F32,
                         precision=lax.Precision.HIGHEST) + b_ref[...]


def _modulation(c, w_ada, b_ada):
    n, d = c.shape
    width = w_ada.shape[1]
    tn = 1536
    return pl.pallas_call(
        _mod_kernel,
        grid=(width // tn,),
        in_specs=[pl.BlockSpec((n, d), lambda j: (0, 0)),
                  pl.BlockSpec((d, tn), lambda j: (0, j)),
                  pl.BlockSpec((1, tn), lambda j: (0, j))],
        out_specs=pl.BlockSpec((n, tn), lambda j: (0, j)),
        out_shape=jax.ShapeDtypeStruct((n, width), F32),
        compiler_params=_cparams("arbitrary"),
        name="modulation",
    )(c, w_ada, b_ada.reshape(1, width))


def _t5_bucket(rel):
    nb = N_BUCKETS // 2
    max_exact = nb // 2
    ret = jnp.where(rel > 0, nb, 0)
    n = jnp.abs(rel)
    large = max_exact + (jnp.log(jnp.maximum(n, 1).astype(jnp.float32) / max_exact)
                         / math.log(MAX_DISTANCE / max_exact) * (nb - max_exact)).astype(jnp.int32)
    large = jnp.minimum(large, nb - 1)
    return ret + jnp.where(n < max_exact, n, large)


def _bias_kernel(table_ref, bucket_ref, o_ref):
    bucket = bucket_ref[...]
    col = lax.broadcasted_iota(jnp.int32, bucket.shape, 1)
    for h in range(N_HEADS):
        acc = jnp.zeros(bucket.shape, F32)
        for b in range(N_BUCKETS):
            acc = jnp.where(bucket == b, table_ref[b, h], acc)
        for v in range(WINDOW_CHUNKS + 1):
            o_ref[v, h] = jnp.where(col >= v * CHUNK, acc, NEG_INF)


def _relative_bias(table):
    rel = jnp.arange(KEYS)[None, :] - WINDOW - jnp.arange(CHUNK)[:, None]
    bucket = _t5_bucket(rel).astype(jnp.int32)
    nv = WINDOW_CHUNKS + 1
    bias = pl.pallas_call(
        _bias_kernel,
        in_specs=[pl.BlockSpec(memory_space=pltpu.SMEM),
                  pl.BlockSpec((CHUNK, KEYS), lambda: (0, 0))],
        out_specs=pl.BlockSpec((nv, N_HEADS, CHUNK, KEYS), lambda: (0, 0, 0, 0)),
        out_shape=jax.ShapeDtypeStruct((nv, N_HEADS, CHUNK, KEYS), F32),
        name="relative_bias",
    )(table, bucket)
    bias = bias.reshape(nv, N_HEADS // 2, 2, CHUNK, KEYS)
    return jnp.transpose(bias, (0, 1, 3, 2, 4)).reshape(nv, N_HEADS // 2, CHUNK, 2 * KEYS)


def _layer_norm_rows(x, g, b):
    mu = jnp.mean(x, -1, keepdims=True)
    xc = x - mu
    var = jnp.mean(xc * xc, -1, keepdims=True)
    return xc * lax.rsqrt(var + LN_EPS) * g + b


def _mixer_kernel(x_ref, mod_ref, win_ref, bias_ref, sink_ref, wdw_ref, bdw_ref, clg_ref, clb_ref,
                  k0_ref, v0_ref, u0_ref,
                  mix_ref, nk_ref, nv_ref, nu_ref,
                  kz, vz, ush, yconv, s_buf, p_buf, e_buf, *, tm, masked):
    t = pl.program_id(1)
    nt = pl.num_programs(1)
    n_chunks = tm // CHUNK
    ucat = ush.at[0]

    left = lax.broadcasted_iota(jnp.int32, (1, KV_WIDTH), 1) < HEAD_DIM

    def store_kv(row0, k, v):
        n = k.shape[0]
        for dst, val in ((kz, k), (vz, v)):
            swapped = pltpu.roll(val, HEAD_DIM, 1)
            dst[0, row0:row0 + n, 0:KV_WIDTH] = jnp.where(left, val, 0.0).astype(BF16)
            dst[1, row0:row0 + n, 0:KV_WIDTH] = jnp.where(left, 0.0, swapped).astype(BF16)
            dst[2, row0:row0 + n, 0:KV_WIDTH] = jnp.where(left, swapped, 0.0).astype(BF16)
            dst[3, row0:row0 + n, 0:KV_WIDTH] = jnp.where(left, 0.0, val).astype(BF16)

    @pl.when(t == 0)
    def _():
        ones_left = jnp.broadcast_to(jnp.where(left, 1.0, 0.0).astype(BF16), (WINDOW + tm, KV_WIDTH))
        ones_right = jnp.broadcast_to(jnp.where(left, 0.0, 1.0).astype(BF16), (WINDOW + tm, KV_WIDTH))
        for i in range(2 * N_KV_HEADS):
            vz[i, :, KV_WIDTH:] = ones_left if i % 2 == 0 else ones_right
        store_kv(0, k0_ref[0], v0_ref[0])
        ucat[0:CONV_PAD, :] = u0_ref[0]

    sh1 = mod_ref[0, 0:1, :]
    sc1 = mod_ref[0, 1:2, :]
    h = (x_ref[0] * (1.0 + sc1) + sh1).astype(BF16)
    o1 = ATTN_WIDTH
    o3 = o1 + 2 * KV_WIDTH
    q = (jnp.dot(h, win_ref[:, 0:o1], preferred_element_type=F32) * (HEAD_DIM ** -0.5)).astype(BF16)
    kv = jnp.dot(h, win_ref[:, o1:o3], preferred_element_type=F32)
    ag = jnp.dot(h, win_ref[:, o3:IN_WIDTH], preferred_element_type=F32)
    u = ag[:, 0:CONV_CH] * jax.nn.sigmoid(ag[:, CONV_CH:])
    store_kv(WINDOW, kv[:, 0:KV_WIDTH], kv[:, KV_WIDTH:])
    ucat[CONV_PAD:CONV_PAD + tm, :] = u

    if tm >= WINDOW:
        nk_ref[0] = kv[tm - WINDOW:, 0:KV_WIDTH]
        nv_ref[0] = kv[tm - WINDOW:, KV_WIDTH:]
    else:
        nk_ref[0, 0:WINDOW - tm, :] = k0_ref[0, tm:WINDOW, :]
        nv_ref[0, 0:WINDOW - tm, :] = v0_ref[0, tm:WINDOW, :]
        nk_ref[0, WINDOW - tm:, :] = kv[:, 0:KV_WIDTH]
        nv_ref[0, WINDOW - tm:, :] = kv[:, KV_WIDTH:]
    nu_ref[0] = ucat[CONV_PAD + tm - CONV_HIST:CONV_PAD + tm, :]

    n_pairs = N_HEADS // 2
    pair_w = 2 * HEAD_DIM
    first_head = lax.broadcasted_iota(jnp.int32, (CHUNK, pair_w), 1) < HEAD_DIM
    nt_dims = (((1,), (1,)), ((), ()))

    def window(ref, j, c):
        kvh = (2 * j) // Q_GROUP
        rows = slice(c * CHUNK, c * CHUNK + KEYS)
        return jnp.concatenate([ref[2 * kvh, rows, :], ref[2 * kvh + 1, rows, :]], axis=0)

    for c in range(n_chunks):
        if masked and c < WINDOW_CHUNKS:
            variant = jnp.where(t == 0, WINDOW_CHUNKS - c, 0)
        else:
            variant = 0
        for j in range(n_pairs):
            qp = q[c * CHUNK:(c + 1) * CHUNK, j * pair_w:(j + 1) * pair_w]
            s_buf[c, j] = (lax.dot_general(qp, window(kz, j, c), nt_dims, preferred_element_type=F32)
                           + bias_ref[variant, j])

    for c in range(n_chunks):
        for j in range(n_pairs):
            s0 = s_buf[c, j, :, 0:pair_w]
            s1 = s_buf[c, j, :, pair_w:2 * pair_w]
            s2 = s_buf[c, j, :, 2 * pair_w:]
            sink_a = sink_ref[2 * j]
            sink_b = sink_ref[2 * j + 1]
            m_a = jnp.maximum(jnp.max(jnp.maximum(s0, jnp.where(first_head, s1, NEG_INF)), -1, keepdims=True), sink_a)
            m_b = jnp.maximum(jnp.max(jnp.maximum(jnp.where(first_head, NEG_INF, s1), s2), -1, keepdims=True), sink_b)
            p_buf[c, j, :, 0:pair_w] = jnp.exp(s0 - m_a).astype(BF16)
            p_buf[c, j, :, pair_w:2 * pair_w] = jnp.exp(s1 - jnp.where(first_head, m_a, m_b)).astype(BF16)
            p_buf[c, j, :, 2 * pair_w:] = jnp.exp(s2 - m_b).astype(BF16)
            e_buf[c, j] = jnp.where(first_head, jnp.exp(sink_a - m_a), jnp.exp(sink_b - m_b))

    span = tm + CONV_PAD - SUBLANES
    for s in range(1, SUBLANES):
        ush[s, 0:span, :] = ucat[s:s + span, :]
    off = CONV_PAD - CONV_HIST
    rb = min(CONV_ROWS, tm)
    groups = rb // SUBLANES

    def conv_rows(lanes, r, carry):
        r0 = pl.multiple_of(r * rb, rb)
        partial = []
        for shift in range(SUBLANES):
            taps = [k for k in range(CONV_K) if (off + k) % SUBLANES == shift]
            top = max(off + k - shift for k in taps)
            slab = ush[shift, pl.ds(r0, rb + top), lanes].reshape(groups + top // SUBLANES, SUBLANES, CONV_LANES)
            acc = None
            for k in taps:
                g0 = (off + k - shift) // SUBLANES
                term = slab[g0:g0 + groups] * wdw_ref[k, :, lanes]
                acc = term if acc is None else acc + term
            partial.append(acc)
        while len(partial) > 1:
            partial = [a + b for a, b in zip(partial[0::2], partial[1::2])]
        yconv[pl.ds(r0, rb), lanes] = (partial[0] + bdw_ref[:, lanes]).reshape(rb, CONV_LANES)
        return carry

    for part in range(CONV_CH // CONV_LANES):
        lanes = slice(part * CONV_LANES, (part + 1) * CONV_LANES)
        lax.fori_loop(0, tm // rb, functools.partial(conv_rows, lanes), 0)
    y = _layer_norm_rows(yconv[...], clg_ref[...], clb_ref[...])
    mix_ref[0, :, ATTN_WIDTH:] = (y * jax.nn.sigmoid(y)).astype(BF16)

    for c in range(n_chunks):
        for j in range(n_pairs):
            res = jnp.dot(p_buf[c, j], window(vz, j, c), preferred_element_type=F32)
            out = res[:, 0:pair_w] * (1.0 / (res[:, pair_w:] + e_buf[c, j]))
            mix_ref[0, c * CHUNK:(c + 1) * CHUNK, j * pair_w:(j + 1) * pair_w] = out.astype(BF16)

    @pl.when(t < nt - 1)
    def _():
        if tm >= WINDOW:
            for i in range(2 * N_KV_HEADS):
                kz[i, 0:WINDOW, :] = kz[i, tm:tm + WINDOW, :]
                vz[i, 0:WINDOW, 0:KV_WIDTH] = vz[i, tm:tm + WINDOW, 0:KV_WIDTH]
            ucat[0:CONV_PAD, :] = ucat[tm:tm + CONV_PAD, :]


def _mixer(x, mod, w_in, bias, sink, w_dw, b_dw, cln_g, cln_b, k0, v0, u0, *, tm, masked):
    bsz, seq, d = x.shape
    assert seq % tm == 0 and tm % CHUNK == 0 and (tm >= WINDOW or seq == tm)
    nt = seq // tm
    kern = functools.partial(_mixer_kernel, tm=tm, masked=masked)
    const2 = lambda b, t: (0, 0)
    const3 = lambda b, t: (0, 0, 0)
    per_b = lambda b, t: (b, 0, 0)
    return pl.pallas_call(
        kern,
        grid=(bsz, nt),
        in_specs=[pl.BlockSpec((1, tm, d), lambda b, t: (b, t, 0)),
                  pl.BlockSpec((1, 6, d), per_b),
                  pl.BlockSpec((d, IN_WIDTH), const2),
                  pl.BlockSpec((WINDOW_CHUNKS + 1, N_HEADS // 2, CHUNK, 2 * KEYS), lambda b, t: (0, 0, 0, 0)),
                  pl.BlockSpec(memory_space=pltpu.SMEM),
                  pl.BlockSpec((CONV_K, SUBLANES, CONV_CH), const3),
                  pl.BlockSpec((1, CONV_CH), const2),
                  pl.BlockSpec((1, CONV_CH), const2),
                  pl.BlockSpec((1, CONV_CH), const2),
                  pl.BlockSpec((1, WINDOW, KV_WIDTH), per_b),
                  pl.BlockSpec((1, WINDOW, KV_WIDTH), per_b),
                  pl.BlockSpec((1, CONV_PAD, CONV_CH), per_b)],
        out_specs=[pl.BlockSpec((1, tm, MIX_WIDTH), lambda b, t: (b, t, 0)),
                   pl.BlockSpec((1, WINDOW, KV_WIDTH), per_b),
                   pl.BlockSpec((1, WINDOW, KV_WIDTH), per_b),
                   pl.BlockSpec((1, CONV_HIST, CONV_CH), per_b)],
        out_shape=[jax.ShapeDtypeStruct((bsz, seq, MIX_WIDTH), BF16),
                   jax.ShapeDtypeStruct((bsz, WINDOW, KV_WIDTH), F32),
                   jax.ShapeDtypeStruct((bsz, WINDOW, KV_WIDTH), F32),
                   jax.ShapeDtypeStruct((bsz, CONV_HIST, CONV_CH), F32)],
        scratch_shapes=[pltpu.VMEM((2 * N_KV_HEADS, WINDOW + tm, KV_WIDTH), BF16),
                        pltpu.VMEM((2 * N_KV_HEADS, WINDOW + tm, 2 * KV_WIDTH), BF16),
                        pltpu.VMEM((SUBLANES, CONV_PAD + tm, CONV_CH), F32),
                        pltpu.VMEM((tm, CONV_CH), F32),
                        pltpu.VMEM((tm // CHUNK, N_HEADS // 2, CHUNK, 2 * KEYS), F32),
                        pltpu.VMEM((tm // CHUNK, N_HEADS // 2, CHUNK, 2 * KEYS), BF16),
                        pltpu.VMEM((tm // CHUNK, N_HEADS // 2, CHUNK, 2 * HEAD_DIM), F32)],
        compiler_params=_cparams("parallel", "arbitrary"),
        name="mixer",
    )(x, mod, w_in, bias, sink, w_dw, b_dw, cln_g, cln_b, k0, v0, u0)


def _post_kernel(x_ref, mix_ref, mod_ref, wout_ref, g_ref, b_ref, wr_split_ref, wr_hi_ref, br_ref,
                 x1_ref, h2e_ref, *, bb, tt):
    rows = bb * tt
    mix = jnp.dot(mix_ref[...].reshape(rows, MIX_WIDTH), wout_ref[...], preferred_element_type=F32)
    g1 = mod_ref[:, 2:3, :]
    sh2 = mod_ref[:, 3:4, :]
    sc2 = mod_ref[:, 4:5, :]
    r = ALPHA * x_ref[...] + (1.0 + g1) * mix.reshape(bb, tt, D_MODEL)
    x1 = _layer_norm_rows(r, g_ref[...], b_ref[...])
    x1_ref[...] = x1
    h2 = (x1 * (1.0 + sc2) + sh2).reshape(rows, D_MODEL)
    h2_hi = h2.astype(BF16)
    h2e_ref[:, :, 0:D_MODEL] = h2_hi.reshape(bb, tt, D_MODEL)

    h2_lo = (h2 - h2_hi.astype(F32)).astype(BF16)
    hi_terms = jnp.dot(h2_hi, wr_split_ref[...], preferred_element_type=F32)
    logits = (hi_terms[:, 0:ROUTE_LANES] + hi_terms[:, ROUTE_LANES:]
              + jnp.dot(h2_lo, wr_hi_ref[...], preferred_element_type=F32) + br_ref[...])
    lane = lax.broadcasted_iota(jnp.int32, (rows, ROUTE_LANES), 1).astype(F32)
    far = float(ROUTE_LANES)
    is_group = (lane >= N_EXPERTS) & (lane < N_EXPERTS + N_GROUPS)
    gl = jnp.where(is_group, logits, NEG_INF)
    gmax = jnp.max(gl, -1, keepdims=True)
    gidx = jnp.min(jnp.where(is_group & (gl == gmax), lane, far), -1, keepdims=True) - N_EXPERTS
    pg = 1.0 / jnp.sum(jnp.where(is_group, jnp.exp(gl - gmax), 0.0), -1, keepdims=True)
    in_group = (lane >= gidx * EXPERTS_PER_GROUP) & (lane < (gidx + 1) * EXPERTS_PER_GROUP)
    el = jnp.where(in_group, logits, NEG_INF)
    v1 = jnp.max(el, -1, keepdims=True)
    i1 = jnp.min(jnp.where(in_group & (el == v1), lane, far), -1, keepdims=True)
    rest = in_group & (lane != i1)
    el2 = jnp.where(rest, logits, NEG_INF)
    v2 = jnp.max(el2, -1, keepdims=True)
    i2 = jnp.min(jnp.where(rest & (el2 == v2), lane, far), -1, keepdims=True)
    e2 = jnp.exp(v2 - v1)
    w1 = pg / (1.0 + e2)
    w2 = pg * e2 / (1.0 + e2)
    comb = jnp.where(lane == i1, w1, jnp.where(lane == i2, w2, 0.0))

    c_hi = comb.astype(BF16).astype(F32)
    c_lo = (comb - c_hi).astype(BF16).astype(F32)
    rec = c_hi + pltpu.roll(c_lo, COMB_LO_LANE, 1) + jnp.where(lane == gidx + GROUP_LANE, 1.0, 0.0)
    h2e_ref[:, :, D_MODEL:] = rec.astype(BF16).reshape(bb, tt, ROUTE_LANES)


def _post(x, mixin, mod, w_out, ln_g, ln_b, w_route_split, w_route_hi, b_route, *, bb, tt):
    bsz, seq, d = x.shape
    assert bsz % bb == 0 and seq % tt == 0
    kern = functools.partial(_post_kernel, bb=bb, tt=tt)
    tile = lambda i, j: (i, j, 0)
    const2 = lambda i, j: (0, 0)
    return pl.pallas_call(
        kern,
        grid=(bsz // bb, seq // tt),
        in_specs=[pl.BlockSpec((bb, tt, d), tile),
                  pl.BlockSpec((bb, tt, MIX_WIDTH), tile),
                  pl.BlockSpec((bb, 6, d), lambda i, j: (i, 0, 0)),
                  pl.BlockSpec((MIX_WIDTH, d), const2),
                  pl.BlockSpec((1, d), const2),
                  pl.BlockSpec((1, d), const2),
                  pl.BlockSpec((d, 2 * ROUTE_LANES), const2),
                  pl.BlockSpec((d, ROUTE_LANES), const2),
                  pl.BlockSpec((1, ROUTE_LANES), const2)],
        out_specs=[pl.BlockSpec((bb, tt, d), tile),
                   pl.BlockSpec((bb, tt, d + ROUTE_LANES), tile)],
        out_shape=[jax.ShapeDtypeStruct((bsz, seq, d), F32),
                   jax.ShapeDtypeStruct((bsz, seq, d + ROUTE_LANES), BF16)],
        compiler_params=_cparams("parallel", "parallel"),
        name="post",
    )(x, mixin, mod, w_out, ln_g, ln_b, w_route_split, w_route_hi, b_route)


def _moe_rows(rows):
    return rows + N_GROUPS * MOE_ALIGN + MOE_BLK


def _moe_kernel(x1_ref, h2e_ref, mod_ref, tri_ref, wg_ref, wu_ref, wd_ref, g_ref, b_ref, y_ref,
                xs_ref, cw_ref, ys_ref, *, bb, tt):
    rows = bb * tt
    srows = _moe_rows(rows)

    @pl.when((pl.program_id(0) == 0) & (pl.program_id(1) == 0))
    def _():
        ys_ref[...] = jnp.zeros_like(ys_ref)

    ext = h2e_ref[...].reshape(rows, D_MODEL + ROUTE_LANES)
    rec = ext[:, D_MODEL:]
    cum = jnp.dot(tri_ref[...], rec, preferred_element_type=F32)
    lane1 = lax.broadcasted_iota(jnp.int32, (1, ROUTE_LANES), 1)
    cnt = cum[rows - 1:rows, :]
    off = jnp.int32(0)
    offs, counts = [], []
    off_vec = jnp.zeros((1, ROUTE_LANES), F32)
    for g in range(N_GROUPS):
        n_g = jnp.sum(jnp.where(lane1 == GROUP_LANE + g, cnt, 0.0)).astype(jnp.int32)
        offs.append(off)
        counts.append(n_g)
        off_vec = jnp.where(lane1 == GROUP_LANE + g, off.astype(F32), off_vec)
        off = off + ((n_g + (MOE_ALIGN - 1)) // MOE_ALIGN) * MOE_ALIGN

    lane = lax.broadcasted_iota(jnp.int32, (rows, ROUTE_LANES), 1)
    mine = (lane >= GROUP_LANE) & (lane < GROUP_LANE + N_GROUPS) & (rec.astype(F32) > 0.0)
    pos = jnp.sum(jnp.where(mine, cum - 1.0 + off_vec, 0.0), -1, keepdims=True)
    scatter = jnp.where(pos == lax.broadcasted_iota(jnp.int32, (rows, srows), 1).astype(F32), 1.0, 0.0).astype(BF16)
    pos_row = jnp.transpose(jnp.broadcast_to(pos, (rows, ROUTE_LANES)))[0:1, :]
    gather = jnp.where(lax.broadcasted_iota(jnp.int32, (srows, rows), 0).astype(F32) == pos_row, 1.0, 0.0).astype(BF16)
    xs_ref[...] = jnp.dot(gather, ext[:, 0:D_MODEL], preferred_element_type=F32).astype(BF16)
    rs = jnp.dot(gather, rec, preferred_element_type=F32)
    cw_ref[...] = rs + pltpu.roll(rs, ROUTE_LANES - COMB_LO_LANE, 1)

    def block(g, i, carry):
        r0 = pl.multiple_of(offs[g] + i * MOE_BLK, MOE_ALIGN)
        xb = xs_ref[pl.ds(r0, MOE_BLK), :]
        cwb = cw_ref[pl.ds(r0, MOE_BLK), :]
        parts = []
        for j in range(EXPERTS_PER_GROUP):
            e = g * EXPERTS_PER_GROUP + j
            hg = jnp.dot(xb, wg_ref[e], preferred_element_type=F32)
            hu = jnp.dot(xb, wu_ref[e], preferred_element_type=F32)
            parts.append((hg * jax.nn.sigmoid(hg) * hu * cwb[:, e:e + 1]).astype(BF16))
        act = jnp.concatenate(parts, axis=1)
        ys_ref[pl.ds(r0, MOE_BLK), :] = jnp.dot(act, wd_ref[g], preferred_element_type=F32).astype(BF16)
        return carry

    for g in range(N_GROUPS):
        lax.fori_loop(0, (counts[g] + (MOE_BLK - 1)) // MOE_BLK, functools.partial(block, g), 0)

    ff = jnp.dot(scatter, ys_ref[...], preferred_element_type=F32)
    g2 = mod_ref[:, 5:6, :]
    r = ALPHA * x1_ref[...] + (1.0 + g2) * ff.reshape(bb, tt, D_MODEL)
    y_ref[...] = _layer_norm_rows(r, g_ref[...], b_ref[...])


def _moe(x1, h2e, mod, w_gate, w_up, w_down, ln_g, ln_b, *, bb, tt):
    bsz, seq, d = x1.shape
    rows = bb * tt
    srows = _moe_rows(rows)
    tri = jnp.tril(jnp.ones((rows, rows), BF16))
    kern = functools.partial(_moe_kernel, bb=bb, tt=tt)
    tile = lambda i, j: (i, j, 0)
    const2 = lambda i, j: (0, 0)
    const3 = lambda i, j: (0, 0, 0)
    resident = pl.Buffered(1)
    return pl.pallas_call(
        kern,
        grid=(bsz // bb, seq // tt),
        in_specs=[pl.BlockSpec((bb, tt, d), tile),
                  pl.BlockSpec((bb, tt, d + ROUTE_LANES), tile),
                  pl.BlockSpec((bb, 6, d), lambda i, j: (i, 0, 0)),
                  pl.BlockSpec((rows, rows), const2, pipeline_mode=resident),
                  pl.BlockSpec(w_gate.shape, const3, pipeline_mode=resident),
                  pl.BlockSpec(w_up.shape, const3, pipeline_mode=resident),
                  pl.BlockSpec(w_down.shape, const3, pipeline_mode=resident),
                  pl.BlockSpec((1, d), const2),
                  pl.BlockSpec((1, d), const2)],
        out_specs=pl.BlockSpec((bb, tt, d), tile),
        out_shape=jax.ShapeDtypeStruct((bsz, seq, d), F32),
        scratch_shapes=[pltpu.VMEM((srows, d), BF16),
                        pltpu.VMEM((srows, ROUTE_LANES), F32),
                        pltpu.VMEM((srows, d), BF16)],
        compiler_params=_cparams("arbitrary", "arbitrary"),
        name="moe",
    )(x1, h2e, mod, tri, w_gate, w_up, w_down, ln_g, ln_b)


def _stream(x, mod, k0, v0, u0, p, *, mixer_tm, masked, bb, tt, moe_tt):
    mixin, nk, nv, nu = _mixer(x, mod, p["w_in"], p["bias"], p["sink"], p["w_dw"], p["b_dw"],
                               p["cln_g"], p["cln_b"], k0, v0, u0, tm=mixer_tm, masked=masked)
    x1, h2e = _post(x, mixin, mod, p["w_out"], p["ln1_g"], p["ln1_b"], p["w_route_split"], p["w_route_hi"],
                    p["b_route"], bb=bb, tt=tt)
    y = _moe(x1, h2e, mod, p["w_gate"], p["w_up"], p["w_down"], p["ln2_g"], p["ln2_b"], bb=bb, tt=moe_tt)
    bsz = x.shape[0]
    cache_shape = (1, bsz, WINDOW, N_KV_HEADS, HEAD_DIM)
    return y, nk.reshape(cache_shape), nv.reshape(cache_shape), nu[None]


def kernel(x_prompt, x_sample, cache_attn_k, cache_attn_v, state_conv, c_prompt, c_sample, rel_bias, w_ada, b_ada, w_in, attn_sinks, w_dw, b_dw, conv_ln_g, conv_ln_b, w_out, ln1_g, ln1_b, w_group, b_group, w_erouter, b_erouter, w_gate, w_up, w_down, ln2_g, ln2_b):
    bp = x_prompt.shape[0]
    bs = x_sample.shape[0]
    mod = _modulation(jnp.concatenate([c_prompt, c_sample], 0), w_ada[0], b_ada[0])
    mod = mod.reshape(bp + bs, 6, D_MODEL)

    w_route = jnp.concatenate([w_erouter[0].reshape(D_MODEL, N_EXPERTS), w_group[0]], 1)
    w_route = jnp.pad(w_route, ((0, 0), (0, ROUTE_LANES - N_EXPERTS - N_GROUPS)))
    b_route = jnp.concatenate([b_erouter[0].reshape(N_EXPERTS), b_group[0]])
    b_route = jnp.pad(b_route, (0, ROUTE_LANES - N_EXPERTS - N_GROUPS)).reshape(1, ROUTE_LANES)
    w_route_hi = w_route.astype(BF16)
    w_route_lo = (w_route - w_route_hi.astype(F32)).astype(BF16)

    p = {
        "w_in": w_in[0].astype(BF16),
        "bias": _relative_bias(rel_bias),
        "sink": attn_sinks[0],
        "w_dw": jnp.broadcast_to(w_dw[0][:, None, :], (CONV_K, SUBLANES, CONV_CH)),
        "b_dw": b_dw[0].reshape(1, CONV_CH),
        "cln_g": conv_ln_g[0].reshape(1, CONV_CH), "cln_b": conv_ln_b[0].reshape(1, CONV_CH),
        "w_out": w_out[0].astype(BF16),
        "ln1_g": ln1_g[0].reshape(1, D_MODEL), "ln1_b": ln1_b[0].reshape(1, D_MODEL),
        "w_route_split": jnp.concatenate([w_route_hi, w_route_lo], 1), "w_route_hi": w_route_hi,
        "b_route": b_route,
        "w_gate": w_gate[0].astype(BF16), "w_up": w_up[0].astype(BF16),
        "w_down": w_down[0].astype(BF16).reshape(N_GROUPS, EXPERTS_PER_GROUP * D_EXPERT, D_MODEL),
        "ln2_g": ln2_g[0].reshape(1, D_MODEL), "ln2_b": ln2_b[0].reshape(1, D_MODEL),
    }

    zk = jnp.zeros((bp, WINDOW, KV_WIDTH), F32)
    zu = jnp.zeros((bp, CONV_PAD, CONV_CH), F32)
    yp, pk, pv, pc = _stream(x_prompt, mod[:bp], zk, zk, zu, p, mixer_tm=512, masked=True, bb=1, tt=512,
                             moe_tt=512)

    k0 = cache_attn_k[0].reshape(bs, WINDOW, KV_WIDTH)
    v0 = cache_attn_v[0].reshape(bs, WINDOW, KV_WIDTH)
    u0 = jnp.pad(state_conv[0], ((0, 0), (CONV_PAD - CONV_HIST, 0), (0, 0)))
    ts = x_sample.shape[1]
    ys, sk, sv, sc = _stream(x_sample, mod[bp:], k0, v0, u0, p, mixer_tm=ts, masked=False, bb=8, tt=ts,
                             moe_tt=ts)
    return yp, ys, pk, pv, pc, sk, sv, sc
```

```python
import functools
import math

import jax
import jax.numpy as jnp
from jax import lax
from jax.experimental import pallas as pl
from jax.experimental.pallas import tpu as pltpu

D_MODEL = 1024
CHUNK = 64
N_HEADS = 8
N_KV_HEADS = 2
HEAD_DIM = 64
Q_GROUP = N_HEADS // N_KV_HEADS
ATTN_WIDTH = N_HEADS * HEAD_DIM
KV_WIDTH = N_KV_HEADS * HEAD_DIM
WINDOW = 128
WINDOW_CHUNKS = WINDOW // CHUNK
KEYS = WINDOW + CHUNK
CONV_CH = D_MODEL // 2
CONV_K = 31
CONV_HIST = CONV_K - 1
CONV_PAD = 32
SUBLANES = 8
CONV_ROWS = 256
CONV_LANES = 128
MIX_WIDTH = ATTN_WIDTH + CONV_CH
IN_WIDTH = ATTN_WIDTH + 2 * KV_WIDTH + 2 * CONV_CH
N_BUCKETS = 32
MAX_DISTANCE = 128
N_GROUPS = 4
EXPERTS_PER_GROUP = 4
N_EXPERTS = N_GROUPS * EXPERTS_PER_GROUP
D_EXPERT = D_MODEL // 4
DEPTH = 1
ALPHA = (2 * DEPTH) ** 0.25
LN_EPS = 1e-5
NEG_INF = -1e30
ROUTE_LANES = 128
COMB_LO_LANE = 16
GROUP_LANE = 32
MOE_BLK = 160
MOE_ALIGN = 16
VMEM_LIMIT = 56 * 1024 * 1024

BF16 = jnp.bfloat16
F32 = jnp.float32


def _cparams(*sem):
    return pltpu.CompilerParams(dimension_semantics=sem, vmem_limit_bytes=VMEM_LIMIT)


def _mod_kernel(c_ref, w_ref, b_ref, o_ref):
    c = c_ref[...]
    s = c * jax.nn.sigmoid(c)
    o_ref[...] = jnp.dot(s, w_ref[...], preferred_element_type=F32,
                         precision=lax.Precision.HIGHEST) + b_ref[...]


def _modulation(c, w_ada, b_ada):
    n, d = c.shape
    width = w_ada.shape[1]
    tn = 1536
    return pl.pallas_call(
        _mod_kernel,
        grid=(width // tn,),
        in_specs=[pl.BlockSpec((n, d), lambda j: (0, 0)),
                  pl.BlockSpec((d, tn), lambda j: (0, j)),
                  pl.BlockSpec((1, tn), lambda j: (0, j))],
        out_specs=pl.BlockSpec((n, tn), lambda j: (0, j)),
        out_shape=jax.ShapeDtypeStruct((n, width), F32),
        compiler_params=_cparams("arbitrary"),
        name="modulation",
    )(c, w_ada, b_ada.reshape(1, width))


def _t5_bucket(rel):
    nb = N_BUCKETS // 2
    max_exact = nb // 2
    ret = jnp.where(rel > 0, nb, 0)
    n = jnp.abs(rel)
    large = max_exact + (jnp.log(jnp.maximum(n, 1).astype(jnp.float32) / max_exact)
                         / math.log(MAX_DISTANCE / max_exact) * (nb - max_exact)).astype(jnp.int32)
    large = jnp.minimum(large, nb - 1)
    return ret + jnp.where(n < max_exact, n, large)


def _bias_kernel(table_ref, bucket_ref, o_ref):
    bucket = bucket_ref[...]
    col = lax.broadcasted_iota(jnp.int32, bucket.shape, 1)
    for h in range(N_HEADS):
        acc = jnp.zeros(bucket.shape, F32)
        for b in range(N_BUCKETS):
            acc = jnp.where(bucket == b, table_ref[b, h], acc)
        for v in range(WINDOW_CHUNKS + 1):
            o_ref[v, h] = jnp.where(col >= v * CHUNK, acc, NEG_INF)


def _relative_bias(table):
    rel = jnp.arange(KEYS)[None, :] - WINDOW - jnp.arange(CHUNK)[:, None]
    bucket = _t5_bucket(rel).astype(jnp.int32)
    nv = WINDOW_CHUNKS + 1
    bias = pl.pallas_call(
        _bias_kernel,
        in_specs=[pl.BlockSpec(memory_space=pltpu.SMEM),
                  pl.BlockSpec((CHUNK, KEYS), lambda: (0, 0))],
        out_specs=pl.BlockSpec((nv, N_HEADS, CHUNK, KEYS), lambda: (0, 0, 0, 0)),
        out_shape=jax.ShapeDtypeStruct((nv, N_HEADS, CHUNK, KEYS), F32),
        name="relative_bias",
    )(table, bucket)
    bias = bias.reshape(nv, N_HEADS // 2, 2, CHUNK, KEYS)
    return jnp.transpose(bias, (0, 1, 3, 2, 4)).reshape(nv, N_HEADS // 2, CHUNK, 2 * KEYS)


def _layer_norm_rows(x, g, b):
    mu = jnp.mean(x, -1, keepdims=True)
    xc = x - mu
    var = jnp.mean(xc * xc, -1, keepdims=True)
    return xc * lax.rsqrt(var + LN_EPS) * g + b


def _mixer_kernel(x_ref, mod_ref, win_ref, bias_ref, sink_ref, wdw_ref, bdw_ref, clg_ref, clb_ref,
                  k0_ref, v0_ref, u0_ref,
                  mix_ref, nk_ref, nv_ref, nu_ref,
                  kz, vz, ush, yconv, s_buf, p_buf, e_buf, *, tm, masked):
    t = pl.program_id(1)
    nt = pl.num_programs(1)
    n_chunks = tm // CHUNK
    ucat = ush.at[0]

    left = lax.broadcasted_iota(jnp.int32, (1, KV_WIDTH), 1) < HEAD_DIM

    def store_kv(row0, k, v):
        n = k.shape[0]
        for dst, val in ((kz, k), (vz, v)):
            swapped = pltpu.roll(val, HEAD_DIM, 1)
            dst[0, row0:row0 + n, 0:KV_WIDTH] = jnp.where(left, val, 0.0).astype(BF16)
            dst[1, row0:row0 + n, 0:KV_WIDTH] = jnp.where(left, 0.0, swapped).astype(BF16)
            dst[2, row0:row0 + n, 0:KV_WIDTH] = jnp.where(left, swapped, 0.0).astype(BF16)
            dst[3, row0:row0 + n, 0:KV_WIDTH] = jnp.where(left, 0.0, val).astype(BF16)

    @pl.when(t == 0)
    def _():
        ones_left = jnp.broadcast_to(jnp.where(left, 1.0, 0.0).astype(BF16), (WINDOW + tm, KV_WIDTH))
        ones_right = jnp.broadcast_to(jnp.where(left, 0.0, 1.0).astype(BF16), (WINDOW + tm, KV_WIDTH))
        for i in range(2 * N_KV_HEADS):
            vz[i, :, KV_WIDTH:] = ones_left if i % 2 == 0 else ones_right
        store_kv(0, k0_ref[0], v0_ref[0])
        ucat[0:CONV_PAD, :] = u0_ref[0]

    sh1 = mod_ref[0, 0:1, :]
    sc1 = mod_ref[0, 1:2, :]
    h = (x_ref[0] * (1.0 + sc1) + sh1).astype(BF16)
    o1 = ATTN_WIDTH
    o3 = o1 + 2 * KV_WIDTH
    q = (jnp.dot(h, win_ref[:, 0:o1], preferred_element_type=F32) * (HEAD_DIM ** -0.5)).astype(BF16)
    kv = jnp.dot(h, win_ref[:, o1:o3], preferred_element_type=F32)
    ag = jnp.dot(h, win_ref[:, o3:IN_WIDTH], preferred_element_type=F32)
    u = ag[:, 0:CONV_CH] * jax.nn.sigmoid(ag[:, CONV_CH:])
    store_kv(WINDOW, kv[:, 0:KV_WIDTH], kv[:, KV_WIDTH:])
    ucat[CONV_PAD:CONV_PAD + tm, :] = u

    if tm >= WINDOW:
        nk_ref[0] = kv[tm - WINDOW:, 0:KV_WIDTH]
        nv_ref[0] = kv[tm - WINDOW:, KV_WIDTH:]
    else:
        nk_ref[0, 0:WINDOW - tm, :] = k0_ref[0, tm:WINDOW, :]
        nv_ref[0, 0:WINDOW - tm, :] = v0_ref[0, tm:WINDOW, :]
        nk_ref[0, WINDOW - tm:, :] = kv[:, 0:KV_WIDTH]
        nv_ref[0, WINDOW - tm:, :] = kv[:, KV_WIDTH:]
    nu_ref[0] = ucat[CONV_PAD + tm - CONV_HIST:CONV_PAD + tm, :]

    n_pairs = N_HEADS // 2
    pair_w = 2 * HEAD_DIM
    first_head = lax.broadcasted_iota(jnp.int32, (CHUNK, pair_w), 1) < HEAD_DIM
    nt_dims = (((1,), (1,)), ((), ()))

    def window(ref, j, c):
        kvh = (2 * j) // Q_GROUP
        rows = slice(c * CHUNK, c * CHUNK + KEYS)
        return jnp.concatenate([ref[2 * kvh, rows, :], ref[2 * kvh + 1, rows, :]], axis=0)

    for c in range(n_chunks):
        if masked and c < WINDOW_CHUNKS:
            variant = jnp.where(t == 0, WINDOW_CHUNKS - c, 0)
        else:
            variant = 0
        for j in range(n_pairs):
            qp = q[c * CHUNK:(c + 1) * CHUNK, j * pair_w:(j + 1) * pair_w]
            s_buf[c, j] = (lax.dot_general(qp, window(kz, j, c), nt_dims, preferred_element_type=F32)
                           + bias_ref[variant, j])

    for c in range(n_chunks):
        for j in range(n_pairs):
            s0 = s_buf[c, j, :, 0:pair_w]
            s1 = s_buf[c, j, :, pair_w:2 * pair_w]
            s2 = s_buf[c, j, :, 2 * pair_w:]
            sink_a = sink_ref[2 * j]
            sink_b = sink_ref[2 * j + 1]
            m_a = jnp.maximum(jnp.max(jnp.maximum(s0, jnp.where(first_head, s1, NEG_INF)), -1, keepdims=True), sink_a)
            m_b = jnp.maximum(jnp.max(jnp.maximum(jnp.where(first_head, NEG_INF, s1), s2), -1, keepdims=True), sink_b)
            p_buf[c, j, :, 0:pair_w] = jnp.exp(s0 - m_a).astype(BF16)
            p_buf[c, j, :, pair_w:2 * pair_w] = jnp.exp(s1 - jnp.where(first_head, m_a, m_b)).astype(BF16)
            p_buf[c, j, :, 2 * pair_w:] = jnp.exp(s2 - m_b).astype(BF16)
            e_buf[c, j] = jnp.where(first_head, jnp.exp(sink_a - m_a), jnp.exp(sink_b - m_b))

    span = tm + CONV_PAD - SUBLANES
    for s in range(1, SUBLANES):
        ush[s, 0:span, :] = ucat[s:s + span, :]
    off = CONV_PAD - CONV_HIST
    rb = min(CONV_ROWS, tm)
    groups = rb // SUBLANES

    def conv_rows(lanes, r, carry):
        r0 = pl.multiple_of(r * rb, rb)
        partial = []
        for shift in range(SUBLANES):
            taps = [k for k in range(CONV_K) if (off + k) % SUBLANES == shift]
            top = max(off + k - shift for k in taps)
            slab = ush[shift, pl.ds(r0, rb + top), lanes].reshape(groups + top // SUBLANES, SUBLANES, CONV_LANES)
            acc = None
            for k in taps:
                g0 = (off + k - shift) // SUBLANES
                term = slab[g0:g0 + groups] * wdw_ref[k, :, lanes]
                acc = term if acc is None else acc + term
            partial.append(acc)
        while len(partial) > 1:
            partial = [a + b for a, b in zip(partial[0::2], partial[1::2])]
        yconv[pl.ds(r0, rb), lanes] = (partial[0] + bdw_ref[:, lanes]).reshape(rb, CONV_LANES)
        return carry

    for part in range(CONV_CH // CONV_LANES):
        lanes = slice(part * CONV_LANES, (part + 1) * CONV_LANES)
        lax.fori_loop(0, tm // rb, functools.partial(conv_rows, lanes), 0)
    y = _layer_norm_rows(yconv[...], clg_ref[...], clb_ref[...])
    mix_ref[0, :, ATTN_WIDTH:] = (y * jax.nn.sigmoid(y)).astype(BF16)

    for c in range(n_chunks):
        for j in range(n_pairs):
            res = jnp.dot(p_buf[c, j], window(vz, j, c), preferred_element_type=F32)
            out = res[:, 0:pair_w] * (1.0 / (res[:, pair_w:] + e_buf[c, j]))
            mix_ref[0, c * CHUNK:(c + 1) * CHUNK, j * pair_w:(j + 1) * pair_w] = out.astype(BF16)

    @pl.when(t < nt - 1)
    def _():
        if tm >= WINDOW:
            for i in range(2 * N_KV_HEADS):
                kz[i, 0:WINDOW, :] = kz[i, tm:tm + WINDOW, :]
                vz[i, 0:WINDOW, 0:KV_WIDTH] = vz[i, tm:tm + WINDOW, 0:KV_WIDTH]
            ucat[0:CONV_PAD, :] = ucat[tm:tm + CONV_PAD, :]


def _mixer(x, mod, w_in, bias, sink, w_dw, b_dw, cln_g, cln_b, k0, v0, u0, *, tm, masked):
    bsz, seq, d = x.shape
    assert seq % tm == 0 and tm % CHUNK == 0 and (tm >= WINDOW or seq == tm)
    nt = seq // tm
    kern = functools.partial(_mixer_kernel, tm=tm, masked=masked)
    const2 = lambda b, t: (0, 0)
    const3 = lambda b, t: (0, 0, 0)
    per_b = lambda b, t: (b, 0, 0)
    return pl.pallas_call(
        kern,
        grid=(bsz, nt),
        in_specs=[pl.BlockSpec((1, tm, d), lambda b, t: (b, t, 0)),
                  pl.BlockSpec((1, 6, d), per_b),
                  pl.BlockSpec((d, IN_WIDTH), const2),
                  pl.BlockSpec((WINDOW_CHUNKS + 1, N_HEADS // 2, CHUNK, 2 * KEYS), lambda b, t: (0, 0, 0, 0)),
                  pl.BlockSpec(memory_space=pltpu.SMEM),
                  pl.BlockSpec((CONV_K, SUBLANES, CONV_CH), const3),
                  pl.BlockSpec((1, CONV_CH), const2),
                  pl.BlockSpec((1, CONV_CH), const2),
                  pl.BlockSpec((1, CONV_CH), const2),
                  pl.BlockSpec((1, WINDOW, KV_WIDTH), per_b),
                  pl.BlockSpec((1, WINDOW, KV_WIDTH), per_b),
                  pl.BlockSpec((1, CONV_PAD, CONV_CH), per_b)],
        out_specs=[pl.BlockSpec((1, tm, MIX_WIDTH), lambda b, t: (b, t, 0)),
                   pl.BlockSpec((1, WINDOW, KV_WIDTH), per_b),
                   pl.BlockSpec((1, WINDOW, KV_WIDTH), per_b),
                   pl.BlockSpec((1, CONV_HIST, CONV_CH), per_b)],
        out_shape=[jax.ShapeDtypeStruct((bsz, seq, MIX_WIDTH), BF16),
                   jax.ShapeDtypeStruct((bsz, WINDOW, KV_WIDTH), F32),
                   jax.ShapeDtypeStruct((bsz, WINDOW, KV_WIDTH), F32),
                   jax.ShapeDtypeStruct((bsz, CONV_HIST, CONV_CH), F32)],
        scratch_shapes=[pltpu.VMEM((2 * N_KV_HEADS, WINDOW + tm, KV_WIDTH), BF16),
                        pltpu.VMEM((2 * N_KV_HEADS, WINDOW + tm, 2 * KV_WIDTH), BF16),
                        pltpu.VMEM((SUBLANES, CONV_PAD + tm, CONV_CH), F32),
                        pltpu.VMEM((tm, CONV_CH), F32),
                        pltpu.VMEM((tm // CHUNK, N_HEADS // 2, CHUNK, 2 * KEYS), F32),
                        pltpu.VMEM((tm // CHUNK, N_HEADS // 2, CHUNK, 2 * KEYS), BF16),
                        pltpu.VMEM((tm // CHUNK, N_HEADS // 2, CHUNK, 2 * HEAD_DIM), F32)],
        compiler_params=_cparams("parallel", "arbitrary"),
        name="mixer",
    )(x, mod, w_in, bias, sink, w_dw, b_dw, cln_g, cln_b, k0, v0, u0)


def _post_kernel(x_ref, mix_ref, mod_ref, wout_ref, g_ref, b_ref, wr_split_ref, wr_hi_ref, br_ref,
                 x1_ref, h2e_ref, *, bb, tt):
    rows = bb * tt
    mix = jnp.dot(mix_ref[...].reshape(rows, MIX_WIDTH), wout_ref[...], preferred_element_type=F32)
    g1 = mod_ref[:, 2:3, :]
    sh2 = mod_ref[:, 3:4, :]
    sc2 = mod_ref[:, 4:5, :]
    r = ALPHA * x_ref[...] + (1.0 + g1) * mix.reshape(bb, tt, D_MODEL)
    x1 = _layer_norm_rows(r, g_ref[...], b_ref[...])
    x1_ref[...] = x1
    h2 = (x1 * (1.0 + sc2) + sh2).reshape(rows, D_MODEL)
    h2_hi = h2.astype(BF16)
    h2e_ref[:, :, 0:D_MODEL] = h2_hi.reshape(bb, tt, D_MODEL)

    h2_lo = (h2 - h2_hi.astype(F32)).astype(BF16)
    hi_terms = jnp.dot(h2_hi, wr_split_ref[...], preferred_element_type=F32)
    logits = (hi_terms[:, 0:ROUTE_LANES] + hi_terms[:, ROUTE_LANES:]
              + jnp.dot(h2_lo, wr_hi_ref[...], preferred_element_type=F32) + br_ref[...])
    lane = lax.broadcasted_iota(jnp.int32, (rows, ROUTE_LANES), 1).astype(F32)
    far = float(ROUTE_LANES)
    is_group = (lane >= N_EXPERTS) & (lane < N_EXPERTS + N_GROUPS)
    gl = jnp.where(is_group, logits, NEG_INF)
    gmax = jnp.max(gl, -1, keepdims=True)
    gidx = jnp.min(jnp.where(is_group & (gl == gmax), lane, far), -1, keepdims=True) - N_EXPERTS
    pg = 1.0 / jnp.sum(jnp.where(is_group, jnp.exp(gl - gmax), 0.0), -1, keepdims=True)
    in_group = (lane >= gidx * EXPERTS_PER_GROUP) & (lane < (gidx + 1) * EXPERTS_PER_GROUP)
    el = jnp.where(in_group, logits, NEG_INF)
    v1 = jnp.max(el, -1, keepdims=True)
    i1 = jnp.min(jnp.where(in_group & (el == v1), lane, far), -1, keepdims=True)
    rest = in_group & (lane != i1)
    el2 = jnp.where(rest, logits, NEG_INF)
    v2 = jnp.max(el2, -1, keepdims=True)
    i2 = jnp.min(jnp.where(rest & (el2 == v2), lane, far), -1, keepdims=True)
    e2 = jnp.exp(v2 - v1)
    w1 = pg / (1.0 + e2)
    w2 = pg * e2 / (1.0 + e2)
    comb = jnp.where(lane == i1, w1, jnp.where(lane == i2, w2, 0.0))

    c_hi = comb.astype(BF16).astype(F32)
    c_lo = (comb - c_hi).astype(BF16).astype(F32)
    rec = c_hi + pltpu.roll(c_lo, COMB_LO_LANE, 1) + jnp.where(lane == gidx + GROUP_LANE, 1.0, 0.0)
    h2e_ref[:, :, D_MODEL:] = rec.astype(BF16).reshape(bb, tt, ROUTE_LANES)


def _moe_rows(rows):
    return rows + N_GROUPS * MOE_ALIGN + MOE_BLK


def _moe_kernel(x1_ref, h2e_ref, mod_ref, tri_ref, wg_ref, wu_ref, wd_ref, g_ref, b_ref, y_ref,
                xs_ref, cw_ref, ys_ref, *, bb, tt):
    rows = bb * tt
    srows = _moe_rows(rows)
    ext = h2e_ref[...].reshape(rows, D_MODEL + ROUTE_LANES)
    rec = ext[:, D_MODEL:]
    cum = jnp.dot(tri_ref[...], rec, preferred_element_type=F32)
    lane1 = lax.broadcasted_iota(jnp.int32, (1, ROUTE_LANES), 1)
    cnt = cum[rows - 1:rows, :]
    off = jnp.int32(0)
    offs, counts = [], []
    off_vec = jnp.zeros((1, ROUTE_LANES), F32)
    for g in range(N_GROUPS):
        n_g = jnp.sum(jnp.where(lane1 == GROUP_LANE + g, cnt, 0.0)).astype(jnp.int32)
        offs.append(off)
        counts.append(n_g)
        off_vec = jnp.where(lane1 == GROUP_LANE + g, off.astype(F32), off_vec)
        off = off + ((n_g + (MOE_ALIGN - 1)) // MOE_ALIGN) * MOE_ALIGN

    lane = lax.broadcasted_iota(jnp.int32, (rows, ROUTE_LANES), 1)
    mine = (lane >= GROUP_LANE) & (lane < GROUP_LANE + N_GROUPS) & (rec.astype(F32) > 0.0)
    pos = jnp.sum(jnp.where(mine, cum - 1.0 + off_vec, 0.0), -1, keepdims=True)
    scatter = jnp.where(pos == lax.broadcasted_iota(jnp.int32, (rows, srows), 1).astype(F32), 1.0, 0.0).astype(BF16)
    pos_row = jnp.transpose(jnp.broadcast_to(pos, (rows, ROUTE_LANES)))[0:1, :]
    gather = jnp.where(lax.broadcasted_iota(jnp.int32, (srows, rows), 0).astype(F32) == pos_row, 1.0, 0.0).astype(BF16)
    xs_ref[...] = jnp.dot(gather, ext[:, 0:D_MODEL], preferred_element_type=F32).astype(BF16)
    rs = jnp.dot(gather, rec, preferred_element_type=F32)
    cw_ref[...] = rs + pltpu.roll(rs, ROUTE_LANES - COMB_LO_LANE, 1)

    def experts(g, r0):
        xb = xs_ref[pl.ds(r0, MOE_BLK), :]
        cwb = cw_ref[pl.ds(r0, MOE_BLK), :]
        parts = []
        for j in range(EXPERTS_PER_GROUP):
            e = g * EXPERTS_PER_GROUP + j
            hg = jnp.dot(xb, wg_ref[e], preferred_element_type=F32)
            hu = jnp.dot(xb, wu_ref[e], preferred_element_type=F32)
            parts.append((hg * jax.nn.sigmoid(hg) * hu * cwb[:, e:e + 1]).astype(BF16))
        act = jnp.concatenate(parts, axis=1)
        return jnp.dot(act, wd_ref[g], preferred_element_type=F32)

    for g in range(N_GROUPS):
        r0 = pl.multiple_of(offs[g], MOE_ALIGN)
        ys_ref[pl.ds(r0, MOE_BLK), :] = experts(g, r0).astype(BF16)

    def extra_block(g, i, carry):
        r0 = pl.multiple_of(offs[g] + i * MOE_BLK, MOE_ALIGN)
        row = r0 + lax.broadcasted_iota(jnp.int32, (MOE_BLK, 1), 0)
        old = ys_ref[pl.ds(r0, MOE_BLK), :].astype(F32)
        ys_ref[pl.ds(r0, MOE_BLK), :] = jnp.where(row < offs[g] + counts[g], experts(g, r0), old).astype(BF16)
        return carry

    for g in range(N_GROUPS):
        lax.fori_loop(1, (counts[g] + (MOE_BLK - 1)) // MOE_BLK, functools.partial(extra_block, g), 0)

    ff = jnp.dot(scatter, ys_ref[...], preferred_element_type=F32)
    g2 = mod_ref[:, 5:6, :]
    r = ALPHA * x1_ref[...] + (1.0 + g2) * ff.reshape(bb, tt, D_MODEL)
    y_ref[...] = _layer_norm_rows(r, g_ref[...], b_ref[...])


def _ffn_kernel(x_ref, mix_ref, mod_cur_ref, mod_prev_ref, wout_ref, g1_ref, b1_ref, wr_split_ref, wr_hi_ref,
                br_ref, tri_ref, wg_ref, wu_ref, wd_ref, g2_ref, b2_ref, y_ref,
                x1_buf, h2e_buf, xs_ref, cw_ref, ys_ref, *, bb, tt):
    s = pl.program_id(0)
    cur = s % 2
    prev = 1 - cur

    @pl.when(s == 0)
    def _():
        ys_ref[...] = jnp.zeros_like(ys_ref)
        x1_buf[1] = jnp.zeros(x1_buf.shape[1:], F32)
        h2e_buf[1] = jnp.zeros(h2e_buf.shape[1:], BF16)

    _post_kernel(x_ref, mix_ref, mod_cur_ref, wout_ref, g1_ref, b1_ref, wr_split_ref, wr_hi_ref, br_ref,
                 x1_buf.at[cur], h2e_buf.at[cur], bb=bb, tt=tt)
    _moe_kernel(x1_buf.at[prev], h2e_buf.at[prev], mod_prev_ref, tri_ref, wg_ref, wu_ref, wd_ref, g2_ref, b2_ref,
                y_ref, xs_ref, cw_ref, ys_ref, bb=bb, tt=tt)


def _ffn(x, mixin, mod, p, *, bb, tt):
    bsz, seq, d = x.shape
    assert bsz % bb == 0 and seq % tt == 0
    rows = bb * tt
    srows = _moe_rows(rows)
    n_j = seq // tt
    n_tiles = (bsz // bb) * n_j
    tri = jnp.tril(jnp.ones((rows, rows), BF16))
    kern = functools.partial(_ffn_kernel, bb=bb, tt=tt)
    cur = lambda s: jnp.minimum(s, n_tiles - 1)
    prev = lambda s: jnp.maximum(s - 1, 0)
    const2 = lambda s: (0, 0)
    const3 = lambda s: (0, 0, 0)
    resident = pl.Buffered(1)
    return pl.pallas_call(
        kern,
        grid=(n_tiles + 1,),
        in_specs=[pl.BlockSpec((bb, tt, d), lambda s: (cur(s) // n_j, cur(s) % n_j, 0)),
                  pl.BlockSpec((bb, tt, MIX_WIDTH), lambda s: (cur(s) // n_j, cur(s) % n_j, 0)),
                  pl.BlockSpec((bb, 6, d), lambda s: (cur(s) // n_j, 0, 0)),
                  pl.BlockSpec((bb, 6, d), lambda s: (prev(s) // n_j, 0, 0)),
                  pl.BlockSpec((MIX_WIDTH, d), const2, pipeline_mode=resident),
                  pl.BlockSpec((1, d), const2),
                  pl.BlockSpec((1, d), const2),
                  pl.BlockSpec((d, 2 * ROUTE_LANES), const2, pipeline_mode=resident),
                  pl.BlockSpec((d, ROUTE_LANES), const2, pipeline_mode=resident),
                  pl.BlockSpec((1, ROUTE_LANES), const2),
                  pl.BlockSpec((rows, rows), const2, pipeline_mode=resident),
                  pl.BlockSpec(p["w_gate"].shape, const3, pipeline_mode=resident),
                  pl.BlockSpec(p["w_up"].shape, const3, pipeline_mode=resident),
                  pl.BlockSpec(p["w_down"].shape, const3, pipeline_mode=resident),
                  pl.BlockSpec((1, d), const2),
                  pl.BlockSpec((1, d), const2)],
        out_specs=pl.BlockSpec((bb, tt, d), lambda s: (prev(s) // n_j, prev(s) % n_j, 0)),
        out_shape=jax.ShapeDtypeStruct((bsz, seq, d), F32),
        scratch_shapes=[pltpu.VMEM((2, bb, tt, d), F32),
                        pltpu.VMEM((2, bb, tt, d + ROUTE_LANES), BF16),
                        pltpu.VMEM((srows, d), BF16),
                        pltpu.VMEM((srows, ROUTE_LANES), F32),
                        pltpu.VMEM((srows, d), BF16)],
        compiler_params=_cparams("arbitrary"),
        name="ffn",
    )(x, mixin, mod, mod, p["w_out"], p["ln1_g"], p["ln1_b"], p["w_route_split"], p["w_route_hi"], p["b_route"],
      tri, p["w_gate"], p["w_up"], p["w_down"], p["ln2_g"], p["ln2_b"])


def _stream(x, mod, k0, v0, u0, p, *, mixer_tm, masked, bb, tt):
    mixin, nk, nv, nu = _mixer(x, mod, p["w_in"], p["bias"], p["sink"], p["w_dw"], p["b_dw"],
                               p["cln_g"], p["cln_b"], k0, v0, u0, tm=mixer_tm, masked=masked)
    y = _ffn(x, mixin, mod, p, bb=bb, tt=tt)
    bsz = x.shape[0]
    cache_shape = (1, bsz, WINDOW, N_KV_HEADS, HEAD_DIM)
    return y, nk.reshape(cache_shape), nv.reshape(cache_shape), nu[None]


def kernel(x_prompt, x_sample, cache_attn_k, cache_attn_v, state_conv, c_prompt, c_sample, rel_bias, w_ada, b_ada, w_in, attn_sinks, w_dw, b_dw, conv_ln_g, conv_ln_b, w_out, ln1_g, ln1_b, w_group, b_group, w_erouter, b_erouter, w_gate, w_up, w_down, ln2_g, ln2_b):
    bp = x_prompt.shape[0]
    bs = x_sample.shape[0]
    mod = _modulation(jnp.concatenate([c_prompt, c_sample], 0), w_ada[0], b_ada[0])
    mod = mod.reshape(bp + bs, 6, D_MODEL)

    w_route = jnp.concatenate([w_erouter[0].reshape(D_MODEL, N_EXPERTS), w_group[0]], 1)
    w_route = jnp.pad(w_route, ((0, 0), (0, ROUTE_LANES - N_EXPERTS - N_GROUPS)))
    b_route = jnp.concatenate([b_erouter[0].reshape(N_EXPERTS), b_group[0]])
    b_route = jnp.pad(b_route, (0, ROUTE_LANES - N_EXPERTS - N_GROUPS)).reshape(1, ROUTE_LANES)
    w_route_hi = w_route.astype(BF16)
    w_route_lo = (w_route - w_route_hi.astype(F32)).astype(BF16)

    p = {
        "w_in": w_in[0].astype(BF16),
        "bias": _relative_bias(rel_bias),
        "sink": attn_sinks[0],
        "w_dw": jnp.broadcast_to(w_dw[0][:, None, :], (CONV_K, SUBLANES, CONV_CH)),
        "b_dw": b_dw[0].reshape(1, CONV_CH),
        "cln_g": conv_ln_g[0].reshape(1, CONV_CH), "cln_b": conv_ln_b[0].reshape(1, CONV_CH),
        "w_out": w_out[0].astype(BF16),
        "ln1_g": ln1_g[0].reshape(1, D_MODEL), "ln1_b": ln1_b[0].reshape(1, D_MODEL),
        "w_route_split": jnp.concatenate([w_route_hi, w_route_lo], 1), "w_route_hi": w_route_hi,
        "b_route": b_route,
        "w_gate": w_gate[0].astype(BF16), "w_up": w_up[0].astype(BF16),
        "w_down": w_down[0].astype(BF16).reshape(N_GROUPS, EXPERTS_PER_GROUP * D_EXPERT, D_MODEL),
        "ln2_g": ln2_g[0].reshape(1, D_MODEL), "ln2_b": ln2_b[0].reshape(1, D_MODEL),
    }

    zk = jnp.zeros((bp, WINDOW, KV_WIDTH), F32)
    zu = jnp.zeros((bp, CONV_PAD, CONV_CH), F32)
    yp, pk, pv, pc = _stream(x_prompt, mod[:bp], zk, zk, zu, p, mixer_tm=512, masked=True, bb=1, tt=512)

    k0 = cache_attn_k[0].reshape(bs, WINDOW, KV_WIDTH)
    v0 = cache_attn_v[0].reshape(bs, WINDOW, KV_WIDTH)
    u0 = jnp.pad(state_conv[0], ((0, 0), (CONV_PAD - CONV_HIST, 0), (0, 0)))
    ts = x_sample.shape[1]
    ys, sk, sv, sc = _stream(x_sample, mod[bp:], k0, v0, u0, p, mixer_tm=ts, masked=False, bb=8, tt=ts)
    return yp, ys, pk, pv, pc, sk, sv, sc
```

```python
import functools
import math

import jax
import jax.numpy as jnp
from jax import lax
from jax.experimental import pallas as pl
from jax.experimental.pallas import tpu as pltpu

D_MODEL = 1024
CHUNK = 64
N_HEADS = 8
N_KV_HEADS = 2
HEAD_DIM = 64
Q_GROUP = N_HEADS // N_KV_HEADS
ATTN_WIDTH = N_HEADS * HEAD_DIM
KV_WIDTH = N_KV_HEADS * HEAD_DIM
WINDOW = 128
WINDOW_CHUNKS = WINDOW // CHUNK
KEYS = WINDOW + CHUNK
CONV_CH = D_MODEL // 2
CONV_K = 31
CONV_HIST = CONV_K - 1
CONV_PAD = 32
SUBLANES = 8
CONV_ROWS = 256
CONV_LANES = 128
MIX_WIDTH = ATTN_WIDTH + CONV_CH
IN_WIDTH = ATTN_WIDTH + 2 * KV_WIDTH + 2 * CONV_CH
N_BUCKETS = 32
MAX_DISTANCE = 128
N_GROUPS = 4
EXPERTS_PER_GROUP = 4
N_EXPERTS = N_GROUPS * EXPERTS_PER_GROUP
D_EXPERT = D_MODEL // 4
DEPTH = 1
ALPHA = (2 * DEPTH) ** 0.25
LN_EPS = 1e-5
NEG_INF = -1e30
LOG2E = math.log2(math.e)
ROUTE_LANES = 128
COMB_LO_LANE = 16
GROUP_LANE = 32
MOE_BLK = 160
MOE_ALIGN = 16
VMEM_LIMIT = 56 * 1024 * 1024

BF16 = jnp.bfloat16
F32 = jnp.float32


def _cparams(*sem):
    return pltpu.CompilerParams(dimension_semantics=sem, vmem_limit_bytes=VMEM_LIMIT)


def _mod_kernel(c_ref, w_ref, b_ref, o_ref):
    c = c_ref[...]
    s = c * jax.nn.sigmoid(c)
    o_ref[...] = jnp.dot(s, w_ref[...], preferred_element_type=F32,
                         precision=lax.Precision.HIGHEST) + b_ref[...]


def _modulation(c, w_ada, b_ada):
    n, d = c.shape
    width = w_ada.shape[1]
    tn = 1536
    return pl.pallas_call(
        _mod_kernel,
        grid=(width // tn,),
        in_specs=[pl.BlockSpec((n, d), lambda j: (0, 0)),
                  pl.BlockSpec((d, tn), lambda j: (0, j)),
                  pl.BlockSpec((1, tn), lambda j: (0, j))],
        out_specs=pl.BlockSpec((n, tn), lambda j: (0, j)),
        out_shape=jax.ShapeDtypeStruct((n, width), F32),
        compiler_params=_cparams("arbitrary"),
        name="modulation",
    )(c, w_ada, b_ada.reshape(1, width))


def _t5_bucket(rel):
    nb = N_BUCKETS // 2
    max_exact = nb // 2
    ret = jnp.where(rel > 0, nb, 0)
    n = jnp.abs(rel)
    large = max_exact + (jnp.log(jnp.maximum(n, 1).astype(jnp.float32) / max_exact)
                         / math.log(MAX_DISTANCE / max_exact) * (nb - max_exact)).astype(jnp.int32)
    large = jnp.minimum(large, nb - 1)
    return ret + jnp.where(n < max_exact, n, large)


def _bias_kernel(table_ref, bucket_ref, o_ref):
    bucket = bucket_ref[...]
    col = lax.broadcasted_iota(jnp.int32, bucket.shape, 1)
    for h in range(N_HEADS):
        acc = jnp.zeros(bucket.shape, F32)
        for b in range(N_BUCKETS):
            acc = jnp.where(bucket == b, table_ref[b, h], acc)
        for v in range(WINDOW_CHUNKS + 1):
            o_ref[v, h] = jnp.where(col >= v * CHUNK, acc * LOG2E, NEG_INF)


def _relative_bias(table):
    rel = jnp.arange(KEYS)[None, :] - WINDOW - jnp.arange(CHUNK)[:, None]
    bucket = _t5_bucket(rel).astype(jnp.int32)
    nv = WINDOW_CHUNKS + 1
    bias = pl.pallas_call(
        _bias_kernel,
        in_specs=[pl.BlockSpec(memory_space=pltpu.SMEM),
                  pl.BlockSpec((CHUNK, KEYS), lambda: (0, 0))],
        out_specs=pl.BlockSpec((nv, N_HEADS, CHUNK, KEYS), lambda: (0, 0, 0, 0)),
        out_shape=jax.ShapeDtypeStruct((nv, N_HEADS, CHUNK, KEYS), F32),
        name="relative_bias",
    )(table, bucket)
    bias = bias.reshape(nv, N_HEADS // 2, 2, CHUNK, KEYS)
    return jnp.transpose(bias, (0, 1, 3, 2, 4)).reshape(nv, N_HEADS // 2, CHUNK, 2 * KEYS)


def _layer_norm_rows(x, g, b):
    mu = jnp.mean(x, -1, keepdims=True)
    xc = x - mu
    var = jnp.mean(xc * xc, -1, keepdims=True)
    return xc * lax.rsqrt(var + LN_EPS) * g + b


def _mixer_kernel(x_ref, mod_ref, win_ref, bias_ref, sink_ref, wdw_ref, bdw_ref, clg_ref, clb_ref,
                  k0_ref, v0_ref, u0_ref,
                  mix_ref, nk_ref, nv_ref, nu_ref,
                  kz, vz, ush, yconv, s_buf, p_buf, e_buf, *, tm, masked):
    t = pl.program_id(1)
    nt = pl.num_programs(1)
    n_chunks = tm // CHUNK
    ucat = ush.at[0]

    left = lax.broadcasted_iota(jnp.int32, (1, KV_WIDTH), 1) < HEAD_DIM

    def store_kv(row0, k, v):
        n = k.shape[0]
        for dst, val in ((kz, k), (vz, v)):
            swapped = pltpu.roll(val, HEAD_DIM, 1)
            dst[0, row0:row0 + n, 0:KV_WIDTH] = jnp.where(left, val, 0.0).astype(BF16)
            dst[1, row0:row0 + n, 0:KV_WIDTH] = jnp.where(left, 0.0, swapped).astype(BF16)
            dst[2, row0:row0 + n, 0:KV_WIDTH] = jnp.where(left, swapped, 0.0).astype(BF16)
            dst[3, row0:row0 + n, 0:KV_WIDTH] = jnp.where(left, 0.0, val).astype(BF16)

    @pl.when(t == 0)
    def _():
        ones_left = jnp.broadcast_to(jnp.where(left, 1.0, 0.0).astype(BF16), (WINDOW + tm, KV_WIDTH))
        ones_right = jnp.broadcast_to(jnp.where(left, 0.0, 1.0).astype(BF16), (WINDOW + tm, KV_WIDTH))
        for i in range(2 * N_KV_HEADS):
            vz[i, :, KV_WIDTH:] = ones_left if i % 2 == 0 else ones_right
        store_kv(0, k0_ref[0], v0_ref[0])
        ucat[0:CONV_PAD, :] = u0_ref[0]

    sh1 = mod_ref[0, 0:1, :]
    sc1 = mod_ref[0, 1:2, :]
    h = (x_ref[0] * (1.0 + sc1) + sh1).astype(BF16)
    o1 = ATTN_WIDTH
    o3 = o1 + 2 * KV_WIDTH
    q = (jnp.dot(h, win_ref[:, 0:o1], preferred_element_type=F32) * (HEAD_DIM ** -0.5 * LOG2E)).astype(BF16)
    kv = jnp.dot(h, win_ref[:, o1:o3], preferred_element_type=F32)
    ag = jnp.dot(h, win_ref[:, o3:IN_WIDTH], preferred_element_type=F32)
    u = ag[:, 0:CONV_CH] * jax.nn.sigmoid(ag[:, CONV_CH:])
    store_kv(WINDOW, kv[:, 0:KV_WIDTH], kv[:, KV_WIDTH:])
    ucat[CONV_PAD:CONV_PAD + tm, :] = u

    if tm >= WINDOW:
        nk_ref[0] = kv[tm - WINDOW:, 0:KV_WIDTH]
        nv_ref[0] = kv[tm - WINDOW:, KV_WIDTH:]
    else:
        nk_ref[0, 0:WINDOW - tm, :] = k0_ref[0, tm:WINDOW, :]
        nv_ref[0, 0:WINDOW - tm, :] = v0_ref[0, tm:WINDOW, :]
        nk_ref[0, WINDOW - tm:, :] = kv[:, 0:KV_WIDTH]
        nv_ref[0, WINDOW - tm:, :] = kv[:, KV_WIDTH:]
    nu_ref[0] = ucat[CONV_PAD + tm - CONV_HIST:CONV_PAD + tm, :]

    n_pairs = N_HEADS // 2
    pair_w = 2 * HEAD_DIM
    first_head = lax.broadcasted_iota(jnp.int32, (CHUNK, pair_w), 1) < HEAD_DIM
    nt_dims = (((1,), (1,)), ((), ()))

    def window(ref, j, c):
        kvh = (2 * j) // Q_GROUP
        rows = slice(c * CHUNK, c * CHUNK + KEYS)
        return jnp.concatenate([ref[2 * kvh, rows, :], ref[2 * kvh + 1, rows, :]], axis=0)

    for c in range(n_chunks):
        if masked and c < WINDOW_CHUNKS:
            variant = jnp.where(t == 0, WINDOW_CHUNKS - c, 0)
        else:
            variant = 0
        for j in range(n_pairs):
            qp = q[c * CHUNK:(c + 1) * CHUNK, j * pair_w:(j + 1) * pair_w]
            s_buf[c, j] = (lax.dot_general(qp, window(kz, j, c), nt_dims, preferred_element_type=F32)
                           + bias_ref[variant, j])

    for c in range(n_chunks):
        for j in range(n_pairs):
            s0 = s_buf[c, j, :, 0:pair_w]
            s1 = s_buf[c, j, :, pair_w:2 * pair_w]
            s2 = s_buf[c, j, :, 2 * pair_w:]
            sink_a = sink_ref[2 * j] * LOG2E
            sink_b = sink_ref[2 * j + 1] * LOG2E
            m_a = jnp.maximum(jnp.max(jnp.maximum(s0, jnp.where(first_head, s1, NEG_INF)), -1, keepdims=True), sink_a)
            m_b = jnp.maximum(jnp.max(jnp.maximum(jnp.where(first_head, NEG_INF, s1), s2), -1, keepdims=True), sink_b)
            p_buf[c, j, :, 0:pair_w] = jnp.exp2(s0 - m_a).astype(BF16)
            p_buf[c, j, :, pair_w:2 * pair_w] = jnp.exp2(s1 - jnp.where(first_head, m_a, m_b)).astype(BF16)
            p_buf[c, j, :, 2 * pair_w:] = jnp.exp2(s2 - m_b).astype(BF16)
            e_buf[c, j] = jnp.where(first_head, jnp.exp2(sink_a - m_a), jnp.exp2(sink_b - m_b))

    span = tm + CONV_PAD - SUBLANES
    for s in range(1, SUBLANES):
        ush[s, 0:span, :] = ucat[s:s + span, :]
    off = CONV_PAD - CONV_HIST
    rb = min(CONV_ROWS, tm)
    groups = rb // SUBLANES

    def conv_rows(lanes, r, carry):
        r0 = pl.multiple_of(r * rb, rb)
        partial = []
        for shift in range(SUBLANES):
            taps = [k for k in range(CONV_K) if (off + k) % SUBLANES == shift]
            top = max(off + k - shift for k in taps)
            slab = ush[shift, pl.ds(r0, rb + top), lanes].reshape(groups + top // SUBLANES, SUBLANES, CONV_LANES)
            acc = None
            for k in taps:
                g0 = (off + k - shift) // SUBLANES
                term = slab[g0:g0 + groups] * wdw_ref[k, :, lanes]
                acc = term if acc is None else acc + term
            partial.append(acc)
        while len(partial) > 1:
            partial = [a + b for a, b in zip(partial[0::2], partial[1::2])]
        yconv[pl.ds(r0, rb), lanes] = (partial[0] + bdw_ref[:, lanes]).reshape(rb, CONV_LANES)
        return carry

    for part in range(CONV_CH // CONV_LANES):
        lanes = slice(part * CONV_LANES, (part + 1) * CONV_LANES)
        lax.fori_loop(0, tm // rb, functools.partial(conv_rows, lanes), 0)
    y = _layer_norm_rows(yconv[...], clg_ref[...], clb_ref[...])
    mix_ref[0, :, ATTN_WIDTH:] = (y * jax.nn.sigmoid(y)).astype(BF16)

    for c in range(n_chunks):
        for j in range(n_pairs):
            res = jnp.dot(p_buf[c, j], window(vz, j, c), preferred_element_type=F32)
            out = res[:, 0:pair_w] * (1.0 / (res[:, pair_w:] + e_buf[c, j]))
            mix_ref[0, c * CHUNK:(c + 1) * CHUNK, j * pair_w:(j + 1) * pair_w] = out.astype(BF16)

    @pl.when(t < nt - 1)
    def _():
        if tm >= WINDOW:
            for i in range(2 * N_KV_HEADS):
                kz[i, 0:WINDOW, :] = kz[i, tm:tm + WINDOW, :]
                vz[i, 0:WINDOW, 0:KV_WIDTH] = vz[i, tm:tm + WINDOW, 0:KV_WIDTH]
            ucat[0:CONV_PAD, :] = ucat[tm:tm + CONV_PAD, :]


def _mixer(x, mod, w_in, bias, sink, w_dw, b_dw, cln_g, cln_b, k0, v0, u0, *, tm, masked):
    bsz, seq, d = x.shape
    assert seq % tm == 0 and tm % CHUNK == 0 and (tm >= WINDOW or seq == tm)
    nt = seq // tm
    kern = functools.partial(_mixer_kernel, tm=tm, masked=masked)
    const2 = lambda b, t: (0, 0)
    const3 = lambda b, t: (0, 0, 0)
    per_b = lambda b, t: (b, 0, 0)
    return pl.pallas_call(
        kern,
        grid=(bsz, nt),
        in_specs=[pl.BlockSpec((1, tm, d), lambda b, t: (b, t, 0)),
                  pl.BlockSpec((1, 6, d), per_b),
                  pl.BlockSpec((d, IN_WIDTH), const2),
                  pl.BlockSpec((WINDOW_CHUNKS + 1, N_HEADS // 2, CHUNK, 2 * KEYS), lambda b, t: (0, 0, 0, 0)),
                  pl.BlockSpec(memory_space=pltpu.SMEM),
                  pl.BlockSpec((CONV_K, SUBLANES, CONV_CH), const3),
                  pl.BlockSpec((1, CONV_CH), const2),
                  pl.BlockSpec((1, CONV_CH), const2),
                  pl.BlockSpec((1, CONV_CH), const2),
                  pl.BlockSpec((1, WINDOW, KV_WIDTH), per_b),
                  pl.BlockSpec((1, WINDOW, KV_WIDTH), per_b),
                  pl.BlockSpec((1, CONV_PAD, CONV_CH), per_b)],
        out_specs=[pl.BlockSpec((1, tm, MIX_WIDTH), lambda b, t: (b, t, 0)),
                   pl.BlockSpec((1, WINDOW, KV_WIDTH), per_b),
                   pl.BlockSpec((1, WINDOW, KV_WIDTH), per_b),
                   pl.BlockSpec((1, CONV_HIST, CONV_CH), per_b)],
        out_shape=[jax.ShapeDtypeStruct((bsz, seq, MIX_WIDTH), BF16),
                   jax.ShapeDtypeStruct((bsz, WINDOW, KV_WIDTH), F32),
                   jax.ShapeDtypeStruct((bsz, WINDOW, KV_WIDTH), F32),
                   jax.ShapeDtypeStruct((bsz, CONV_HIST, CONV_CH), F32)],
        scratch_shapes=[pltpu.VMEM((2 * N_KV_HEADS, WINDOW + tm, KV_WIDTH), BF16),
                        pltpu.VMEM((2 * N_KV_HEADS, WINDOW + tm, 2 * KV_WIDTH), BF16),
                        pltpu.VMEM((SUBLANES, CONV_PAD + tm, CONV_CH), F32),
                        pltpu.VMEM((tm, CONV_CH), F32),
                        pltpu.VMEM((tm // CHUNK, N_HEADS // 2, CHUNK, 2 * KEYS), F32),
                        pltpu.VMEM((tm // CHUNK, N_HEADS // 2, CHUNK, 2 * KEYS), BF16),
                        pltpu.VMEM((tm // CHUNK, N_HEADS // 2, CHUNK, 2 * HEAD_DIM), F32)],
        compiler_params=_cparams("parallel", "arbitrary"),
        name="mixer",
    )(x, mod, w_in, bias, sink, w_dw, b_dw, cln_g, cln_b, k0, v0, u0)


def _post_kernel(x_ref, mix_ref, mod_ref, wout_ref, g_ref, b_ref, wr_split_ref, wr_hi_ref, br_ref,
                 x1_ref, h2e_ref, *, bb, tt):
    rows = bb * tt
    mix = jnp.dot(mix_ref[...].reshape(rows, MIX_WIDTH), wout_ref[...], preferred_element_type=F32)
    g1 = mod_ref[:, 2:3, :]
    sh2 = mod_ref[:, 3:4, :]
    sc2 = mod_ref[:, 4:5, :]
    r = ALPHA * x_ref[...] + (1.0 + g1) * mix.reshape(bb, tt, D_MODEL)
    x1 = _layer_norm_rows(r, g_ref[...], b_ref[...])
    x1_ref[...] = x1
    h2 = (x1 * (1.0 + sc2) + sh2).reshape(rows, D_MODEL)
    h2_hi = h2.astype(BF16)
    h2e_ref[:, :, 0:D_MODEL] = h2_hi.reshape(bb, tt, D_MODEL)

    h2_lo = (h2 - h2_hi.astype(F32)).astype(BF16)
    hi_terms = jnp.dot(h2_hi, wr_split_ref[...], preferred_element_type=F32)
    logits = (hi_terms[:, 0:ROUTE_LANES] + hi_terms[:, ROUTE_LANES:]
              + jnp.dot(h2_lo, wr_hi_ref[...], preferred_element_type=F32) + br_ref[...])
    lane = lax.broadcasted_iota(jnp.int32, (rows, ROUTE_LANES), 1).astype(F32)
    far = float(ROUTE_LANES)
    is_group = (lane >= N_EXPERTS) & (lane < N_EXPERTS + N_GROUPS)
    gl = jnp.where(is_group, logits, NEG_INF)
    gmax = jnp.max(gl, -1, keepdims=True)
    gidx = jnp.min(jnp.where(is_group & (gl == gmax), lane, far), -1, keepdims=True) - N_EXPERTS
    pg = 1.0 / jnp.sum(jnp.where(is_group, jnp.exp(gl - gmax), 0.0), -1, keepdims=True)
    in_group = (lane >= gidx * EXPERTS_PER_GROUP) & (lane < (gidx + 1) * EXPERTS_PER_GROUP)
    el = jnp.where(in_group, logits, NEG_INF)
    v1 = jnp.max(el, -1, keepdims=True)
    i1 = jnp.min(jnp.where(in_group & (el == v1), lane, far), -1, keepdims=True)
    rest = in_group & (lane != i1)
    el2 = jnp.where(rest, logits, NEG_INF)
    v2 = jnp.max(el2, -1, keepdims=True)
    i2 = jnp.min(jnp.where(rest & (el2 == v2), lane, far), -1, keepdims=True)
    e2 = jnp.exp(v2 - v1)
    w1 = pg / (1.0 + e2)
    w2 = pg * e2 / (1.0 + e2)
    comb = jnp.where(lane == i1, w1, jnp.where(lane == i2, w2, 0.0))

    c_hi = comb.astype(BF16).astype(F32)
    c_lo = (comb - c_hi).astype(BF16).astype(F32)
    rec = c_hi + pltpu.roll(c_lo, COMB_LO_LANE, 1) + jnp.where(lane == gidx + GROUP_LANE, 1.0, 0.0)
    h2e_ref[:, :, D_MODEL:] = rec.astype(BF16).reshape(bb, tt, ROUTE_LANES)


def _moe_rows(rows):
    return rows + N_GROUPS * MOE_ALIGN + MOE_BLK


def _moe_kernel(x1_ref, h2e_ref, mod_ref, tri_ref, wg_ref, wu_ref, wd_ref, g_ref, b_ref, y_ref,
                xs_ref, cw_ref, ys_ref, *, bb, tt):
    rows = bb * tt
    srows = _moe_rows(rows)
    ext = h2e_ref[...].reshape(rows, D_MODEL + ROUTE_LANES)
    rec = ext[:, D_MODEL:]
    cum = jnp.dot(tri_ref[...], rec, preferred_element_type=F32)
    lane1 = lax.broadcasted_iota(jnp.int32, (1, ROUTE_LANES), 1)
    cnt = cum[rows - 1:rows, :]
    off = jnp.int32(0)
    offs, counts = [], []
    off_vec = jnp.zeros((1, ROUTE_LANES), F32)
    for g in range(N_GROUPS):
        n_g = jnp.sum(jnp.where(lane1 == GROUP_LANE + g, cnt, 0.0)).astype(jnp.int32)
        offs.append(off)
        counts.append(n_g)
        off_vec = jnp.where(lane1 == GROUP_LANE + g, off.astype(F32), off_vec)
        off = off + ((n_g + (MOE_ALIGN - 1)) // MOE_ALIGN) * MOE_ALIGN

    lane = lax.broadcasted_iota(jnp.int32, (rows, ROUTE_LANES), 1)
    mine = (lane >= GROUP_LANE) & (lane < GROUP_LANE + N_GROUPS) & (rec.astype(F32) > 0.0)
    pos = jnp.sum(jnp.where(mine, cum - 1.0 + off_vec, 0.0), -1, keepdims=True)
    scatter = jnp.where(pos == lax.broadcasted_iota(jnp.int32, (rows, srows), 1).astype(F32), 1.0, 0.0).astype(BF16)
    pos_row = jnp.transpose(jnp.broadcast_to(pos, (rows, ROUTE_LANES)))[0:1, :]
    gather = jnp.where(lax.broadcasted_iota(jnp.int32, (srows, rows), 0).astype(F32) == pos_row, 1.0, 0.0).astype(BF16)
    xs_ref[...] = jnp.dot(gather, ext[:, 0:D_MODEL], preferred_element_type=F32).astype(BF16)
    rs = jnp.dot(gather, rec, preferred_element_type=F32)
    cw_ref[...] = rs + pltpu.roll(rs, ROUTE_LANES - COMB_LO_LANE, 1)

    def experts(g, r0):
        xb = xs_ref[pl.ds(r0, MOE_BLK), :]
        cwb = cw_ref[pl.ds(r0, MOE_BLK), :]
        parts = []
        for j in range(EXPERTS_PER_GROUP):
            e = g * EXPERTS_PER_GROUP + j
            hg = jnp.dot(xb, wg_ref[e], preferred_element_type=F32)
            hu = jnp.dot(xb, wu_ref[e], preferred_element_type=F32)
            parts.append((hg * jax.nn.sigmoid(hg) * hu * cwb[:, e:e + 1]).astype(BF16))
        act = jnp.concatenate(parts, axis=1)
        return jnp.dot(act, wd_ref[g], preferred_element_type=F32)

    for g in range(N_GROUPS):
        r0 = pl.multiple_of(offs[g], MOE_ALIGN)
        ys_ref[pl.ds(r0, MOE_BLK), :] = experts(g, r0).astype(BF16)

    def extra_block(g, i, carry):
        r0 = pl.multiple_of(offs[g] + i * MOE_BLK, MOE_ALIGN)
        row = r0 + lax.broadcasted_iota(jnp.int32, (MOE_BLK, 1), 0)
        old = ys_ref[pl.ds(r0, MOE_BLK), :].astype(F32)
        ys_ref[pl.ds(r0, MOE_BLK), :] = jnp.where(row < offs[g] + counts[g], experts(g, r0), old).astype(BF16)
        return carry

    for g in range(N_GROUPS):
        lax.fori_loop(1, (counts[g] + (MOE_BLK - 1)) // MOE_BLK, functools.partial(extra_block, g), 0)

    ff = jnp.dot(scatter, ys_ref[...], preferred_element_type=F32)
    g2 = mod_ref[:, 5:6, :]
    r = ALPHA * x1_ref[...] + (1.0 + g2) * ff.reshape(bb, tt, D_MODEL)
    y_ref[...] = _layer_norm_rows(r, g_ref[...], b_ref[...])


def _ffn_kernel(x_ref, mix_ref, mod_cur_ref, mod_prev_ref, wout_ref, g1_ref, b1_ref, wr_split_ref, wr_hi_ref,
                br_ref, tri_ref, wg_ref, wu_ref, wd_ref, g2_ref, b2_ref, y_ref,
                x1_buf, h2e_buf, xs_ref, cw_ref, ys_ref, *, bb, tt):
    s = pl.program_id(0)
    cur = s % 2
    prev = 1 - cur

    @pl.when(s == 0)
    def _():
        ys_ref[...] = jnp.zeros_like(ys_ref)
        x1_buf[1] = jnp.zeros(x1_buf.shape[1:], F32)
        h2e_buf[1] = jnp.zeros(h2e_buf.shape[1:], BF16)

    _post_kernel(x_ref, mix_ref, mod_cur_ref, wout_ref, g1_ref, b1_ref, wr_split_ref, wr_hi_ref, br_ref,
                 x1_buf.at[cur], h2e_buf.at[cur], bb=bb, tt=tt)
    _moe_kernel(x1_buf.at[prev], h2e_buf.at[prev], mod_prev_ref, tri_ref, wg_ref, wu_ref, wd_ref, g2_ref, b2_ref,
                y_ref, xs_ref, cw_ref, ys_ref, bb=bb, tt=tt)


def _ffn(x, mixin, mod, p, *, bb, tt):
    bsz, seq, d = x.shape
    assert bsz % bb == 0 and seq % tt == 0
    rows = bb * tt
    srows = _moe_rows(rows)
    n_j = seq // tt
    n_tiles = (bsz // bb) * n_j
    tri = jnp.tril(jnp.ones((rows, rows), BF16))
    kern = functools.partial(_ffn_kernel, bb=bb, tt=tt)
    cur = lambda s: jnp.minimum(s, n_tiles - 1)
    prev = lambda s: jnp.maximum(s - 1, 0)
    const2 = lambda s: (0, 0)
    const3 = lambda s: (0, 0, 0)
    resident = pl.Buffered(1)
    return pl.pallas_call(
        kern,
        grid=(n_tiles + 1,),
        in_specs=[pl.BlockSpec((bb, tt, d), lambda s: (cur(s) // n_j, cur(s) % n_j, 0)),
                  pl.BlockSpec((bb, tt, MIX_WIDTH), lambda s: (cur(s) // n_j, cur(s) % n_j, 0)),
                  pl.BlockSpec((bb, 6, d), lambda s: (cur(s) // n_j, 0, 0)),
                  pl.BlockSpec((bb, 6, d), lambda s: (prev(s) // n_j, 0, 0)),
                  pl.BlockSpec((MIX_WIDTH, d), const2, pipeline_mode=resident),
                  pl.BlockSpec((1, d), const2),
                  pl.BlockSpec((1, d), const2),
                  pl.BlockSpec((d, 2 * ROUTE_LANES), const2, pipeline_mode=resident),
                  pl.BlockSpec((d, ROUTE_LANES), const2, pipeline_mode=resident),
                  pl.BlockSpec((1, ROUTE_LANES), const2),
                  pl.BlockSpec((rows, rows), const2, pipeline_mode=resident),
                  pl.BlockSpec(p["w_gate"].shape, const3, pipeline_mode=resident),
                  pl.BlockSpec(p["w_up"].shape, const3, pipeline_mode=resident),
                  pl.BlockSpec(p["w_down"].shape, const3, pipeline_mode=resident),
                  pl.BlockSpec((1, d), const2),
                  pl.BlockSpec((1, d), const2)],
        out_specs=pl.BlockSpec((bb, tt, d), lambda s: (prev(s) // n_j, prev(s) % n_j, 0)),
        out_shape=jax.ShapeDtypeStruct((bsz, seq, d), F32),
        scratch_shapes=[pltpu.VMEM((2, bb, tt, d), F32),
                        pltpu.VMEM((2, bb, tt, d + ROUTE_LANES), BF16),
                        pltpu.VMEM((srows, d), BF16),
                        pltpu.VMEM((srows, ROUTE_LANES), F32),
                        pltpu.VMEM((srows, d), BF16)],
        compiler_params=_cparams("arbitrary"),
        name="ffn",
    )(x, mixin, mod, mod, p["w_out"], p["ln1_g"], p["ln1_b"], p["w_route_split"], p["w_route_hi"], p["b_route"],
      tri, p["w_gate"], p["w_up"], p["w_down"], p["ln2_g"], p["ln2_b"])


def _stream(x, mod, k0, v0, u0, p, *, mixer_tm, masked, bb, tt):
    mixin, nk, nv, nu = _mixer(x, mod, p["w_in"], p["bias"], p["sink"], p["w_dw"], p["b_dw"],
                               p["cln_g"], p["cln_b"], k0, v0, u0, tm=mixer_tm, masked=masked)
    y = _ffn(x, mixin, mod, p, bb=bb, tt=tt)
    bsz = x.shape[0]
    cache_shape = (1, bsz, WINDOW, N_KV_HEADS, HEAD_DIM)
    return y, nk.reshape(cache_shape), nv.reshape(cache_shape), nu[None]


def kernel(x_prompt, x_sample, cache_attn_k, cache_attn_v, state_conv, c_prompt, c_sample, rel_bias, w_ada, b_ada, w_in, attn_sinks, w_dw, b_dw, conv_ln_g, conv_ln_b, w_out, ln1_g, ln1_b, w_group, b_group, w_erouter, b_erouter, w_gate, w_up, w_down, ln2_g, ln2_b):
    bp = x_prompt.shape[0]
    bs = x_sample.shape[0]
    mod = _modulation(jnp.concatenate([c_prompt, c_sample], 0), w_ada[0], b_ada[0])
    mod = mod.reshape(bp + bs, 6, D_MODEL)

    w_route = jnp.concatenate([w_erouter[0].reshape(D_MODEL, N_EXPERTS), w_group[0]], 1)
    w_route = jnp.pad(w_route, ((0, 0), (0, ROUTE_LANES - N_EXPERTS - N_GROUPS)))
    b_route = jnp.concatenate([b_erouter[0].reshape(N_EXPERTS), b_group[0]])
    b_route = jnp.pad(b_route, (0, ROUTE_LANES - N_EXPERTS - N_GROUPS)).reshape(1, ROUTE_LANES)
    w_route_hi = w_route.astype(BF16)
    w_route_lo = (w_route - w_route_hi.astype(F32)).astype(BF16)

    p = {
        "w_in": w_in[0].astype(BF16),
        "bias": _relative_bias(rel_bias),
        "sink": attn_sinks[0],
        "w_dw": jnp.broadcast_to(w_dw[0][:, None, :], (CONV_K, SUBLANES, CONV_CH)),
        "b_dw": b_dw[0].reshape(1, CONV_CH),
        "cln_g": conv_ln_g[0].reshape(1, CONV_CH), "cln_b": conv_ln_b[0].reshape(1, CONV_CH),
        "w_out": w_out[0].astype(BF16),
        "ln1_g": ln1_g[0].reshape(1, D_MODEL), "ln1_b": ln1_b[0].reshape(1, D_MODEL),
        "w_route_split": jnp.concatenate([w_route_hi, w_route_lo], 1), "w_route_hi": w_route_hi,
        "b_route": b_route,
        "w_gate": w_gate[0].astype(BF16), "w_up": w_up[0].astype(BF16),
        "w_down": w_down[0].astype(BF16).reshape(N_GROUPS, EXPERTS_PER_GROUP * D_EXPERT, D_MODEL),
        "ln2_g": ln2_g[0].reshape(1, D_MODEL), "ln2_b": ln2_b[0].reshape(1, D_MODEL),
    }

    zk = jnp.zeros((bp, WINDOW, KV_WIDTH), F32)
    zu = jnp.zeros((bp, CONV_PAD, CONV_CH), F32)
    yp, pk, pv, pc = _stream(x_prompt, mod[:bp], zk, zk, zu, p, mixer_tm=512, masked=True, bb=1, tt=512)

    k0 = cache_attn_k[0].reshape(bs, WINDOW, KV_WIDTH)
    v0 = cache_attn_v[0].reshape(bs, WINDOW, KV_WIDTH)
    u0 = jnp.pad(state_conv[0], ((0, 0), (CONV_PAD - CONV_HIST, 0), (0, 0)))
    ts = x_sample.shape[1]
    ys, sk, sv, sc = _stream(x_sample, mod[bp:], k0, v0, u0, p, mixer_tm=ts, masked=False, bb=8, tt=ts)
    return yp, ys, pk, pv, pc, sk, sv, sc
```

```python
import functools
import math

import jax
import jax.numpy as jnp
from jax import lax
from jax.experimental import pallas as pl
from jax.experimental.pallas import tpu as pltpu

D_MODEL = 1024
CHUNK = 64
N_HEADS = 8
N_KV_HEADS = 2
HEAD_DIM = 64
Q_GROUP = N_HEADS // N_KV_HEADS
ATTN_WIDTH = N_HEADS * HEAD_DIM
KV_WIDTH = N_KV_HEADS * HEAD_DIM
WINDOW = 128
WINDOW_CHUNKS = WINDOW // CHUNK
KEYS = WINDOW + CHUNK
CONV_CH = D_MODEL // 2
CONV_K = 31
CONV_HIST = CONV_K - 1
CONV_PAD = 32
SUBLANES = 8
CONV_ROWS = 256
CONV_LANES = 128
MIX_WIDTH = ATTN_WIDTH + CONV_CH
IN_WIDTH = ATTN_WIDTH + 2 * KV_WIDTH + 2 * CONV_CH
N_BUCKETS = 32
MAX_DISTANCE = 128
N_GROUPS = 4
EXPERTS_PER_GROUP = 4
N_EXPERTS = N_GROUPS * EXPERTS_PER_GROUP
D_EXPERT = D_MODEL // 4
DEPTH = 1
ALPHA = (2 * DEPTH) ** 0.25
LN_EPS = 1e-5
NEG_INF = -1e30
LOG2E = math.log2(math.e)
ROUTE_LANES = 128
COMB_LO_LANE = 16
GROUP_LANE = 32
MOE_BLK = 160
MOE_ALIGN = 16
VMEM_LIMIT = 56 * 1024 * 1024

BF16 = jnp.bfloat16
F32 = jnp.float32


def _cparams(*sem):
    return pltpu.CompilerParams(dimension_semantics=sem, vmem_limit_bytes=VMEM_LIMIT)


def _mod_kernel(c_ref, w_ref, b_ref, o_ref):
    c = c_ref[...]
    s = c * jax.nn.sigmoid(c)
    o_ref[...] = jnp.dot(s, w_ref[...], preferred_element_type=F32,
                         precision=lax.Precision.HIGHEST) + b_ref[...]


def _modulation(c, w_ada, b_ada):
    n, d = c.shape
    width = w_ada.shape[1]
    tn = 1536
    return pl.pallas_call(
        _mod_kernel,
        grid=(width // tn,),
        in_specs=[pl.BlockSpec((n, d), lambda j: (0, 0)),
                  pl.BlockSpec((d, tn), lambda j: (0, j)),
                  pl.BlockSpec((1, tn), lambda j: (0, j))],
        out_specs=pl.BlockSpec((n, tn), lambda j: (0, j)),
        out_shape=jax.ShapeDtypeStruct((n, width), F32),
        compiler_params=_cparams("arbitrary"),
        name="modulation",
    )(c, w_ada, b_ada.reshape(1, width))


def _t5_bucket(rel):
    nb = N_BUCKETS // 2
    max_exact = nb // 2
    ret = jnp.where(rel > 0, nb, 0)
    n = jnp.abs(rel)
    large = max_exact + (jnp.log(jnp.maximum(n, 1).astype(jnp.float32) / max_exact)
                         / math.log(MAX_DISTANCE / max_exact) * (nb - max_exact)).astype(jnp.int32)
    large = jnp.minimum(large, nb - 1)
    return ret + jnp.where(n < max_exact, n, large)


def _bias_kernel(table_ref, bucket_ref, o_ref):
    bucket = bucket_ref[...]
    col = lax.broadcasted_iota(jnp.int32, bucket.shape, 1)
    for h in range(N_HEADS):
        acc = jnp.zeros(bucket.shape, F32)
        for b in range(N_BUCKETS):
            acc = jnp.where(bucket == b, table_ref[b, h], acc)
        for v in range(WINDOW_CHUNKS + 1):
            o_ref[v, h] = jnp.where(col >= v * CHUNK, acc * LOG2E, NEG_INF)


def _relative_bias(table):
    rel = jnp.arange(KEYS)[None, :] - WINDOW - jnp.arange(CHUNK)[:, None]
    bucket = _t5_bucket(rel).astype(jnp.int32)
    nv = WINDOW_CHUNKS + 1
    bias = pl.pallas_call(
        _bias_kernel,
        in_specs=[pl.BlockSpec(memory_space=pltpu.SMEM),
                  pl.BlockSpec((CHUNK, KEYS), lambda: (0, 0))],
        out_specs=pl.BlockSpec((nv, N_HEADS, CHUNK, KEYS), lambda: (0, 0, 0, 0)),
        out_shape=jax.ShapeDtypeStruct((nv, N_HEADS, CHUNK, KEYS), F32),
        name="relative_bias",
    )(table, bucket)
    bias = bias.reshape(nv, N_HEADS // 2, 2, CHUNK, KEYS)
    return jnp.transpose(bias, (0, 1, 3, 2, 4)).reshape(nv, N_HEADS // 2, CHUNK, 2 * KEYS)


def _layer_norm_rows(x, g, b):
    mu = jnp.mean(x, -1, keepdims=True)
    xc = x - mu
    var = jnp.mean(xc * xc, -1, keepdims=True)
    return xc * lax.rsqrt(var + LN_EPS) * g + b


def _mixer_kernel(x_ref, mod_ref, win_ref, bias_ref, sink_ref, wdw_ref, bdw_ref, clg_ref, clb_ref,
                  k0_ref, v0_ref, u0_ref,
                  mix_ref, nk_ref, nv_ref, nu_ref,
                  kz, vz, ush, yconv, s_buf, p_buf, e_buf, *, tm, masked):
    t = pl.program_id(1)
    nt = pl.num_programs(1)
    n_chunks = tm // CHUNK
    ucat = ush.at[0]

    left = lax.broadcasted_iota(jnp.int32, (1, KV_WIDTH), 1) < HEAD_DIM

    def store_kv(row0, k, v):
        n = k.shape[0]
        for dst, val in ((kz, k), (vz, v)):
            swapped = pltpu.roll(val, HEAD_DIM, 1)
            dst[0, row0:row0 + n, 0:KV_WIDTH] = jnp.where(left, val, 0.0).astype(BF16)
            dst[1, row0:row0 + n, 0:KV_WIDTH] = jnp.where(left, 0.0, swapped).astype(BF16)
            dst[2, row0:row0 + n, 0:KV_WIDTH] = jnp.where(left, swapped, 0.0).astype(BF16)
            dst[3, row0:row0 + n, 0:KV_WIDTH] = jnp.where(left, 0.0, val).astype(BF16)

    @pl.when(t == 0)
    def _():
        ones_left = jnp.broadcast_to(jnp.where(left, 1.0, 0.0).astype(BF16), (WINDOW + tm, KV_WIDTH))
        ones_right = jnp.broadcast_to(jnp.where(left, 0.0, 1.0).astype(BF16), (WINDOW + tm, KV_WIDTH))
        for i in range(2 * N_KV_HEADS):
            vz[i, :, KV_WIDTH:] = ones_left if i % 2 == 0 else ones_right
        store_kv(0, k0_ref[0], v0_ref[0])
        ucat[0:CONV_PAD, :] = u0_ref[0]

    sh1 = mod_ref[0, 0:1, :]
    sc1 = mod_ref[0, 1:2, :]
    h = (x_ref[0] * (1.0 + sc1) + sh1).astype(BF16)
    o1 = ATTN_WIDTH
    o3 = o1 + 2 * KV_WIDTH
    q = (jnp.dot(h, win_ref[:, 0:o1], preferred_element_type=F32) * (HEAD_DIM ** -0.5 * LOG2E)).astype(BF16)
    kv = jnp.dot(h, win_ref[:, o1:o3], preferred_element_type=F32)
    ag = jnp.dot(h, win_ref[:, o3:IN_WIDTH], preferred_element_type=F32)
    u = ag[:, 0:CONV_CH] * jax.nn.sigmoid(ag[:, CONV_CH:])
    store_kv(WINDOW, kv[:, 0:KV_WIDTH], kv[:, KV_WIDTH:])
    ucat[CONV_PAD:CONV_PAD + tm, :] = u

    if tm >= WINDOW:
        nk_ref[0] = kv[tm - WINDOW:, 0:KV_WIDTH]
        nv_ref[0] = kv[tm - WINDOW:, KV_WIDTH:]
    else:
        nk_ref[0, 0:WINDOW - tm, :] = k0_ref[0, tm:WINDOW, :]
        nv_ref[0, 0:WINDOW - tm, :] = v0_ref[0, tm:WINDOW, :]
        nk_ref[0, WINDOW - tm:, :] = kv[:, 0:KV_WIDTH]
        nv_ref[0, WINDOW - tm:, :] = kv[:, KV_WIDTH:]
    nu_ref[0] = ucat[CONV_PAD + tm - CONV_HIST:CONV_PAD + tm, :]

    n_pairs = N_HEADS // 2
    pair_w = 2 * HEAD_DIM
    first_head = lax.broadcasted_iota(jnp.int32, (CHUNK, pair_w), 1) < HEAD_DIM
    nt_dims = (((1,), (1,)), ((), ()))

    def window(ref, j, c):
        kvh = (2 * j) // Q_GROUP
        rows = slice(c * CHUNK, c * CHUNK + KEYS)
        return jnp.concatenate([ref[2 * kvh, rows, :], ref[2 * kvh + 1, rows, :]], axis=0)

    for c in range(n_chunks):
        if masked and c < WINDOW_CHUNKS:
            variant = jnp.where(t == 0, WINDOW_CHUNKS - c, 0)
        else:
            variant = 0
        for j in range(n_pairs):
            qp = q[c * CHUNK:(c + 1) * CHUNK, j * pair_w:(j + 1) * pair_w]
            s_buf[c, j] = (lax.dot_general(qp, window(kz, j, c), nt_dims, preferred_element_type=F32)
                           + bias_ref[variant, j])

    for c in range(n_chunks):
        for j in range(n_pairs):
            s0 = s_buf[c, j, :, 0:pair_w]
            s1 = s_buf[c, j, :, pair_w:2 * pair_w]
            s2 = s_buf[c, j, :, 2 * pair_w:]
            sink_a = sink_ref[2 * j] * LOG2E
            sink_b = sink_ref[2 * j + 1] * LOG2E
            m_a = jnp.maximum(jnp.max(jnp.maximum(s0, jnp.where(first_head, s1, NEG_INF)), -1, keepdims=True), sink_a)
            m_b = jnp.maximum(jnp.max(jnp.maximum(jnp.where(first_head, NEG_INF, s1), s2), -1, keepdims=True), sink_b)
            p_buf[c, j, :, 0:pair_w] = jnp.exp2(s0 - m_a).astype(BF16)
            p_buf[c, j, :, pair_w:2 * pair_w] = jnp.exp2(s1 - jnp.where(first_head, m_a, m_b)).astype(BF16)
            p_buf[c, j, :, 2 * pair_w:] = jnp.exp2(s2 - m_b).astype(BF16)
            e_buf[c, j] = jnp.where(first_head, jnp.exp2(sink_a - m_a), jnp.exp2(sink_b - m_b))

    span = tm + CONV_PAD - SUBLANES
    for s in range(1, SUBLANES):
        ush[s, 0:span, :] = ucat[s:s + span, :]
    off = CONV_PAD - CONV_HIST
    rb = min(CONV_ROWS, tm)
    groups = rb // SUBLANES

    def conv_rows(lanes, r, carry):
        r0 = pl.multiple_of(r * rb, rb)
        partial = []
        for shift in range(SUBLANES):
            taps = [k for k in range(CONV_K) if (off + k) % SUBLANES == shift]
            top = max(off + k - shift for k in taps)
            slab = ush[shift, pl.ds(r0, rb + top), lanes].reshape(groups + top // SUBLANES, SUBLANES, CONV_LANES)
            acc = None
            for k in taps:
                g0 = (off + k - shift) // SUBLANES
                term = slab[g0:g0 + groups] * wdw_ref[k, :, lanes]
                acc = term if acc is None else acc + term
            partial.append(acc)
        while len(partial) > 1:
            partial = [a + b for a, b in zip(partial[0::2], partial[1::2])]
        yconv[pl.ds(r0, rb), lanes] = (partial[0] + bdw_ref[:, lanes]).reshape(rb, CONV_LANES)
        return carry

    for part in range(CONV_CH // CONV_LANES):
        lanes = slice(part * CONV_LANES, (part + 1) * CONV_LANES)
        lax.fori_loop(0, tm // rb, functools.partial(conv_rows, lanes), 0)
    y = _layer_norm_rows(yconv[...], clg_ref[...], clb_ref[...])
    mix_ref[0, :, ATTN_WIDTH:] = (y * jax.nn.sigmoid(y)).astype(BF16)

    for c in range(n_chunks):
        for j in range(n_pairs):
            res = jnp.dot(p_buf[c, j], window(vz, j, c), preferred_element_type=F32)
            out = res[:, 0:pair_w] * (1.0 / (res[:, pair_w:] + e_buf[c, j]))
            mix_ref[0, c * CHUNK:(c + 1) * CHUNK, j * pair_w:(j + 1) * pair_w] = out.astype(BF16)

    @pl.when(t < nt - 1)
    def _():
        if tm >= WINDOW:
            for i in range(2 * N_KV_HEADS):
                kz[i, 0:WINDOW, :] = kz[i, tm:tm + WINDOW, :]
                vz[i, 0:WINDOW, 0:KV_WIDTH] = vz[i, tm:tm + WINDOW, 0:KV_WIDTH]
            ucat[0:CONV_PAD, :] = ucat[tm:tm + CONV_PAD, :]


def _mixer(x, mod, w_in, bias, sink, w_dw, b_dw, cln_g, cln_b, k0, v0, u0, *, tm, masked):
    bsz, seq, d = x.shape
    assert seq % tm == 0 and tm % CHUNK == 0 and (tm >= WINDOW or seq == tm)
    nt = seq // tm
    kern = functools.partial(_mixer_kernel, tm=tm, masked=masked)
    const2 = lambda b, t: (0, 0)
    const3 = lambda b, t: (0, 0, 0)
    per_b = lambda b, t: (b, 0, 0)
    return pl.pallas_call(
        kern,
        grid=(bsz, nt),
        in_specs=[pl.BlockSpec((1, tm, d), lambda b, t: (b, t, 0)),
                  pl.BlockSpec((1, 6, d), per_b),
                  pl.BlockSpec((d, IN_WIDTH), const2),
                  pl.BlockSpec((WINDOW_CHUNKS + 1, N_HEADS // 2, CHUNK, 2 * KEYS), lambda b, t: (0, 0, 0, 0)),
                  pl.BlockSpec(memory_space=pltpu.SMEM),
                  pl.BlockSpec((CONV_K, SUBLANES, CONV_CH), const3),
                  pl.BlockSpec((1, CONV_CH), const2),
                  pl.BlockSpec((1, CONV_CH), const2),
                  pl.BlockSpec((1, CONV_CH), const2),
                  pl.BlockSpec((1, WINDOW, KV_WIDTH), per_b),
                  pl.BlockSpec((1, WINDOW, KV_WIDTH), per_b),
                  pl.BlockSpec((1, CONV_PAD, CONV_CH), per_b)],
        out_specs=[pl.BlockSpec((1, tm, MIX_WIDTH), lambda b, t: (b, t, 0)),
                   pl.BlockSpec((1, WINDOW, KV_WIDTH), per_b),
                   pl.BlockSpec((1, WINDOW, KV_WIDTH), per_b),
                   pl.BlockSpec((1, CONV_HIST, CONV_CH), per_b)],
        out_shape=[jax.ShapeDtypeStruct((bsz, seq, MIX_WIDTH), BF16),
                   jax.ShapeDtypeStruct((bsz, WINDOW, KV_WIDTH), F32),
                   jax.ShapeDtypeStruct((bsz, WINDOW, KV_WIDTH), F32),
                   jax.ShapeDtypeStruct((bsz, CONV_HIST, CONV_CH), F32)],
        scratch_shapes=[pltpu.VMEM((2 * N_KV_HEADS, WINDOW + tm, KV_WIDTH), BF16),
                        pltpu.VMEM((2 * N_KV_HEADS, WINDOW + tm, 2 * KV_WIDTH), BF16),
                        pltpu.VMEM((SUBLANES, CONV_PAD + tm, CONV_CH), F32),
                        pltpu.VMEM((tm, CONV_CH), F32),
                        pltpu.VMEM((tm // CHUNK, N_HEADS // 2, CHUNK, 2 * KEYS), F32),
                        pltpu.VMEM((tm // CHUNK, N_HEADS // 2, CHUNK, 2 * KEYS), BF16),
                        pltpu.VMEM((tm // CHUNK, N_HEADS // 2, CHUNK, 2 * HEAD_DIM), F32)],
        compiler_params=_cparams("parallel", "arbitrary"),
        name="mixer",
    )(x, mod, w_in, bias, sink, w_dw, b_dw, cln_g, cln_b, k0, v0, u0)


def _post_kernel(x_ref, mix_ref, mod_ref, wout_ref, g_ref, b_ref, wr_split_ref, wr_hi_ref, br_ref,
                 x1_ref, h2e_ref, *, bb, tt):
    rows = bb * tt
    mix = jnp.dot(mix_ref[...].reshape(rows, MIX_WIDTH), wout_ref[...], preferred_element_type=F32)
    g1 = mod_ref[:, 2:3, :]
    sh2 = mod_ref[:, 3:4, :]
    sc2 = mod_ref[:, 4:5, :]
    r = ALPHA * x_ref[...] + (1.0 + g1) * mix.reshape(bb, tt, D_MODEL)
    x1 = _layer_norm_rows(r, g_ref[...], b_ref[...])
    x1_ref[...] = x1
    h2 = (x1 * (1.0 + sc2) + sh2).reshape(rows, D_MODEL)
    h2_hi = h2.astype(BF16)
    h2e_ref[:, :, 0:D_MODEL] = h2_hi.reshape(bb, tt, D_MODEL)

    h2_lo = (h2 - h2_hi.astype(F32)).astype(BF16)
    hi_terms = jnp.dot(h2_hi, wr_split_ref[...], preferred_element_type=F32)
    logits = (hi_terms[:, 0:ROUTE_LANES] + hi_terms[:, ROUTE_LANES:]
              + jnp.dot(h2_lo, wr_hi_ref[...], preferred_element_type=F32) + br_ref[...])
    lane = lax.broadcasted_iota(jnp.int32, (rows, ROUTE_LANES), 1).astype(F32)
    far = float(ROUTE_LANES)
    is_group = (lane >= N_EXPERTS) & (lane < N_EXPERTS + N_GROUPS)
    gl = jnp.where(is_group, logits, NEG_INF)
    gmax = jnp.max(gl, -1, keepdims=True)
    gidx = jnp.min(jnp.where(is_group & (gl == gmax), lane, far), -1, keepdims=True) - N_EXPERTS
    pg = 1.0 / jnp.sum(jnp.where(is_group, jnp.exp(gl - gmax), 0.0), -1, keepdims=True)
    in_group = (lane >= gidx * EXPERTS_PER_GROUP) & (lane < (gidx + 1) * EXPERTS_PER_GROUP)
    el = jnp.where(in_group, logits, NEG_INF)
    v1 = jnp.max(el, -1, keepdims=True)
    i1 = jnp.min(jnp.where(in_group & (el == v1), lane, far), -1, keepdims=True)
    rest = in_group & (lane != i1)
    el2 = jnp.where(rest, logits, NEG_INF)
    v2 = jnp.max(el2, -1, keepdims=True)
    i2 = jnp.min(jnp.where(rest & (el2 == v2), lane, far), -1, keepdims=True)
    e2 = jnp.exp(v2 - v1)
    w1 = pg / (1.0 + e2)
    w2 = pg * e2 / (1.0 + e2)
    comb = jnp.where(lane == i1, w1, jnp.where(lane == i2, w2, 0.0))

    c_hi = comb.astype(BF16).astype(F32)
    c_lo = (comb - c_hi).astype(BF16).astype(F32)
    rec = c_hi + pltpu.roll(c_lo, COMB_LO_LANE, 1) + jnp.where(lane == gidx + GROUP_LANE, 1.0, 0.0)
    h2e_ref[:, :, D_MODEL:] = rec.astype(BF16).reshape(bb, tt, ROUTE_LANES)


def _moe_rows(rows):
    return rows + N_GROUPS * MOE_ALIGN + MOE_BLK


def _moe_experts(h2e_ref, tri_ref, wg_ref, wu_ref, wd_ref, xs_ref, cw_ref, ys_ref, scatter_ref, other_stream,
                 *, bb, tt):
    rows = bb * tt
    srows = _moe_rows(rows)
    ext = h2e_ref[...].reshape(rows, D_MODEL + ROUTE_LANES)
    rec = ext[:, D_MODEL:]
    cum = jnp.dot(tri_ref[...], rec, preferred_element_type=F32)
    lane1 = lax.broadcasted_iota(jnp.int32, (1, ROUTE_LANES), 1)
    cnt = cum[rows - 1:rows, :]
    off = jnp.int32(0)
    offs, counts = [], []
    off_vec = jnp.zeros((1, ROUTE_LANES), F32)
    for g in range(N_GROUPS):
        n_g = jnp.sum(jnp.where(lane1 == GROUP_LANE + g, cnt, 0.0)).astype(jnp.int32)
        offs.append(off)
        counts.append(n_g)
        off_vec = jnp.where(lane1 == GROUP_LANE + g, off.astype(F32), off_vec)
        off = off + ((n_g + (MOE_ALIGN - 1)) // MOE_ALIGN) * MOE_ALIGN

    lane = lax.broadcasted_iota(jnp.int32, (rows, ROUTE_LANES), 1)
    mine = (lane >= GROUP_LANE) & (lane < GROUP_LANE + N_GROUPS) & (rec.astype(F32) > 0.0)
    pos = jnp.sum(jnp.where(mine, cum - 1.0 + off_vec, 0.0), -1, keepdims=True)
    scatter_ref[...] = jnp.where(pos == lax.broadcasted_iota(jnp.int32, (rows, srows), 1).astype(F32),
                                 1.0, 0.0).astype(BF16)
    pos_row = jnp.transpose(jnp.broadcast_to(pos, (rows, ROUTE_LANES)))[0:1, :]
    gather = jnp.where(lax.broadcasted_iota(jnp.int32, (srows, rows), 0).astype(F32) == pos_row, 1.0, 0.0).astype(BF16)
    xs_ref[...] = jnp.dot(gather, ext[:, 0:D_MODEL], preferred_element_type=F32).astype(BF16)
    rs = jnp.dot(gather, rec, preferred_element_type=F32)
    cw_ref[...] = rs + pltpu.roll(rs, ROUTE_LANES - COMB_LO_LANE, 1)

    def experts(g, r0):
        xb = xs_ref[pl.ds(r0, MOE_BLK), :]
        cwb = cw_ref[pl.ds(r0, MOE_BLK), :]
        parts = []
        for j in range(EXPERTS_PER_GROUP):
            e = g * EXPERTS_PER_GROUP + j
            hg = jnp.dot(xb, wg_ref[e], preferred_element_type=F32)
            hu = jnp.dot(xb, wu_ref[e], preferred_element_type=F32)
            parts.append((hg * jax.nn.sigmoid(hg) * hu * cwb[:, e:e + 1]).astype(BF16))
        act = jnp.concatenate(parts, axis=1)
        return jnp.dot(act, wd_ref[g], preferred_element_type=F32)

    for g in range(N_GROUPS):
        r0 = pl.multiple_of(offs[g], MOE_ALIGN)
        ys_ref[pl.ds(r0, MOE_BLK), :] = experts(g, r0).astype(BF16)
        if g == 1:
            other_stream()

    def extra_block(g, i, carry):
        r0 = pl.multiple_of(offs[g] + i * MOE_BLK, MOE_ALIGN)
        row = r0 + lax.broadcasted_iota(jnp.int32, (MOE_BLK, 1), 0)
        old = ys_ref[pl.ds(r0, MOE_BLK), :].astype(F32)
        ys_ref[pl.ds(r0, MOE_BLK), :] = jnp.where(row < offs[g] + counts[g], experts(g, r0), old).astype(BF16)
        return carry

    def extra_blocks():
        for g in range(N_GROUPS):
            lax.fori_loop(1, (counts[g] + (MOE_BLK - 1)) // MOE_BLK, functools.partial(extra_block, g), 0)

    return extra_blocks


def _moe_finish(x1_ref, mod_ref, g_ref, b_ref, ys_ref, scatter_ref, y_ref, *, bb, tt):
    ff = jnp.dot(scatter_ref[...], ys_ref[...], preferred_element_type=F32)
    g2 = mod_ref[:, 5:6, :]
    r = ALPHA * x1_ref[...] + (1.0 + g2) * ff.reshape(bb, tt, D_MODEL)
    y_ref[...] = _layer_norm_rows(r, g_ref[...], b_ref[...])


def _ffn_kernel(x_ref, mix_ref, mod_cur_ref, mod_prev_ref, wout_ref, g1_ref, b1_ref, wr_split_ref, wr_hi_ref,
                br_ref, tri_ref, wg_ref, wu_ref, wd_ref, g2_ref, b2_ref, y_ref,
                x1_buf, h2e_buf, xs_ref, cw_ref, ys_ref, scatter_ref, *, bb, tt):
    s = pl.program_id(0)
    cur = s % 2
    prev = 1 - cur

    @pl.when(s == 0)
    def _():
        ys_ref[...] = jnp.zeros_like(ys_ref)
        x1_buf[1] = jnp.zeros(x1_buf.shape[1:], F32)
        h2e_buf[...] = jnp.zeros(h2e_buf.shape, BF16)

    post = functools.partial(_post_kernel, x_ref, mix_ref, mod_cur_ref, wout_ref, g1_ref, b1_ref, wr_split_ref,
                             wr_hi_ref, br_ref, x1_buf.at[cur], h2e_buf, bb=bb, tt=tt)
    extra_blocks = _moe_experts(h2e_buf, tri_ref, wg_ref, wu_ref, wd_ref, xs_ref, cw_ref, ys_ref,
                                scatter_ref, post, bb=bb, tt=tt)
    extra_blocks()
    _moe_finish(x1_buf.at[prev], mod_prev_ref, g2_ref, b2_ref, ys_ref, scatter_ref, y_ref, bb=bb, tt=tt)


def _ffn(x, mixin, mod, p, *, bb, tt):
    bsz, seq, d = x.shape
    assert bsz % bb == 0 and seq % tt == 0
    rows = bb * tt
    srows = _moe_rows(rows)
    n_j = seq // tt
    n_tiles = (bsz // bb) * n_j
    tri = jnp.tril(jnp.ones((rows, rows), BF16))
    kern = functools.partial(_ffn_kernel, bb=bb, tt=tt)
    cur = lambda s: jnp.minimum(s, n_tiles - 1)
    prev = lambda s: jnp.maximum(s - 1, 0)
    const2 = lambda s: (0, 0)
    const3 = lambda s: (0, 0, 0)
    resident = pl.Buffered(1)
    return pl.pallas_call(
        kern,
        grid=(n_tiles + 1,),
        in_specs=[pl.BlockSpec((bb, tt, d), lambda s: (cur(s) // n_j, cur(s) % n_j, 0)),
                  pl.BlockSpec((bb, tt, MIX_WIDTH), lambda s: (cur(s) // n_j, cur(s) % n_j, 0)),
                  pl.BlockSpec((bb, 6, d), lambda s: (cur(s) // n_j, 0, 0)),
                  pl.BlockSpec((bb, 6, d), lambda s: (prev(s) // n_j, 0, 0)),
                  pl.BlockSpec((MIX_WIDTH, d), const2, pipeline_mode=resident),
                  pl.BlockSpec((1, d), const2),
                  pl.BlockSpec((1, d), const2),
                  pl.BlockSpec((d, 2 * ROUTE_LANES), const2, pipeline_mode=resident),
                  pl.BlockSpec((d, ROUTE_LANES), const2, pipeline_mode=resident),
                  pl.BlockSpec((1, ROUTE_LANES), const2),
                  pl.BlockSpec((rows, rows), const2, pipeline_mode=resident),
                  pl.BlockSpec(p["w_gate"].shape, const3, pipeline_mode=resident),
                  pl.BlockSpec(p["w_up"].shape, const3, pipeline_mode=resident),
                  pl.BlockSpec(p["w_down"].shape, const3, pipeline_mode=resident),
                  pl.BlockSpec((1, d), const2),
                  pl.BlockSpec((1, d), const2)],
        out_specs=pl.BlockSpec((bb, tt, d), lambda s: (prev(s) // n_j, prev(s) % n_j, 0)),
        out_shape=jax.ShapeDtypeStruct((bsz, seq, d), F32),
        scratch_shapes=[pltpu.VMEM((2, bb, tt, d), F32),
                        pltpu.VMEM((bb, tt, d + ROUTE_LANES), BF16),
                        pltpu.VMEM((srows, d), BF16),
                        pltpu.VMEM((srows, ROUTE_LANES), F32),
                        pltpu.VMEM((srows, d), BF16),
                        pltpu.VMEM((rows, srows), BF16)],
        compiler_params=_cparams("arbitrary"),
        name="ffn",
    )(x, mixin, mod, mod, p["w_out"], p["ln1_g"], p["ln1_b"], p["w_route_split"], p["w_route_hi"], p["b_route"],
      tri, p["w_gate"], p["w_up"], p["w_down"], p["ln2_g"], p["ln2_b"])


def _stream(x, mod, k0, v0, u0, p, *, mixer_tm, masked, bb, tt):
    mixin, nk, nv, nu = _mixer(x, mod, p["w_in"], p["bias"], p["sink"], p["w_dw"], p["b_dw"],
                               p["cln_g"], p["cln_b"], k0, v0, u0, tm=mixer_tm, masked=masked)
    y = _ffn(x, mixin, mod, p, bb=bb, tt=tt)
    bsz = x.shape[0]
    cache_shape = (1, bsz, WINDOW, N_KV_HEADS, HEAD_DIM)
    return y, nk.reshape(cache_shape), nv.reshape(cache_shape), nu[None]


def kernel(x_prompt, x_sample, cache_attn_k, cache_attn_v, state_conv, c_prompt, c_sample, rel_bias, w_ada, b_ada, w_in, attn_sinks, w_dw, b_dw, conv_ln_g, conv_ln_b, w_out, ln1_g, ln1_b, w_group, b_group, w_erouter, b_erouter, w_gate, w_up, w_down, ln2_g, ln2_b):
    bp = x_prompt.shape[0]
    bs = x_sample.shape[0]
    mod = _modulation(jnp.concatenate([c_prompt, c_sample], 0), w_ada[0], b_ada[0])
    mod = mod.reshape(bp + bs, 6, D_MODEL)

    w_route = jnp.concatenate([w_erouter[0].reshape(D_MODEL, N_EXPERTS), w_group[0]], 1)
    w_route = jnp.pad(w_route, ((0, 0), (0, ROUTE_LANES - N_EXPERTS - N_GROUPS)))
    b_route = jnp.concatenate([b_erouter[0].reshape(N_EXPERTS), b_group[0]])
    b_route = jnp.pad(b_route, (0, ROUTE_LANES - N_EXPERTS - N_GROUPS)).reshape(1, ROUTE_LANES)
    w_route_hi = w_route.astype(BF16)
    w_route_lo = (w_route - w_route_hi.astype(F32)).astype(BF16)

    p = {
        "w_in": w_in[0].astype(BF16),
        "bias": _relative_bias(rel_bias),
        "sink": attn_sinks[0],
        "w_dw": jnp.broadcast_to(w_dw[0][:, None, :], (CONV_K, SUBLANES, CONV_CH)),
        "b_dw": b_dw[0].reshape(1, CONV_CH),
        "cln_g": conv_ln_g[0].reshape(1, CONV_CH), "cln_b": conv_ln_b[0].reshape(1, CONV_CH),
        "w_out": w_out[0].astype(BF16),
        "ln1_g": ln1_g[0].reshape(1, D_MODEL), "ln1_b": ln1_b[0].reshape(1, D_MODEL),
        "w_route_split": jnp.concatenate([w_route_hi, w_route_lo], 1), "w_route_hi": w_route_hi,
        "b_route": b_route,
        "w_gate": w_gate[0].astype(BF16), "w_up": w_up[0].astype(BF16),
        "w_down": w_down[0].astype(BF16).reshape(N_GROUPS, EXPERTS_PER_GROUP * D_EXPERT, D_MODEL),
        "ln2_g": ln2_g[0].reshape(1, D_MODEL), "ln2_b": ln2_b[0].reshape(1, D_MODEL),
    }

    zk = jnp.zeros((bp, WINDOW, KV_WIDTH), F32)
    zu = jnp.zeros((bp, CONV_PAD, CONV_CH), F32)
    yp, pk, pv, pc = _stream(x_prompt, mod[:bp], zk, zk, zu, p, mixer_tm=512, masked=True, bb=1, tt=512)

    k0 = cache_attn_k[0].reshape(bs, WINDOW, KV_WIDTH)
    v0 = cache_attn_v[0].reshape(bs, WINDOW, KV_WIDTH)
    u0 = jnp.pad(state_conv[0], ((0, 0), (CONV_PAD - CONV_HIST, 0), (0, 0)))
    ts = x_sample.shape[1]
    ys, sk, sv, sc = _stream(x_sample, mod[bp:], k0, v0, u0, p, mixer_tm=ts, masked=False, bb=8, tt=ts)
    return yp, ys, pk, pv, pc, sk, sv, sc
```

```python
import functools
import math

import jax
import jax.numpy as jnp
from jax import lax
from jax.experimental import pallas as pl
from jax.experimental.pallas import tpu as pltpu

D_MODEL = 1024
CHUNK = 64
N_HEADS = 8
N_KV_HEADS = 2
HEAD_DIM = 64
Q_GROUP = N_HEADS // N_KV_HEADS
ATTN_WIDTH = N_HEADS * HEAD_DIM
KV_WIDTH = N_KV_HEADS * HEAD_DIM
WINDOW = 128
WINDOW_CHUNKS = WINDOW // CHUNK
KEYS = WINDOW + CHUNK
CONV_CH = D_MODEL // 2
CONV_K = 31
CONV_HIST = CONV_K - 1
CONV_PAD = 32
SUBLANES = 8
CONV_ROWS = 256
CONV_LANES = 128
MIX_WIDTH = ATTN_WIDTH + CONV_CH
IN_WIDTH = ATTN_WIDTH + 2 * KV_WIDTH + 2 * CONV_CH
N_BUCKETS = 32
MAX_DISTANCE = 128
N_GROUPS = 4
EXPERTS_PER_GROUP = 4
N_EXPERTS = N_GROUPS * EXPERTS_PER_GROUP
D_EXPERT = D_MODEL // 4
DEPTH = 1
ALPHA = (2 * DEPTH) ** 0.25
LN_EPS = 1e-5
NEG_INF = -1e30
LOG2E = math.log2(math.e)
ROUTE_LANES = 128
COMB_LO_LANE = 16
GROUP_LANE = 32
MOE_BLK = 160
MOE_ALIGN = 16
VMEM_LIMIT = 56 * 1024 * 1024

BF16 = jnp.bfloat16
F32 = jnp.float32


def _cparams(*sem):
    return pltpu.CompilerParams(dimension_semantics=sem, vmem_limit_bytes=VMEM_LIMIT)


def _mod_kernel(c_ref, w_ref, b_ref, o_ref):
    c = c_ref[...]
    s = c * jax.nn.sigmoid(c)
    o_ref[...] = jnp.dot(s, w_ref[...], preferred_element_type=F32,
                         precision=lax.Precision.HIGHEST) + b_ref[...]


def _modulation(c, w_ada, b_ada):
    n, d = c.shape
    width = w_ada.shape[1]
    tn = 1536
    return pl.pallas_call(
        _mod_kernel,
        grid=(width // tn,),
        in_specs=[pl.BlockSpec((n, d), lambda j: (0, 0)),
                  pl.BlockSpec((d, tn), lambda j: (0, j)),
                  pl.BlockSpec((1, tn), lambda j: (0, j))],
        out_specs=pl.BlockSpec((n, tn), lambda j: (0, j)),
        out_shape=jax.ShapeDtypeStruct((n, width), F32),
        compiler_params=_cparams("arbitrary"),
        name="modulation",
    )(c, w_ada, b_ada.reshape(1, width))


def _t5_bucket(rel):
    nb = N_BUCKETS // 2
    max_exact = nb // 2
    ret = jnp.where(rel > 0, nb, 0)
    n = jnp.abs(rel)
    large = max_exact + (jnp.log(jnp.maximum(n, 1).astype(jnp.float32) / max_exact)
                         / math.log(MAX_DISTANCE / max_exact) * (nb - max_exact)).astype(jnp.int32)
    large = jnp.minimum(large, nb - 1)
    return ret + jnp.where(n < max_exact, n, large)


def _bias_kernel(table_ref, bucket_ref, o_ref):
    bucket = bucket_ref[...]
    col = lax.broadcasted_iota(jnp.int32, bucket.shape, 1)
    for h in range(N_HEADS):
        acc = jnp.zeros(bucket.shape, F32)
        for b in range(N_BUCKETS):
            acc = jnp.where(bucket == b, table_ref[b, h], acc)
        for v in range(WINDOW_CHUNKS + 1):
            o_ref[v, h] = jnp.where(col >= v * CHUNK, acc * LOG2E, NEG_INF)


def _relative_bias(table):
    rel = jnp.arange(KEYS)[None, :] - WINDOW - jnp.arange(CHUNK)[:, None]
    bucket = _t5_bucket(rel).astype(jnp.int32)
    nv = WINDOW_CHUNKS + 1
    bias = pl.pallas_call(
        _bias_kernel,
        in_specs=[pl.BlockSpec(memory_space=pltpu.SMEM),
                  pl.BlockSpec((CHUNK, KEYS), lambda: (0, 0))],
        out_specs=pl.BlockSpec((nv, N_HEADS, CHUNK, KEYS), lambda: (0, 0, 0, 0)),
        out_shape=jax.ShapeDtypeStruct((nv, N_HEADS, CHUNK, KEYS), F32),
        name="relative_bias",
    )(table, bucket)
    bias = bias.reshape(nv, N_HEADS // 2, 2, CHUNK, KEYS)
    return jnp.transpose(bias, (0, 1, 3, 2, 4)).reshape(nv, N_HEADS // 2, CHUNK, 2 * KEYS)


def _layer_norm_rows(x, g, b):
    mu = jnp.mean(x, -1, keepdims=True)
    xc = x - mu
    var = jnp.mean(xc * xc, -1, keepdims=True)
    return xc * lax.rsqrt(var + LN_EPS) * g + b


def _mixer_kernel(x_ref, mod_ref, win_ref, bias_ref, sink_ref, wdw_ref, bdw_ref, clg_ref, clb_ref,
                  k0_ref, v0_ref, u0_ref,
                  mix_ref, nk_ref, nv_ref, nu_ref,
                  kz, vz, ush, yconv, s_buf, p_buf, e_buf, *, tm, masked):
    t = pl.program_id(1)
    nt = pl.num_programs(1)
    n_chunks = tm // CHUNK
    ucat = ush.at[0]

    left = lax.broadcasted_iota(jnp.int32, (1, KV_WIDTH), 1) < HEAD_DIM

    def store_kv(row0, k, v):
        n = k.shape[0]
        for dst, val in ((kz, k), (vz, v)):
            swapped = pltpu.roll(val, HEAD_DIM, 1)
            dst[0, row0:row0 + n, 0:KV_WIDTH] = jnp.where(left, val, 0.0).astype(BF16)
            dst[1, row0:row0 + n, 0:KV_WIDTH] = jnp.where(left, 0.0, swapped).astype(BF16)
            dst[2, row0:row0 + n, 0:KV_WIDTH] = jnp.where(left, swapped, 0.0).astype(BF16)
            dst[3, row0:row0 + n, 0:KV_WIDTH] = jnp.where(left, 0.0, val).astype(BF16)

    @pl.when(t == 0)
    def _():
        ones_left = jnp.broadcast_to(jnp.where(left, 1.0, 0.0).astype(BF16), (WINDOW + tm, KV_WIDTH))
        ones_right = jnp.broadcast_to(jnp.where(left, 0.0, 1.0).astype(BF16), (WINDOW + tm, KV_WIDTH))
        for i in range(2 * N_KV_HEADS):
            vz[i, :, KV_WIDTH:] = ones_left if i % 2 == 0 else ones_right
        store_kv(0, k0_ref[0], v0_ref[0])
        ucat[0:CONV_PAD, :] = u0_ref[0]

    sh1 = mod_ref[0, 0:1, :]
    sc1 = mod_ref[0, 1:2, :]
    h = (x_ref[0] * (1.0 + sc1) + sh1).astype(BF16)
    o1 = ATTN_WIDTH
    o3 = o1 + 2 * KV_WIDTH
    q = (jnp.dot(h, win_ref[:, 0:o1], preferred_element_type=F32) * (HEAD_DIM ** -0.5 * LOG2E)).astype(BF16)
    kv = jnp.dot(h, win_ref[:, o1:o3], preferred_element_type=F32)
    ag = jnp.dot(h, win_ref[:, o3:IN_WIDTH], preferred_element_type=F32)
    u = ag[:, 0:CONV_CH] * jax.nn.sigmoid(ag[:, CONV_CH:])
    store_kv(WINDOW, kv[:, 0:KV_WIDTH], kv[:, KV_WIDTH:])
    ucat[CONV_PAD:CONV_PAD + tm, :] = u

    if tm >= WINDOW:
        nk_ref[0] = kv[tm - WINDOW:, 0:KV_WIDTH]
        nv_ref[0] = kv[tm - WINDOW:, KV_WIDTH:]
    else:
        nk_ref[0, 0:WINDOW - tm, :] = k0_ref[0, tm:WINDOW, :]
        nv_ref[0, 0:WINDOW - tm, :] = v0_ref[0, tm:WINDOW, :]
        nk_ref[0, WINDOW - tm:, :] = kv[:, 0:KV_WIDTH]
        nv_ref[0, WINDOW - tm:, :] = kv[:, KV_WIDTH:]
    nu_ref[0] = ucat[CONV_PAD + tm - CONV_HIST:CONV_PAD + tm, :]

    n_pairs = N_HEADS // 2
    pair_w = 2 * HEAD_DIM
    first_head = lax.broadcasted_iota(jnp.int32, (CHUNK, pair_w), 1) < HEAD_DIM
    nt_dims = (((1,), (1,)), ((), ()))

    def window(ref, j, c):
        kvh = (2 * j) // Q_GROUP
        rows = slice(c * CHUNK, c * CHUNK + KEYS)
        return jnp.concatenate([ref[2 * kvh, rows, :], ref[2 * kvh + 1, rows, :]], axis=0)

    for c in range(n_chunks):
        if masked and c < WINDOW_CHUNKS:
            variant = jnp.where(t == 0, WINDOW_CHUNKS - c, 0)
        else:
            variant = 0
        for j in range(n_pairs):
            qp = q[c * CHUNK:(c + 1) * CHUNK, j * pair_w:(j + 1) * pair_w]
            s_buf[c, j] = (lax.dot_general(qp, window(kz, j, c), nt_dims, preferred_element_type=F32)
                           + bias_ref[variant, j])

    for c in range(n_chunks):
        for j in range(n_pairs):
            s0 = s_buf[c, j, :, 0:pair_w]
            s1 = s_buf[c, j, :, pair_w:2 * pair_w]
            s2 = s_buf[c, j, :, 2 * pair_w:]
            sink_a = sink_ref[2 * j] * LOG2E
            sink_b = sink_ref[2 * j + 1] * LOG2E
            m_a = jnp.maximum(jnp.max(jnp.maximum(s0, jnp.where(first_head, s1, NEG_INF)), -1, keepdims=True), sink_a)
            m_b = jnp.maximum(jnp.max(jnp.maximum(jnp.where(first_head, NEG_INF, s1), s2), -1, keepdims=True), sink_b)
            p_buf[c, j, :, 0:pair_w] = jnp.exp2(s0 - m_a).astype(BF16)
            p_buf[c, j, :, pair_w:2 * pair_w] = jnp.exp2(s1 - jnp.where(first_head, m_a, m_b)).astype(BF16)
            p_buf[c, j, :, 2 * pair_w:] = jnp.exp2(s2 - m_b).astype(BF16)
            e_buf[c, j] = jnp.where(first_head, jnp.exp2(sink_a - m_a), jnp.exp2(sink_b - m_b))

    span = tm + CONV_PAD - SUBLANES
    for s in range(1, SUBLANES):
        ush[s, 0:span, :] = ucat[s:s + span, :]
    off = CONV_PAD - CONV_HIST
    rb = min(CONV_ROWS, tm)
    groups = rb // SUBLANES

    def conv_rows(lanes, r, carry):
        r0 = pl.multiple_of(r * rb, rb)
        partial = []
        for shift in range(SUBLANES):
            taps = [k for k in range(CONV_K) if (off + k) % SUBLANES == shift]
            top = max(off + k - shift for k in taps)
            slab = ush[shift, pl.ds(r0, rb + top), lanes].reshape(groups + top // SUBLANES, SUBLANES, CONV_LANES)
            acc = None
            for k in taps:
                g0 = (off + k - shift) // SUBLANES
                term = slab[g0:g0 + groups] * wdw_ref[k, :, lanes]
                acc = term if acc is None else acc + term
            partial.append(acc)
        while len(partial) > 1:
            partial = [a + b for a, b in zip(partial[0::2], partial[1::2])]
        yconv[pl.ds(r0, rb), lanes] = (partial[0] + bdw_ref[:, lanes]).reshape(rb, CONV_LANES)
        return carry

    for part in range(CONV_CH // CONV_LANES):
        lanes = slice(part * CONV_LANES, (part + 1) * CONV_LANES)
        lax.fori_loop(0, tm // rb, functools.partial(conv_rows, lanes), 0)
    y = _layer_norm_rows(yconv[...], clg_ref[...], clb_ref[...])
    mix_ref[0, :, ATTN_WIDTH:] = (y * jax.nn.sigmoid(y)).astype(BF16)

    for c in range(n_chunks):
        for j in range(n_pairs):
            res = jnp.dot(p_buf[c, j], window(vz, j, c), preferred_element_type=F32)
            out = res[:, 0:pair_w] * (1.0 / (res[:, pair_w:] + e_buf[c, j]))
            mix_ref[0, c * CHUNK:(c + 1) * CHUNK, j * pair_w:(j + 1) * pair_w] = out.astype(BF16)

    @pl.when(t < nt - 1)
    def _():
        if tm >= WINDOW:
            for i in range(2 * N_KV_HEADS):
                kz[i, 0:WINDOW, :] = kz[i, tm:tm + WINDOW, :]
                vz[i, 0:WINDOW, 0:KV_WIDTH] = vz[i, tm:tm + WINDOW, 0:KV_WIDTH]
            ucat[0:CONV_PAD, :] = ucat[tm:tm + CONV_PAD, :]


def _mixer(x, mod, w_in, bias, sink, w_dw, b_dw, cln_g, cln_b, k0, v0, u0, *, tm, masked):
    bsz, seq, d = x.shape
    assert seq % tm == 0 and tm % CHUNK == 0 and (tm >= WINDOW or seq == tm)
    nt = seq // tm
    kern = functools.partial(_mixer_kernel, tm=tm, masked=masked)
    const2 = lambda b, t: (0, 0)
    const3 = lambda b, t: (0, 0, 0)
    per_b = lambda b, t: (b, 0, 0)
    return pl.pallas_call(
        kern,
        grid=(bsz, nt),
        in_specs=[pl.BlockSpec((1, tm, d), lambda b, t: (b, t, 0)),
                  pl.BlockSpec((1, 6, d), per_b),
                  pl.BlockSpec((d, IN_WIDTH), const2),
                  pl.BlockSpec((WINDOW_CHUNKS + 1, N_HEADS // 2, CHUNK, 2 * KEYS), lambda b, t: (0, 0, 0, 0)),
                  pl.BlockSpec(memory_space=pltpu.SMEM),
                  pl.BlockSpec((CONV_K, SUBLANES, CONV_CH), const3),
                  pl.BlockSpec((1, CONV_CH), const2),
                  pl.BlockSpec((1, CONV_CH), const2),
                  pl.BlockSpec((1, CONV_CH), const2),
                  pl.BlockSpec((1, WINDOW, KV_WIDTH), per_b),
                  pl.BlockSpec((1, WINDOW, KV_WIDTH), per_b),
                  pl.BlockSpec((1, CONV_PAD, CONV_CH), per_b)],
        out_specs=[pl.BlockSpec((1, tm, MIX_WIDTH), lambda b, t: (b, t, 0)),
                   pl.BlockSpec((1, WINDOW, KV_WIDTH), per_b),
                   pl.BlockSpec((1, WINDOW, KV_WIDTH), per_b),
                   pl.BlockSpec((1, CONV_HIST, CONV_CH), per_b)],
        out_shape=[jax.ShapeDtypeStruct((bsz, seq, MIX_WIDTH), BF16),
                   jax.ShapeDtypeStruct((bsz, WINDOW, KV_WIDTH), F32),
                   jax.ShapeDtypeStruct((bsz, WINDOW, KV_WIDTH), F32),
                   jax.ShapeDtypeStruct((bsz, CONV_HIST, CONV_CH), F32)],
        scratch_shapes=[pltpu.VMEM((2 * N_KV_HEADS, WINDOW + tm, KV_WIDTH), BF16),
                        pltpu.VMEM((2 * N_KV_HEADS, WINDOW + tm, 2 * KV_WIDTH), BF16),
                        pltpu.VMEM((SUBLANES, CONV_PAD + tm, CONV_CH), F32),
                        pltpu.VMEM((tm, CONV_CH), F32),
                        pltpu.VMEM((tm // CHUNK, N_HEADS // 2, CHUNK, 2 * KEYS), F32),
                        pltpu.VMEM((tm // CHUNK, N_HEADS // 2, CHUNK, 2 * KEYS), BF16),
                        pltpu.VMEM((tm // CHUNK, N_HEADS // 2, CHUNK, 2 * HEAD_DIM), F32)],
        compiler_params=_cparams("parallel", "arbitrary"),
        name="mixer",
    )(x, mod, w_in, bias, sink, w_dw, b_dw, cln_g, cln_b, k0, v0, u0)


def _post_kernel(x_ref, mix_ref, mod_ref, wout_ref, g_ref, b_ref, wr_split_ref, wr_hi_ref, br_ref,
                 x1_ref, h2e_ref, *, bb, tt):
    rows = bb * tt
    mix = jnp.dot(mix_ref[...].reshape(rows, MIX_WIDTH), wout_ref[...], preferred_element_type=F32)
    g1 = mod_ref[:, 2:3, :]
    sh2 = mod_ref[:, 3:4, :]
    sc2 = mod_ref[:, 4:5, :]
    r = ALPHA * x_ref[...] + (1.0 + g1) * mix.reshape(bb, tt, D_MODEL)
    x1 = _layer_norm_rows(r, g_ref[...], b_ref[...])
    x1_ref[...] = x1
    h2 = (x1 * (1.0 + sc2) + sh2).reshape(rows, D_MODEL)
    h2_hi = h2.astype(BF16)
    h2e_ref[:, :, 0:D_MODEL] = h2_hi.reshape(bb, tt, D_MODEL)

    h2_lo = (h2 - h2_hi.astype(F32)).astype(BF16)
    hi_terms = jnp.dot(h2_hi, wr_split_ref[...], preferred_element_type=F32)
    logits = (hi_terms[:, 0:ROUTE_LANES] + hi_terms[:, ROUTE_LANES:]
              + jnp.dot(h2_lo, wr_hi_ref[...], preferred_element_type=F32) + br_ref[...])
    lane = lax.broadcasted_iota(jnp.int32, (rows, ROUTE_LANES), 1).astype(F32)
    far = float(ROUTE_LANES)
    is_group = (lane >= N_EXPERTS) & (lane < N_EXPERTS + N_GROUPS)
    gl = jnp.where(is_group, logits, NEG_INF)
    gmax = jnp.max(gl, -1, keepdims=True)
    gidx = jnp.min(jnp.where(is_group & (gl == gmax), lane, far), -1, keepdims=True) - N_EXPERTS
    pg = 1.0 / jnp.sum(jnp.where(is_group, jnp.exp(gl - gmax), 0.0), -1, keepdims=True)
    in_group = (lane >= gidx * EXPERTS_PER_GROUP) & (lane < (gidx + 1) * EXPERTS_PER_GROUP)
    el = jnp.where(in_group, logits, NEG_INF)
    v1 = jnp.max(el, -1, keepdims=True)
    i1 = jnp.min(jnp.where(in_group & (el == v1), lane, far), -1, keepdims=True)
    rest = in_group & (lane != i1)
    el2 = jnp.where(rest, logits, NEG_INF)
    v2 = jnp.max(el2, -1, keepdims=True)
    i2 = jnp.min(jnp.where(rest & (el2 == v2), lane, far), -1, keepdims=True)
    e2 = jnp.exp(v2 - v1)
    w1 = pg / (1.0 + e2)
    w2 = pg * e2 / (1.0 + e2)
    comb = jnp.where(lane == i1, w1, jnp.where(lane == i2, w2, 0.0))

    c_hi = comb.astype(BF16).astype(F32)
    c_lo = (comb - c_hi).astype(BF16).astype(F32)
    rec = c_hi + pltpu.roll(c_lo, COMB_LO_LANE, 1) + jnp.where(lane == gidx + GROUP_LANE, 1.0, 0.0)
    h2e_ref[:, :, D_MODEL:] = rec.astype(BF16).reshape(bb, tt, ROUTE_LANES)


def _moe_rows(rows):
    return rows + N_GROUPS * MOE_ALIGN + MOE_BLK


def _moe_experts(h2e_ref, tri_ref, wg_ref, wu_ref, wd_ref, xs_ref, cw_ref, ys_ref, scatter_ref, other_stream,
                 *, bb, tt):
    rows = bb * tt
    srows = _moe_rows(rows)
    ext = h2e_ref[...].reshape(rows, D_MODEL + ROUTE_LANES)
    rec = ext[:, D_MODEL:]
    cum = jnp.dot(tri_ref[...], rec, preferred_element_type=F32)
    lane1 = lax.broadcasted_iota(jnp.int32, (1, ROUTE_LANES), 1)
    cnt = cum[rows - 1:rows, :]
    off = jnp.int32(0)
    offs, counts = [], []
    off_vec = jnp.zeros((1, ROUTE_LANES), F32)
    for g in range(N_GROUPS):
        n_g = jnp.sum(jnp.where(lane1 == GROUP_LANE + g, cnt, 0.0)).astype(jnp.int32)
        offs.append(off)
        counts.append(n_g)
        off_vec = jnp.where(lane1 == GROUP_LANE + g, off.astype(F32), off_vec)
        off = off + ((n_g + (MOE_ALIGN - 1)) // MOE_ALIGN) * MOE_ALIGN

    lane = lax.broadcasted_iota(jnp.int32, (rows, ROUTE_LANES), 1)
    mine = (lane >= GROUP_LANE) & (lane < GROUP_LANE + N_GROUPS) & (rec.astype(F32) > 0.0)
    pos = jnp.sum(jnp.where(mine, cum - 1.0 + off_vec, 0.0), -1, keepdims=True)
    scatter_ref[...] = jnp.where(pos == lax.broadcasted_iota(jnp.int32, (rows, srows), 1).astype(F32),
                                 1.0, 0.0).astype(BF16)
    pos_row = jnp.transpose(jnp.broadcast_to(pos, (rows, ROUTE_LANES)))[0:1, :]
    grows = rows + N_GROUPS * MOE_ALIGN
    gather = jnp.where(lax.broadcasted_iota(jnp.int32, (grows, rows), 0).astype(F32) == pos_row, 1.0, 0.0).astype(BF16)
    xs_ref[0:grows, :] = jnp.dot(gather, ext[:, 0:D_MODEL], preferred_element_type=F32).astype(BF16)
    rs = jnp.dot(gather, rec, preferred_element_type=F32)
    cw_ref[0:grows, :] = rs + pltpu.roll(rs, ROUTE_LANES - COMB_LO_LANE, 1)

    def experts(g, r0):
        xb = xs_ref[pl.ds(r0, MOE_BLK), :]
        cwb = cw_ref[pl.ds(r0, MOE_BLK), :]
        parts = []
        for j in range(EXPERTS_PER_GROUP):
            e = g * EXPERTS_PER_GROUP + j
            hg = jnp.dot(xb, wg_ref[e], preferred_element_type=F32)
            hu = jnp.dot(xb, wu_ref[e], preferred_element_type=F32)
            parts.append((hg * jax.nn.sigmoid(hg) * hu * cwb[:, e:e + 1]).astype(BF16))
        act = jnp.concatenate(parts, axis=1)
        return jnp.dot(act, wd_ref[g], preferred_element_type=F32)

    for g in range(N_GROUPS):
        r0 = pl.multiple_of(offs[g], MOE_ALIGN)
        ys_ref[pl.ds(r0, MOE_BLK), :] = experts(g, r0).astype(BF16)
        if g == 1:
            other_stream()

    def extra_block(g, i, carry):
        r0 = pl.multiple_of(offs[g] + i * MOE_BLK, MOE_ALIGN)
        row = r0 + lax.broadcasted_iota(jnp.int32, (MOE_BLK, 1), 0)
        old = ys_ref[pl.ds(r0, MOE_BLK), :].astype(F32)
        ys_ref[pl.ds(r0, MOE_BLK), :] = jnp.where(row < offs[g] + counts[g], experts(g, r0), old).astype(BF16)
        return carry

    def extra_blocks():
        for g in range(N_GROUPS):
            lax.fori_loop(1, (counts[g] + (MOE_BLK - 1)) // MOE_BLK, functools.partial(extra_block, g), 0)

    return extra_blocks


def _moe_finish(x1_ref, mod_ref, g_ref, b_ref, ys_ref, scatter_ref, y_ref, *, bb, tt):
    ff = jnp.dot(scatter_ref[...], ys_ref[...], preferred_element_type=F32)
    g2 = mod_ref[:, 5:6, :]
    r = ALPHA * x1_ref[...] + (1.0 + g2) * ff.reshape(bb, tt, D_MODEL)
    y_ref[...] = _layer_norm_rows(r, g_ref[...], b_ref[...])


def _ffn_kernel(x_ref, mix_ref, mod_cur_ref, mod_prev_ref, wout_ref, g1_ref, b1_ref, wr_split_ref, wr_hi_ref,
                br_ref, tri_ref, wg_ref, wu_ref, wd_ref, g2_ref, b2_ref, y_ref,
                x1_buf, h2e_buf, xs_ref, cw_ref, ys_ref, scatter_ref, *, bb, tt):
    s = pl.program_id(0)
    cur = s % 2
    prev = 1 - cur

    @pl.when(s == 0)
    def _():
        ys_ref[...] = jnp.zeros_like(ys_ref)
        xs_ref[...] = jnp.zeros_like(xs_ref)
        cw_ref[...] = jnp.zeros_like(cw_ref)
        x1_buf[1] = jnp.zeros(x1_buf.shape[1:], F32)
        h2e_buf[...] = jnp.zeros(h2e_buf.shape, BF16)

    post = functools.partial(_post_kernel, x_ref, mix_ref, mod_cur_ref, wout_ref, g1_ref, b1_ref, wr_split_ref,
                             wr_hi_ref, br_ref, x1_buf.at[cur], h2e_buf, bb=bb, tt=tt)
    extra_blocks = _moe_experts(h2e_buf, tri_ref, wg_ref, wu_ref, wd_ref, xs_ref, cw_ref, ys_ref,
                                scatter_ref, post, bb=bb, tt=tt)
    extra_blocks()
    _moe_finish(x1_buf.at[prev], mod_prev_ref, g2_ref, b2_ref, ys_ref, scatter_ref, y_ref, bb=bb, tt=tt)


def _ffn(x, mixin, mod, p, *, bb, tt):
    bsz, seq, d = x.shape
    assert bsz % bb == 0 and seq % tt == 0
    rows = bb * tt
    srows = _moe_rows(rows)
    n_j = seq // tt
    n_tiles = (bsz // bb) * n_j
    tri = jnp.tril(jnp.ones((rows, rows), BF16))
    kern = functools.partial(_ffn_kernel, bb=bb, tt=tt)
    cur = lambda s: jnp.minimum(s, n_tiles - 1)
    prev = lambda s: jnp.maximum(s - 1, 0)
    const2 = lambda s: (0, 0)
    const3 = lambda s: (0, 0, 0)
    resident = pl.Buffered(1)
    return pl.pallas_call(
        kern,
        grid=(n_tiles + 1,),
        in_specs=[pl.BlockSpec((bb, tt, d), lambda s: (cur(s) // n_j, cur(s) % n_j, 0)),
                  pl.BlockSpec((bb, tt, MIX_WIDTH), lambda s: (cur(s) // n_j, cur(s) % n_j, 0)),
                  pl.BlockSpec((bb, 6, d), lambda s: (cur(s) // n_j, 0, 0)),
                  pl.BlockSpec((bb, 6, d), lambda s: (prev(s) // n_j, 0, 0)),
                  pl.BlockSpec((MIX_WIDTH, d), const2, pipeline_mode=resident),
                  pl.BlockSpec((1, d), const2),
                  pl.BlockSpec((1, d), const2),
                  pl.BlockSpec((d, 2 * ROUTE_LANES), const2, pipeline_mode=resident),
                  pl.BlockSpec((d, ROUTE_LANES), const2, pipeline_mode=resident),
                  pl.BlockSpec((1, ROUTE_LANES), const2),
                  pl.BlockSpec((rows, rows), const2, pipeline_mode=resident),
                  pl.BlockSpec(p["w_gate"].shape, const3, pipeline_mode=resident),
                  pl.BlockSpec(p["w_up"].shape, const3, pipeline_mode=resident),
                  pl.BlockSpec(p["w_down"].shape, const3, pipeline_mode=resident),
                  pl.BlockSpec((1, d), const2),
                  pl.BlockSpec((1, d), const2)],
        out_specs=pl.BlockSpec((bb, tt, d), lambda s: (prev(s) // n_j, prev(s) % n_j, 0)),
        out_shape=jax.ShapeDtypeStruct((bsz, seq, d), F32),
        scratch_shapes=[pltpu.VMEM((2, bb, tt, d), F32),
                        pltpu.VMEM((bb, tt, d + ROUTE_LANES), BF16),
                        pltpu.VMEM((srows, d), BF16),
                        pltpu.VMEM((srows, ROUTE_LANES), F32),
                        pltpu.VMEM((srows, d), BF16),
                        pltpu.VMEM((rows, srows), BF16)],
        compiler_params=_cparams("arbitrary"),
        name="ffn",
    )(x, mixin, mod, mod, p["w_out"], p["ln1_g"], p["ln1_b"], p["w_route_split"], p["w_route_hi"], p["b_route"],
      tri, p["w_gate"], p["w_up"], p["w_down"], p["ln2_g"], p["ln2_b"])


def _stream(x, mod, k0, v0, u0, p, *, mixer_tm, masked, bb, tt):
    mixin, nk, nv, nu = _mixer(x, mod, p["w_in"], p["bias"], p["sink"], p["w_dw"], p["b_dw"],
                               p["cln_g"], p["cln_b"], k0, v0, u0, tm=mixer_tm, masked=masked)
    y = _ffn(x, mixin, mod, p, bb=bb, tt=tt)
    bsz = x.shape[0]
    cache_shape = (1, bsz, WINDOW, N_KV_HEADS, HEAD_DIM)
    return y, nk.reshape(cache_shape), nv.reshape(cache_shape), nu[None]


def kernel(x_prompt, x_sample, cache_attn_k, cache_attn_v, state_conv, c_prompt, c_sample, rel_bias, w_ada, b_ada, w_in, attn_sinks, w_dw, b_dw, conv_ln_g, conv_ln_b, w_out, ln1_g, ln1_b, w_group, b_group, w_erouter, b_erouter, w_gate, w_up, w_down, ln2_g, ln2_b):
    bp = x_prompt.shape[0]
    bs = x_sample.shape[0]
    mod = _modulation(jnp.concatenate([c_prompt, c_sample], 0), w_ada[0], b_ada[0])
    mod = mod.reshape(bp + bs, 6, D_MODEL)

    w_route = jnp.concatenate([w_erouter[0].reshape(D_MODEL, N_EXPERTS), w_group[0]], 1)
    w_route = jnp.pad(w_route, ((0, 0), (0, ROUTE_LANES - N_EXPERTS - N_GROUPS)))
    b_route = jnp.concatenate([b_erouter[0].reshape(N_EXPERTS), b_group[0]])
    b_route = jnp.pad(b_route, (0, ROUTE_LANES - N_EXPERTS - N_GROUPS)).reshape(1, ROUTE_LANES)
    w_route_hi = w_route.astype(BF16)
    w_route_lo = (w_route - w_route_hi.astype(F32)).astype(BF16)

    p = {
        "w_in": w_in[0].astype(BF16),
        "bias": _relative_bias(rel_bias),
        "sink": attn_sinks[0],
        "w_dw": jnp.broadcast_to(w_dw[0][:, None, :], (CONV_K, SUBLANES, CONV_CH)),
        "b_dw": b_dw[0].reshape(1, CONV_CH),
        "cln_g": conv_ln_g[0].reshape(1, CONV_CH), "cln_b": conv_ln_b[0].reshape(1, CONV_CH),
        "w_out": w_out[0].astype(BF16),
        "ln1_g": ln1_g[0].reshape(1, D_MODEL), "ln1_b": ln1_b[0].reshape(1, D_MODEL),
        "w_route_split": jnp.concatenate([w_route_hi, w_route_lo], 1), "w_route_hi": w_route_hi,
        "b_route": b_route,
        "w_gate": w_gate[0].astype(BF16), "w_up": w_up[0].astype(BF16),
        "w_down": w_down[0].astype(BF16).reshape(N_GROUPS, EXPERTS_PER_GROUP * D_EXPERT, D_MODEL),
        "ln2_g": ln2_g[0].reshape(1, D_MODEL), "ln2_b": ln2_b[0].reshape(1, D_MODEL),
    }

    zk = jnp.zeros((bp, WINDOW, KV_WIDTH), F32)
    zu = jnp.zeros((bp, CONV_PAD, CONV_CH), F32)
    yp, pk, pv, pc = _stream(x_prompt, mod[:bp], zk, zk, zu, p, mixer_tm=512, masked=True, bb=1, tt=512)

    k0 = cache_attn_k[0].reshape(bs, WINDOW, KV_WIDTH)
    v0 = cache_attn_v[0].reshape(bs, WINDOW, KV_WIDTH)
    u0 = jnp.pad(state_conv[0], ((0, 0), (CONV_PAD - CONV_HIST, 0), (0, 0)))
    ts = x_sample.shape[1]
    ys, sk, sv, sc = _stream(x_sample, mod[bp:], k0, v0, u0, p, mixer_tm=ts, masked=False, bb=8, tt=ts)
    return yp, ys, pk, pv, pc, sk, sv, sc
```

```python
import functools
import math

import jax
import jax.numpy as jnp
from jax import lax
from jax.experimental import pallas as pl
from jax.experimental.pallas import tpu as pltpu

D_MODEL = 1024
CHUNK = 64
N_HEADS = 8
N_KV_HEADS = 2
HEAD_DIM = 64
Q_GROUP = N_HEADS // N_KV_HEADS
ATTN_WIDTH = N_HEADS * HEAD_DIM
KV_WIDTH = N_KV_HEADS * HEAD_DIM
WINDOW = 128
WINDOW_CHUNKS = WINDOW // CHUNK
KEYS = WINDOW + CHUNK
CONV_CH = D_MODEL // 2
CONV_K = 31
CONV_HIST = CONV_K - 1
CONV_PAD = 32
SUBLANES = 8
CONV_ROWS = 256
CONV_LANES = 128
MIX_WIDTH = ATTN_WIDTH + CONV_CH
IN_WIDTH = ATTN_WIDTH + 2 * KV_WIDTH + 2 * CONV_CH
N_BUCKETS = 32
MAX_DISTANCE = 128
N_GROUPS = 4
EXPERTS_PER_GROUP = 4
N_EXPERTS = N_GROUPS * EXPERTS_PER_GROUP
D_EXPERT = D_MODEL // 4
DEPTH = 1
ALPHA = (2 * DEPTH) ** 0.25
LN_EPS = 1e-5
NEG_INF = -1e30
LOG2E = math.log2(math.e)
ROUTE_LANES = 128
COMB_LO_LANE = 16
GROUP_LANE = 32
MOE_BLK = 160
MOE_ALIGN = 16
TILE_PARTS = 2
VMEM_LIMIT = 56 * 1024 * 1024

BF16 = jnp.bfloat16
F32 = jnp.float32


def _cparams(*sem):
    return pltpu.CompilerParams(dimension_semantics=sem, vmem_limit_bytes=VMEM_LIMIT)


def _mod_kernel(c_ref, w_ref, b_ref, o_ref):
    c = c_ref[...]
    s = c * jax.nn.sigmoid(c)
    o_ref[...] = jnp.dot(s, w_ref[...], preferred_element_type=F32,
                         precision=lax.Precision.HIGHEST) + b_ref[...]


def _modulation(c, w_ada, b_ada):
    n, d = c.shape
    width = w_ada.shape[1]
    tn = 1536
    return pl.pallas_call(
        _mod_kernel,
        grid=(width // tn,),
        in_specs=[pl.BlockSpec((n, d), lambda j: (0, 0)),
                  pl.BlockSpec((d, tn), lambda j: (0, j)),
                  pl.BlockSpec((1, tn), lambda j: (0, j))],
        out_specs=pl.BlockSpec((n, tn), lambda j: (0, j)),
        out_shape=jax.ShapeDtypeStruct((n, width), F32),
        compiler_params=_cparams("arbitrary"),
        name="modulation",
    )(c, w_ada, b_ada.reshape(1, width))


def _t5_bucket(rel):
    nb = N_BUCKETS // 2
    max_exact = nb // 2
    ret = jnp.where(rel > 0, nb, 0)
    n = jnp.abs(rel)
    large = max_exact + (jnp.log(jnp.maximum(n, 1).astype(jnp.float32) / max_exact)
                         / math.log(MAX_DISTANCE / max_exact) * (nb - max_exact)).astype(jnp.int32)
    large = jnp.minimum(large, nb - 1)
    return ret + jnp.where(n < max_exact, n, large)


def _bias_kernel(table_ref, bucket_ref, o_ref):
    bucket = bucket_ref[...]
    col = lax.broadcasted_iota(jnp.int32, bucket.shape, 1)
    for h in range(N_HEADS):
        acc = jnp.zeros(bucket.shape, F32)
        for b in range(N_BUCKETS):
            acc = jnp.where(bucket == b, table_ref[b, h], acc)
        for v in range(WINDOW_CHUNKS + 1):
            o_ref[v, h] = jnp.where(col >= v * CHUNK, acc * LOG2E, NEG_INF)


def _relative_bias(table):
    rel = jnp.arange(KEYS)[None, :] - WINDOW - jnp.arange(CHUNK)[:, None]
    bucket = _t5_bucket(rel).astype(jnp.int32)
    nv = WINDOW_CHUNKS + 1
    bias = pl.pallas_call(
        _bias_kernel,
        in_specs=[pl.BlockSpec(memory_space=pltpu.SMEM),
                  pl.BlockSpec((CHUNK, KEYS), lambda: (0, 0))],
        out_specs=pl.BlockSpec((nv, N_HEADS, CHUNK, KEYS), lambda: (0, 0, 0, 0)),
        out_shape=jax.ShapeDtypeStruct((nv, N_HEADS, CHUNK, KEYS), F32),
        name="relative_bias",
    )(table, bucket)
    bias = bias.reshape(nv, N_HEADS // 2, 2, CHUNK, KEYS)
    return jnp.transpose(bias, (0, 1, 3, 2, 4)).reshape(nv, N_HEADS // 2, CHUNK, 2 * KEYS)


def _layer_norm_rows(x, g, b):
    mu = jnp.mean(x, -1, keepdims=True)
    xc = x - mu
    var = jnp.mean(xc * xc, -1, keepdims=True)
    return xc * lax.rsqrt(var + LN_EPS) * g + b


def _mixer_kernel(x_ref, mod_ref, win_ref, bias_ref, sink_ref, wdw_ref, bdw_ref, clg_ref, clb_ref,
                  k0_ref, v0_ref, u0_ref,
                  mix_ref, nk_ref, nv_ref, nu_ref,
                  kz, vz, ush, yconv, s_buf, p_buf, e_buf, *, tm, masked):
    t = pl.program_id(1)
    nt = pl.num_programs(1)
    n_chunks = tm // CHUNK
    ucat = ush.at[0]

    left = lax.broadcasted_iota(jnp.int32, (1, KV_WIDTH), 1) < HEAD_DIM

    def store_kv(row0, k, v):
        n = k.shape[0]
        for dst, val in ((kz, k), (vz, v)):
            swapped = pltpu.roll(val, HEAD_DIM, 1)
            dst[0, row0:row0 + n, 0:KV_WIDTH] = jnp.where(left, val, 0.0).astype(BF16)
            dst[1, row0:row0 + n, 0:KV_WIDTH] = jnp.where(left, 0.0, swapped).astype(BF16)
            dst[2, row0:row0 + n, 0:KV_WIDTH] = jnp.where(left, swapped, 0.0).astype(BF16)
            dst[3, row0:row0 + n, 0:KV_WIDTH] = jnp.where(left, 0.0, val).astype(BF16)

    @pl.when(t == 0)
    def _():
        ones_left = jnp.broadcast_to(jnp.where(left, 1.0, 0.0).astype(BF16), (WINDOW + tm, KV_WIDTH))
        ones_right = jnp.broadcast_to(jnp.where(left, 0.0, 1.0).astype(BF16), (WINDOW + tm, KV_WIDTH))
        for i in range(2 * N_KV_HEADS):
            vz[i, :, KV_WIDTH:] = ones_left if i % 2 == 0 else ones_right
        store_kv(0, k0_ref[0], v0_ref[0])
        ucat[0:CONV_PAD, :] = u0_ref[0]

    sh1 = mod_ref[0, 0:1, :]
    sc1 = mod_ref[0, 1:2, :]
    h = (x_ref[0] * (1.0 + sc1) + sh1).astype(BF16)
    o1 = ATTN_WIDTH
    o3 = o1 + 2 * KV_WIDTH
    q = (jnp.dot(h, win_ref[:, 0:o1], preferred_element_type=F32) * (HEAD_DIM ** -0.5 * LOG2E)).astype(BF16)
    kv = jnp.dot(h, win_ref[:, o1:o3], preferred_element_type=F32)
    ag = jnp.dot(h, win_ref[:, o3:IN_WIDTH], preferred_element_type=F32)
    u = ag[:, 0:CONV_CH] * jax.nn.sigmoid(ag[:, CONV_CH:])
    store_kv(WINDOW, kv[:, 0:KV_WIDTH], kv[:, KV_WIDTH:])
    ucat[CONV_PAD:CONV_PAD + tm, :] = u

    if tm >= WINDOW:
        nk_ref[0] = kv[tm - WINDOW:, 0:KV_WIDTH]
        nv_ref[0] = kv[tm - WINDOW:, KV_WIDTH:]
    else:
        nk_ref[0, 0:WINDOW - tm, :] = k0_ref[0, tm:WINDOW, :]
        nv_ref[0, 0:WINDOW - tm, :] = v0_ref[0, tm:WINDOW, :]
        nk_ref[0, WINDOW - tm:, :] = kv[:, 0:KV_WIDTH]
        nv_ref[0, WINDOW - tm:, :] = kv[:, KV_WIDTH:]
    nu_ref[0] = ucat[CONV_PAD + tm - CONV_HIST:CONV_PAD + tm, :]

    n_pairs = N_HEADS // 2
    pair_w = 2 * HEAD_DIM
    first_head = lax.broadcasted_iota(jnp.int32, (CHUNK, pair_w), 1) < HEAD_DIM
    nt_dims = (((1,), (1,)), ((), ()))

    def window(ref, j, c):
        kvh = (2 * j) // Q_GROUP
        rows = slice(c * CHUNK, c * CHUNK + KEYS)
        return jnp.concatenate([ref[2 * kvh, rows, :], ref[2 * kvh + 1, rows, :]], axis=0)

    for c in range(n_chunks):
        if masked and c < WINDOW_CHUNKS:
            variant = jnp.where(t == 0, WINDOW_CHUNKS - c, 0)
        else:
            variant = 0
        for j in range(n_pairs):
            qp = q[c * CHUNK:(c + 1) * CHUNK, j * pair_w:(j + 1) * pair_w]
            s_buf[c, j] = (lax.dot_general(qp, window(kz, j, c), nt_dims, preferred_element_type=F32)
                           + bias_ref[variant, j])

    for c in range(n_chunks):
        for j in range(n_pairs):
            s0 = s_buf[c, j, :, 0:pair_w]
            s1 = s_buf[c, j, :, pair_w:2 * pair_w]
            s2 = s_buf[c, j, :, 2 * pair_w:]
            sink_a = sink_ref[2 * j] * LOG2E
            sink_b = sink_ref[2 * j + 1] * LOG2E
            m_a = jnp.maximum(jnp.max(jnp.maximum(s0, jnp.where(first_head, s1, NEG_INF)), -1, keepdims=True), sink_a)
            m_b = jnp.maximum(jnp.max(jnp.maximum(jnp.where(first_head, NEG_INF, s1), s2), -1, keepdims=True), sink_b)
            p_buf[c, j, :, 0:pair_w] = jnp.exp2(s0 - m_a).astype(BF16)
            p_buf[c, j, :, pair_w:2 * pair_w] = jnp.exp2(s1 - jnp.where(first_head, m_a, m_b)).astype(BF16)
            p_buf[c, j, :, 2 * pair_w:] = jnp.exp2(s2 - m_b).astype(BF16)
            e_buf[c, j] = jnp.where(first_head, jnp.exp2(sink_a - m_a), jnp.exp2(sink_b - m_b))

    span = tm + CONV_PAD - SUBLANES
    for s in range(1, SUBLANES):
        ush[s, 0:span, :] = ucat[s:s + span, :]
    off = CONV_PAD - CONV_HIST
    rb = min(CONV_ROWS, tm)
    groups = rb // SUBLANES

    def conv_rows(lanes, r, carry):
        r0 = pl.multiple_of(r * rb, rb)
        partial = []
        for shift in range(SUBLANES):
            taps = [k for k in range(CONV_K) if (off + k) % SUBLANES == shift]
            top = max(off + k - shift for k in taps)
            slab = ush[shift, pl.ds(r0, rb + top), lanes].reshape(groups + top // SUBLANES, SUBLANES, CONV_LANES)
            acc = None
            for k in taps:
                g0 = (off + k - shift) // SUBLANES
                term = slab[g0:g0 + groups] * wdw_ref[k, :, lanes]
                acc = term if acc is None else acc + term
            partial.append(acc)
        while len(partial) > 1:
            partial = [a + b for a, b in zip(partial[0::2], partial[1::2])]
        yconv[pl.ds(r0, rb), lanes] = (partial[0] + bdw_ref[:, lanes]).reshape(rb, CONV_LANES)
        return carry

    for part in range(CONV_CH // CONV_LANES):
        lanes = slice(part * CONV_LANES, (part + 1) * CONV_LANES)
        lax.fori_loop(0, tm // rb, functools.partial(conv_rows, lanes), 0)
    y = _layer_norm_rows(yconv[...], clg_ref[...], clb_ref[...])
    mix_ref[0, :, ATTN_WIDTH:] = (y * jax.nn.sigmoid(y)).astype(BF16)

    for c in range(n_chunks):
        for j in range(n_pairs):
            res = jnp.dot(p_buf[c, j], window(vz, j, c), preferred_element_type=F32)
            out = res[:, 0:pair_w] * (1.0 / (res[:, pair_w:] + e_buf[c, j]))
            mix_ref[0, c * CHUNK:(c + 1) * CHUNK, j * pair_w:(j + 1) * pair_w] = out.astype(BF16)

    @pl.when(t < nt - 1)
    def _():
        if tm >= WINDOW:
            for i in range(2 * N_KV_HEADS):
                kz[i, 0:WINDOW, :] = kz[i, tm:tm + WINDOW, :]
                vz[i, 0:WINDOW, 0:KV_WIDTH] = vz[i, tm:tm + WINDOW, 0:KV_WIDTH]
            ucat[0:CONV_PAD, :] = ucat[tm:tm + CONV_PAD, :]


def _mixer(x, mod, w_in, bias, sink, w_dw, b_dw, cln_g, cln_b, k0, v0, u0, *, tm, masked):
    bsz, seq, d = x.shape
    assert seq % tm == 0 and tm % CHUNK == 0 and (tm >= WINDOW or seq == tm)
    nt = seq // tm
    kern = functools.partial(_mixer_kernel, tm=tm, masked=masked)
    const2 = lambda b, t: (0, 0)
    const3 = lambda b, t: (0, 0, 0)
    per_b = lambda b, t: (b, 0, 0)
    return pl.pallas_call(
        kern,
        grid=(bsz, nt),
        in_specs=[pl.BlockSpec((1, tm, d), lambda b, t: (b, t, 0)),
                  pl.BlockSpec((1, 6, d), per_b),
                  pl.BlockSpec((d, IN_WIDTH), const2),
                  pl.BlockSpec((WINDOW_CHUNKS + 1, N_HEADS // 2, CHUNK, 2 * KEYS), lambda b, t: (0, 0, 0, 0)),
                  pl.BlockSpec(memory_space=pltpu.SMEM),
                  pl.BlockSpec((CONV_K, SUBLANES, CONV_CH), const3),
                  pl.BlockSpec((1, CONV_CH), const2),
                  pl.BlockSpec((1, CONV_CH), const2),
                  pl.BlockSpec((1, CONV_CH), const2),
                  pl.BlockSpec((1, WINDOW, KV_WIDTH), per_b),
                  pl.BlockSpec((1, WINDOW, KV_WIDTH), per_b),
                  pl.BlockSpec((1, CONV_PAD, CONV_CH), per_b)],
        out_specs=[pl.BlockSpec((1, tm, MIX_WIDTH), lambda b, t: (b, t, 0)),
                   pl.BlockSpec((1, WINDOW, KV_WIDTH), per_b),
                   pl.BlockSpec((1, WINDOW, KV_WIDTH), per_b),
                   pl.BlockSpec((1, CONV_HIST, CONV_CH), per_b)],
        out_shape=[jax.ShapeDtypeStruct((bsz, seq, MIX_WIDTH), BF16),
                   jax.ShapeDtypeStruct((bsz, WINDOW, KV_WIDTH), F32),
                   jax.ShapeDtypeStruct((bsz, WINDOW, KV_WIDTH), F32),
                   jax.ShapeDtypeStruct((bsz, CONV_HIST, CONV_CH), F32)],
        scratch_shapes=[pltpu.VMEM((2 * N_KV_HEADS, WINDOW + tm, KV_WIDTH), BF16),
                        pltpu.VMEM((2 * N_KV_HEADS, WINDOW + tm, 2 * KV_WIDTH), BF16),
                        pltpu.VMEM((SUBLANES, CONV_PAD + tm, CONV_CH), F32),
                        pltpu.VMEM((tm, CONV_CH), F32),
                        pltpu.VMEM((tm // CHUNK, N_HEADS // 2, CHUNK, 2 * KEYS), F32),
                        pltpu.VMEM((tm // CHUNK, N_HEADS // 2, CHUNK, 2 * KEYS), BF16),
                        pltpu.VMEM((tm // CHUNK, N_HEADS // 2, CHUNK, 2 * HEAD_DIM), F32)],
        compiler_params=_cparams("parallel", "arbitrary"),
        name="mixer",
    )(x, mod, w_in, bias, sink, w_dw, b_dw, cln_g, cln_b, k0, v0, u0)


def _post_kernel(x_ref, mix_ref, mod_ref, wout_ref, g_ref, b_ref, wr_split_ref, wr_hi_ref, br_ref,
                 x1_ref, h2e_ref, *, bb, tt):
    rows = bb * tt
    mix = jnp.dot(mix_ref[...].reshape(rows, MIX_WIDTH), wout_ref[...], preferred_element_type=F32)
    g1 = mod_ref[:, 2:3, :]
    sh2 = mod_ref[:, 3:4, :]
    sc2 = mod_ref[:, 4:5, :]
    r = ALPHA * x_ref[...] + (1.0 + g1) * mix.reshape(bb, tt, D_MODEL)
    x1 = _layer_norm_rows(r, g_ref[...], b_ref[...])
    x1_ref[...] = x1
    h2 = (x1 * (1.0 + sc2) + sh2).reshape(rows, D_MODEL)
    h2_hi = h2.astype(BF16)
    h2e_ref[:, :, 0:D_MODEL] = h2_hi.reshape(bb, tt, D_MODEL)

    h2_lo = (h2 - h2_hi.astype(F32)).astype(BF16)
    hi_terms = jnp.dot(h2_hi, wr_split_ref[...], preferred_element_type=F32)
    logits = (hi_terms[:, 0:ROUTE_LANES] + hi_terms[:, ROUTE_LANES:]
              + jnp.dot(h2_lo, wr_hi_ref[...], preferred_element_type=F32) + br_ref[...])
    lane = lax.broadcasted_iota(jnp.int32, (rows, ROUTE_LANES), 1).astype(F32)
    far = float(ROUTE_LANES)
    is_group = (lane >= N_EXPERTS) & (lane < N_EXPERTS + N_GROUPS)
    gl = jnp.where(is_group, logits, NEG_INF)
    gmax = jnp.max(gl, -1, keepdims=True)
    gidx = jnp.min(jnp.where(is_group & (gl == gmax), lane, far), -1, keepdims=True) - N_EXPERTS
    pg = 1.0 / jnp.sum(jnp.where(is_group, jnp.exp(gl - gmax), 0.0), -1, keepdims=True)
    in_group = (lane >= gidx * EXPERTS_PER_GROUP) & (lane < (gidx + 1) * EXPERTS_PER_GROUP)
    el = jnp.where(in_group, logits, NEG_INF)
    v1 = jnp.max(el, -1, keepdims=True)
    i1 = jnp.min(jnp.where(in_group & (el == v1), lane, far), -1, keepdims=True)
    rest = in_group & (lane != i1)
    el2 = jnp.where(rest, logits, NEG_INF)
    v2 = jnp.max(el2, -1, keepdims=True)
    i2 = jnp.min(jnp.where(rest & (el2 == v2), lane, far), -1, keepdims=True)
    e2 = jnp.exp(v2 - v1)
    w1 = pg / (1.0 + e2)
    w2 = pg * e2 / (1.0 + e2)
    comb = jnp.where(lane == i1, w1, jnp.where(lane == i2, w2, 0.0))

    c_hi = comb.astype(BF16).astype(F32)
    c_lo = (comb - c_hi).astype(BF16).astype(F32)
    rec = c_hi + pltpu.roll(c_lo, COMB_LO_LANE, 1) + jnp.where(lane == gidx + GROUP_LANE, 1.0, 0.0)
    h2e_ref[:, :, D_MODEL:] = rec.astype(BF16).reshape(bb, tt, ROUTE_LANES)


def _moe_rows(rows):
    return rows + N_GROUPS * MOE_ALIGN + MOE_BLK


def _moe_experts(h2e_ref, tri_ref, wg_ref, wu_ref, wd_ref, xs_ref, cw_ref, ys_ref, scatter_ref, other_stream,
                 *, bb, tt):
    rows = bb * tt
    srows = _moe_rows(rows)
    ext = h2e_ref[...].reshape(rows, D_MODEL + ROUTE_LANES)
    rec = ext[:, D_MODEL:]
    cum = jnp.dot(tri_ref[...], rec, preferred_element_type=F32)
    lane1 = lax.broadcasted_iota(jnp.int32, (1, ROUTE_LANES), 1)
    cnt = cum[rows - 1:rows, :]
    off = jnp.int32(0)
    offs, counts = [], []
    off_vec = jnp.zeros((1, ROUTE_LANES), F32)
    for g in range(N_GROUPS):
        n_g = jnp.sum(jnp.where(lane1 == GROUP_LANE + g, cnt, 0.0)).astype(jnp.int32)
        offs.append(off)
        counts.append(n_g)
        off_vec = jnp.where(lane1 == GROUP_LANE + g, off.astype(F32), off_vec)
        off = off + ((n_g + (MOE_ALIGN - 1)) // MOE_ALIGN) * MOE_ALIGN

    lane = lax.broadcasted_iota(jnp.int32, (rows, ROUTE_LANES), 1)
    mine = (lane >= GROUP_LANE) & (lane < GROUP_LANE + N_GROUPS) & (rec.astype(F32) > 0.0)
    pos = jnp.sum(jnp.where(mine, cum - 1.0 + off_vec, 0.0), -1, keepdims=True)
    scatter_ref[...] = jnp.where(pos == lax.broadcasted_iota(jnp.int32, (rows, srows), 1).astype(F32),
                                 1.0, 0.0).astype(BF16)
    pos_row = jnp.transpose(jnp.broadcast_to(pos, (rows, ROUTE_LANES)))[0:1, :]
    grows = rows + N_GROUPS * MOE_ALIGN
    gather = jnp.where(lax.broadcasted_iota(jnp.int32, (grows, rows), 0).astype(F32) == pos_row, 1.0, 0.0).astype(BF16)
    xs_ref[0:grows, :] = jnp.dot(gather, ext[:, 0:D_MODEL], preferred_element_type=F32).astype(BF16)
    rs = jnp.dot(gather, rec, preferred_element_type=F32)
    cw_ref[0:grows, :] = rs + pltpu.roll(rs, ROUTE_LANES - COMB_LO_LANE, 1)

    def experts(g, r0):
        xb = xs_ref[pl.ds(r0, MOE_BLK), :]
        cwb = cw_ref[pl.ds(r0, MOE_BLK), :]
        parts = []
        for j in range(EXPERTS_PER_GROUP):
            e = g * EXPERTS_PER_GROUP + j
            hg = jnp.dot(xb, wg_ref[e], preferred_element_type=F32)
            hu = jnp.dot(xb, wu_ref[e], preferred_element_type=F32)
            parts.append((hg * jax.nn.sigmoid(hg) * hu * cwb[:, e:e + 1]).astype(BF16))
        act = jnp.concatenate(parts, axis=1)
        return jnp.dot(act, wd_ref[g], preferred_element_type=F32)

    for g in range(N_GROUPS):
        r0 = pl.multiple_of(offs[g], MOE_ALIGN)
        ys_ref[pl.ds(r0, MOE_BLK), :] = experts(g, r0).astype(BF16)
        if g == 1:
            other_stream()

    def extra_block(g, i, carry):
        r0 = pl.multiple_of(offs[g] + i * MOE_BLK, MOE_ALIGN)
        row = r0 + lax.broadcasted_iota(jnp.int32, (MOE_BLK, 1), 0)
        old = ys_ref[pl.ds(r0, MOE_BLK), :].astype(F32)
        ys_ref[pl.ds(r0, MOE_BLK), :] = jnp.where(row < offs[g] + counts[g], experts(g, r0), old).astype(BF16)
        return carry

    def extra_blocks():
        for g in range(N_GROUPS):
            lax.fori_loop(1, (counts[g] + (MOE_BLK - 1)) // MOE_BLK, functools.partial(extra_block, g), 0)

    return extra_blocks


def _moe_finish(x1_ref, mod_ref, g_ref, b_ref, ys_ref, scatter_ref, y_ref, *, bb, tt):
    ff = jnp.dot(scatter_ref[...], ys_ref[...], preferred_element_type=F32)
    g2 = mod_ref[:, 5:6, :]
    r = ALPHA * x1_ref[...] + (1.0 + g2) * ff.reshape(bb, tt, D_MODEL)
    y_ref[...] = _layer_norm_rows(r, g_ref[...], b_ref[...])


def _row_parts(bb, tt, n):
    if bb == 1:
        pt = tt // n
        return [((slice(None), slice(i * pt, (i + 1) * pt)), 1, pt) for i in range(n)]
    pb = bb // n
    return [((slice(i * pb, (i + 1) * pb), slice(None)), pb, tt) for i in range(n)]


def _ffn_kernel(x_ref, mix_ref, mod_cur_ref, mod_prev_ref, wout_ref, g1_ref, b1_ref, wr_split_ref, wr_hi_ref,
                br_ref, tri_ref, wg_ref, wu_ref, wd_ref, g2_ref, b2_ref, y_ref,
                x1_buf, h2e_buf, xs_ref, cw_ref, ys_ref, scatter_ref, *, bb, tt):
    s = pl.program_id(0)
    cur = s % 2
    prev = 1 - cur

    @pl.when(s == 0)
    def _():
        ys_ref[...] = jnp.zeros_like(ys_ref)
        xs_ref[...] = jnp.zeros_like(xs_ref)
        cw_ref[...] = jnp.zeros_like(cw_ref)
        x1_buf[1] = jnp.zeros(x1_buf.shape[1:], F32)
        h2e_buf[...] = jnp.zeros(h2e_buf.shape, BF16)

    post = functools.partial(_post_kernel, x_ref, mix_ref, mod_cur_ref, wout_ref, g1_ref, b1_ref, wr_split_ref,
                             wr_hi_ref, br_ref, x1_buf.at[cur], h2e_buf, bb=bb, tt=tt)
    extra_blocks = _moe_experts(h2e_buf, tri_ref, wg_ref, wu_ref, wd_ref, xs_ref, cw_ref, ys_ref,
                                scatter_ref, post, bb=bb, tt=tt)
    extra_blocks()
    for i, (sl, pb, pt) in enumerate(_row_parts(bb, tt, TILE_PARTS)):
        part_rows = slice(i * pb * pt, (i + 1) * pb * pt)
        _moe_finish(x1_buf.at[prev].at[sl], mod_prev_ref.at[sl[0]], g2_ref, b2_ref, ys_ref, scatter_ref.at[part_rows],
                    y_ref.at[sl], bb=pb, tt=pt)


def _ffn(x, mixin, mod, p, *, bb, tt):
    bsz, seq, d = x.shape
    assert bsz % bb == 0 and seq % tt == 0
    rows = bb * tt
    srows = _moe_rows(rows)
    n_j = seq // tt
    n_tiles = (bsz // bb) * n_j
    tri = jnp.tril(jnp.ones((rows, rows), BF16))
    kern = functools.partial(_ffn_kernel, bb=bb, tt=tt)
    cur = lambda s: jnp.minimum(s, n_tiles - 1)
    prev = lambda s: jnp.maximum(s - 1, 0)
    const2 = lambda s: (0, 0)
    const3 = lambda s: (0, 0, 0)
    resident = pl.Buffered(1)
    return pl.pallas_call(
        kern,
        grid=(n_tiles + 1,),
        in_specs=[pl.BlockSpec((bb, tt, d), lambda s: (cur(s) // n_j, cur(s) % n_j, 0)),
                  pl.BlockSpec((bb, tt, MIX_WIDTH), lambda s: (cur(s) // n_j, cur(s) % n_j, 0)),
                  pl.BlockSpec((bb, 6, d), lambda s: (cur(s) // n_j, 0, 0)),
                  pl.BlockSpec((bb, 6, d), lambda s: (prev(s) // n_j, 0, 0)),
                  pl.BlockSpec((MIX_WIDTH, d), const2, pipeline_mode=resident),
                  pl.BlockSpec((1, d), const2),
                  pl.BlockSpec((1, d), const2),
                  pl.BlockSpec((d, 2 * ROUTE_LANES), const2, pipeline_mode=resident),
                  pl.BlockSpec((d, ROUTE_LANES), const2, pipeline_mode=resident),
                  pl.BlockSpec((1, ROUTE_LANES), const2),
                  pl.BlockSpec((rows, rows), const2, pipeline_mode=resident),
                  pl.BlockSpec(p["w_gate"].shape, const3, pipeline_mode=resident),
                  pl.BlockSpec(p["w_up"].shape, const3, pipeline_mode=resident),
                  pl.BlockSpec(p["w_down"].shape, const3, pipeline_mode=resident),
                  pl.BlockSpec((1, d), const2),
                  pl.BlockSpec((1, d), const2)],
        out_specs=pl.BlockSpec((bb, tt, d), lambda s: (prev(s) // n_j, prev(s) % n_j, 0)),
        out_shape=jax.ShapeDtypeStruct((bsz, seq, d), F32),
        scratch_shapes=[pltpu.VMEM((2, bb, tt, d), F32),
                        pltpu.VMEM((bb, tt, d + ROUTE_LANES), BF16),
                        pltpu.VMEM((srows, d), BF16),
                        pltpu.VMEM((srows, ROUTE_LANES), F32),
                        pltpu.VMEM((srows, d), BF16),
                        pltpu.VMEM((rows, srows), BF16)],
        compiler_params=_cparams("arbitrary"),
        name="ffn",
    )(x, mixin, mod, mod, p["w_out"], p["ln1_g"], p["ln1_b"], p["w_route_split"], p["w_route_hi"], p["b_route"],
      tri, p["w_gate"], p["w_up"], p["w_down"], p["ln2_g"], p["ln2_b"])


def _stream(x, mod, k0, v0, u0, p, *, mixer_tm, masked, bb, tt):
    mixin, nk, nv, nu = _mixer(x, mod, p["w_in"], p["bias"], p["sink"], p["w_dw"], p["b_dw"],
                               p["cln_g"], p["cln_b"], k0, v0, u0, tm=mixer_tm, masked=masked)
    y = _ffn(x, mixin, mod, p, bb=bb, tt=tt)
    bsz = x.shape[0]
    cache_shape = (1, bsz, WINDOW, N_KV_HEADS, HEAD_DIM)
    return y, nk.reshape(cache_shape), nv.reshape(cache_shape), nu[None]


def kernel(x_prompt, x_sample, cache_attn_k, cache_attn_v, state_conv, c_prompt, c_sample, rel_bias, w_ada, b_ada, w_in, attn_sinks, w_dw, b_dw, conv_ln_g, conv_ln_b, w_out, ln1_g, ln1_b, w_group, b_group, w_erouter, b_erouter, w_gate, w_up, w_down, ln2_g, ln2_b):
    bp = x_prompt.shape[0]
    bs = x_sample.shape[0]
    mod = _modulation(jnp.concatenate([c_prompt, c_sample], 0), w_ada[0], b_ada[0])
    mod = mod.reshape(bp + bs, 6, D_MODEL)

    w_route = jnp.concatenate([w_erouter[0].reshape(D_MODEL, N_EXPERTS), w_group[0]], 1)
    w_route = jnp.pad(w_route, ((0, 0), (0, ROUTE_LANES - N_EXPERTS - N_GROUPS)))
    b_route = jnp.concatenate([b_erouter[0].reshape(N_EXPERTS), b_group[0]])
    b_route = jnp.pad(b_route, (0, ROUTE_LANES - N_EXPERTS - N_GROUPS)).reshape(1, ROUTE_LANES)
    w_route_hi = w_route.astype(BF16)
    w_route_lo = (w_route - w_route_hi.astype(F32)).astype(BF16)

    p = {
        "w_in": w_in[0].astype(BF16),
        "bias": _relative_bias(rel_bias),
        "sink": attn_sinks[0],
        "w_dw": jnp.broadcast_to(w_dw[0][:, None, :], (CONV_K, SUBLANES, CONV_CH)),
        "b_dw": b_dw[0].reshape(1, CONV_CH),
        "cln_g": conv_ln_g[0].reshape(1, CONV_CH), "cln_b": conv_ln_b[0].reshape(1, CONV_CH),
        "w_out": w_out[0].astype(BF16),
        "ln1_g": ln1_g[0].reshape(1, D_MODEL), "ln1_b": ln1_b[0].reshape(1, D_MODEL),
        "w_route_split": jnp.concatenate([w_route_hi, w_route_lo], 1), "w_route_hi": w_route_hi,
        "b_route": b_route,
        "w_gate": w_gate[0].astype(BF16), "w_up": w_up[0].astype(BF16),
        "w_down": w_down[0].astype(BF16).reshape(N_GROUPS, EXPERTS_PER_GROUP * D_EXPERT, D_MODEL),
        "ln2_g": ln2_g[0].reshape(1, D_MODEL), "ln2_b": ln2_b[0].reshape(1, D_MODEL),
    }

    zk = jnp.zeros((bp, WINDOW, KV_WIDTH), F32)
    zu = jnp.zeros((bp, CONV_PAD, CONV_CH), F32)
    yp, pk, pv, pc = _stream(x_prompt, mod[:bp], zk, zk, zu, p, mixer_tm=512, masked=True, bb=1, tt=512)

    k0 = cache_attn_k[0].reshape(bs, WINDOW, KV_WIDTH)
    v0 = cache_attn_v[0].reshape(bs, WINDOW, KV_WIDTH)
    u0 = jnp.pad(state_conv[0], ((0, 0), (CONV_PAD - CONV_HIST, 0), (0, 0)))
    ts = x_sample.shape[1]
    ys, sk, sv, sc = _stream(x_sample, mod[bp:], k0, v0, u0, p, mixer_tm=ts, masked=False, bb=8, tt=ts)
    return yp, ys, pk, pv, pc, sk, sv, sc
```

```python
import functools
import math

import jax
import jax.numpy as jnp
from jax import lax
from jax.experimental import pallas as pl
from jax.experimental.pallas import tpu as pltpu

D_MODEL = 1024
CHUNK = 64
N_HEADS = 8
N_KV_HEADS = 2
HEAD_DIM = 64
Q_GROUP = N_HEADS // N_KV_HEADS
ATTN_WIDTH = N_HEADS * HEAD_DIM
KV_WIDTH = N_KV_HEADS * HEAD_DIM
WINDOW = 128
WINDOW_CHUNKS = WINDOW // CHUNK
KEYS = WINDOW + CHUNK
CONV_CH = D_MODEL // 2
CONV_K = 31
CONV_HIST = CONV_K - 1
CONV_PAD = 32
SUBLANES = 8
CONV_ROWS = 256
CONV_LANES = 128
MIX_WIDTH = ATTN_WIDTH + CONV_CH
IN_WIDTH = ATTN_WIDTH + 2 * KV_WIDTH + 2 * CONV_CH
N_BUCKETS = 32
MAX_DISTANCE = 128
N_GROUPS = 4
EXPERTS_PER_GROUP = 4
N_EXPERTS = N_GROUPS * EXPERTS_PER_GROUP
D_EXPERT = D_MODEL // 4
DEPTH = 1
ALPHA = (2 * DEPTH) ** 0.25
LN_EPS = 1e-5
NEG_INF = -1e30
LOG2E = math.log2(math.e)
ROUTE_LANES = 128
COMB_LO_LANE = 16
GROUP_LANE = 32
MOE_BLK = 144
MOE_ALIGN = 16
VMEM_LIMIT = 56 * 1024 * 1024

BF16 = jnp.bfloat16
F32 = jnp.float32


def _cparams(*sem):
    return pltpu.CompilerParams(dimension_semantics=sem, vmem_limit_bytes=VMEM_LIMIT)


def _mod_kernel(c_ref, w_ref, b_ref, o_ref):
    c = c_ref[...]
    s = c * jax.nn.sigmoid(c)
    o_ref[...] = jnp.dot(s, w_ref[...], preferred_element_type=F32,
                         precision=lax.Precision.HIGHEST) + b_ref[...]


def _modulation(c, w_ada, b_ada):
    n, d = c.shape
    width = w_ada.shape[1]
    tn = 1536
    return pl.pallas_call(
        _mod_kernel,
        grid=(width // tn,),
        in_specs=[pl.BlockSpec((n, d), lambda j: (0, 0)),
                  pl.BlockSpec((d, tn), lambda j: (0, j)),
                  pl.BlockSpec((1, tn), lambda j: (0, j))],
        out_specs=pl.BlockSpec((n, tn), lambda j: (0, j)),
        out_shape=jax.ShapeDtypeStruct((n, width), F32),
        compiler_params=_cparams("arbitrary"),
        name="modulation",
    )(c, w_ada, b_ada.reshape(1, width))


def _t5_bucket(rel):
    nb = N_BUCKETS // 2
    max_exact = nb // 2
    ret = jnp.where(rel > 0, nb, 0)
    n = jnp.abs(rel)
    large = max_exact + (jnp.log(jnp.maximum(n, 1).astype(jnp.float32) / max_exact)
                         / math.log(MAX_DISTANCE / max_exact) * (nb - max_exact)).astype(jnp.int32)
    large = jnp.minimum(large, nb - 1)
    return ret + jnp.where(n < max_exact, n, large)


def _bias_kernel(table_ref, bucket_ref, o_ref):
    bucket = bucket_ref[...]
    col = lax.broadcasted_iota(jnp.int32, bucket.shape, 1)
    for h in range(N_HEADS):
        acc = jnp.zeros(bucket.shape, F32)
        for b in range(N_BUCKETS):
            acc = jnp.where(bucket == b, table_ref[b, h], acc)
        for v in range(WINDOW_CHUNKS + 1):
            o_ref[v, h] = jnp.where(col >= v * CHUNK, acc * LOG2E, NEG_INF)


def _relative_bias(table):
    rel = jnp.arange(KEYS)[None, :] - WINDOW - jnp.arange(CHUNK)[:, None]
    bucket = _t5_bucket(rel).astype(jnp.int32)
    nv = WINDOW_CHUNKS + 1
    bias = pl.pallas_call(
        _bias_kernel,
        in_specs=[pl.BlockSpec(memory_space=pltpu.SMEM),
                  pl.BlockSpec((CHUNK, KEYS), lambda: (0, 0))],
        out_specs=pl.BlockSpec((nv, N_HEADS, CHUNK, KEYS), lambda: (0, 0, 0, 0)),
        out_shape=jax.ShapeDtypeStruct((nv, N_HEADS, CHUNK, KEYS), F32),
        name="relative_bias",
    )(table, bucket)
    bias = bias.reshape(nv, N_HEADS // 2, 2, CHUNK, KEYS)
    return jnp.transpose(bias, (0, 1, 3, 2, 4)).reshape(nv, N_HEADS // 2, CHUNK, 2 * KEYS)


def _layer_norm_rows(x, g, b):
    mu = jnp.mean(x, -1, keepdims=True)
    xc = x - mu
    var = jnp.mean(xc * xc, -1, keepdims=True)
    return xc * lax.rsqrt(var + LN_EPS) * g + b


def _mixer_kernel(x_ref, mod_ref, win_ref, bias_ref, sink_ref, wdw_ref, bdw_ref, clg_ref, clb_ref,
                  k0_ref, v0_ref, u0_ref,
                  mix_ref, nk_ref, nv_ref, nu_ref,
                  kz, vz, ush, yconv, s_buf, p_buf, e_buf, *, tm, masked):
    t = pl.program_id(1)
    nt = pl.num_programs(1)
    n_chunks = tm // CHUNK
    ucat = ush.at[0]

    left = lax.broadcasted_iota(jnp.int32, (1, KV_WIDTH), 1) < HEAD_DIM

    def store_kv(row0, k, v):
        n = k.shape[0]
        for dst, val in ((kz, k), (vz, v)):
            swapped = pltpu.roll(val, HEAD_DIM, 1)
            dst[0, row0:row0 + n, 0:KV_WIDTH] = jnp.where(left, val, 0.0).astype(BF16)
            dst[1, row0:row0 + n, 0:KV_WIDTH] = jnp.where(left, 0.0, swapped).astype(BF16)
            dst[2, row0:row0 + n, 0:KV_WIDTH] = jnp.where(left, swapped, 0.0).astype(BF16)
            dst[3, row0:row0 + n, 0:KV_WIDTH] = jnp.where(left, 0.0, val).astype(BF16)

    @pl.when(t == 0)
    def _():
        ones_left = jnp.broadcast_to(jnp.where(left, 1.0, 0.0).astype(BF16), (WINDOW + tm, KV_WIDTH))
        ones_right = jnp.broadcast_to(jnp.where(left, 0.0, 1.0).astype(BF16), (WINDOW + tm, KV_WIDTH))
        for i in range(2 * N_KV_HEADS):
            vz[i, :, KV_WIDTH:] = ones_left if i % 2 == 0 else ones_right
        store_kv(0, k0_ref[0], v0_ref[0])
        ucat[0:CONV_PAD, :] = u0_ref[0]

    sh1 = mod_ref[0, 0:1, :]
    sc1 = mod_ref[0, 1:2, :]
    h = (x_ref[0] * (1.0 + sc1) + sh1).astype(BF16)
    o1 = ATTN_WIDTH
    o3 = o1 + 2 * KV_WIDTH
    q = (jnp.dot(h, win_ref[:, 0:o1], preferred_element_type=F32) * (HEAD_DIM ** -0.5 * LOG2E)).astype(BF16)
    kv = jnp.dot(h, win_ref[:, o1:o3], preferred_element_type=F32)
    ag = jnp.dot(h, win_ref[:, o3:IN_WIDTH], preferred_element_type=F32)
    u = ag[:, 0:CONV_CH] * jax.nn.sigmoid(ag[:, CONV_CH:])
    store_kv(WINDOW, kv[:, 0:KV_WIDTH], kv[:, KV_WIDTH:])
    ucat[CONV_PAD:CONV_PAD + tm, :] = u

    if tm >= WINDOW:
        nk_ref[0] = kv[tm - WINDOW:, 0:KV_WIDTH]
        nv_ref[0] = kv[tm - WINDOW:, KV_WIDTH:]
    else:
        nk_ref[0, 0:WINDOW - tm, :] = k0_ref[0, tm:WINDOW, :]
        nv_ref[0, 0:WINDOW - tm, :] = v0_ref[0, tm:WINDOW, :]
        nk_ref[0, WINDOW - tm:, :] = kv[:, 0:KV_WIDTH]
        nv_ref[0, WINDOW - tm:, :] = kv[:, KV_WIDTH:]
    nu_ref[0] = ucat[CONV_PAD + tm - CONV_HIST:CONV_PAD + tm, :]

    n_pairs = N_HEADS // 2
    pair_w = 2 * HEAD_DIM
    first_head = lax.broadcasted_iota(jnp.int32, (CHUNK, pair_w), 1) < HEAD_DIM
    nt_dims = (((1,), (1,)), ((), ()))

    def window(ref, j, c):
        kvh = (2 * j) // Q_GROUP
        rows = slice(c * CHUNK, c * CHUNK + KEYS)
        return jnp.concatenate([ref[2 * kvh, rows, :], ref[2 * kvh + 1, rows, :]], axis=0)

    for c in range(n_chunks):
        if masked and c < WINDOW_CHUNKS:
            variant = jnp.where(t == 0, WINDOW_CHUNKS - c, 0)
        else:
            variant = 0
        for j in range(n_pairs):
            qp = q[c * CHUNK:(c + 1) * CHUNK, j * pair_w:(j + 1) * pair_w]
            s_buf[c, j] = (lax.dot_general(qp, window(kz, j, c), nt_dims, preferred_element_type=F32)
                           + bias_ref[variant, j])

    for c in range(n_chunks):
        for j in range(n_pairs):
            s0 = s_buf[c, j, :, 0:pair_w]
            s1 = s_buf[c, j, :, pair_w:2 * pair_w]
            s2 = s_buf[c, j, :, 2 * pair_w:]
            sink_a = sink_ref[2 * j] * LOG2E
            sink_b = sink_ref[2 * j + 1] * LOG2E
            m_a = jnp.maximum(jnp.max(jnp.maximum(s0, jnp.where(first_head, s1, NEG_INF)), -1, keepdims=True), sink_a)
            m_b = jnp.maximum(jnp.max(jnp.maximum(jnp.where(first_head, NEG_INF, s1), s2), -1, keepdims=True), sink_b)
            p_buf[c, j, :, 0:pair_w] = jnp.exp2(s0 - m_a).astype(BF16)
            p_buf[c, j, :, pair_w:2 * pair_w] = jnp.exp2(s1 - jnp.where(first_head, m_a, m_b)).astype(BF16)
            p_buf[c, j, :, 2 * pair_w:] = jnp.exp2(s2 - m_b).astype(BF16)
            e_buf[c, j] = jnp.where(first_head, jnp.exp2(sink_a - m_a), jnp.exp2(sink_b - m_b))

    span = tm + CONV_PAD - SUBLANES
    for s in range(1, SUBLANES):
        ush[s, 0:span, :] = ucat[s:s + span, :]
    off = CONV_PAD - CONV_HIST
    rb = min(CONV_ROWS, tm)
    groups = rb // SUBLANES

    def conv_rows(lanes, r, carry):
        r0 = pl.multiple_of(r * rb, rb)
        partial = []
        for shift in range(SUBLANES):
            taps = [k for k in range(CONV_K) if (off + k) % SUBLANES == shift]
            top = max(off + k - shift for k in taps)
            slab = ush[shift, pl.ds(r0, rb + top), lanes].reshape(groups + top // SUBLANES, SUBLANES, CONV_LANES)
            acc = None
            for k in taps:
                g0 = (off + k - shift) // SUBLANES
                term = slab[g0:g0 + groups] * wdw_ref[k, :, lanes]
                acc = term if acc is None else acc + term
            partial.append(acc)
        while len(partial) > 1:
            partial = [a + b for a, b in zip(partial[0::2], partial[1::2])]
        yconv[pl.ds(r0, rb), lanes] = (partial[0] + bdw_ref[:, lanes]).reshape(rb, CONV_LANES)
        return carry

    for part in range(CONV_CH // CONV_LANES):
        lanes = slice(part * CONV_LANES, (part + 1) * CONV_LANES)
        lax.fori_loop(0, tm // rb, functools.partial(conv_rows, lanes), 0)
    y = _layer_norm_rows(yconv[...], clg_ref[...], clb_ref[...])
    mix_ref[0, :, ATTN_WIDTH:] = (y * jax.nn.sigmoid(y)).astype(BF16)

    for c in range(n_chunks):
        for j in range(n_pairs):
            res = jnp.dot(p_buf[c, j], window(vz, j, c), preferred_element_type=F32)
            out = res[:, 0:pair_w] * (1.0 / (res[:, pair_w:] + e_buf[c, j]))
            mix_ref[0, c * CHUNK:(c + 1) * CHUNK, j * pair_w:(j + 1) * pair_w] = out.astype(BF16)

    @pl.when(t < nt - 1)
    def _():
        if tm >= WINDOW:
            for i in range(2 * N_KV_HEADS):
                kz[i, 0:WINDOW, :] = kz[i, tm:tm + WINDOW, :]
                vz[i, 0:WINDOW, 0:KV_WIDTH] = vz[i, tm:tm + WINDOW, 0:KV_WIDTH]
            ucat[0:CONV_PAD, :] = ucat[tm:tm + CONV_PAD, :]


def _mixer(x, mod, w_in, bias, sink, w_dw, b_dw, cln_g, cln_b, k0, v0, u0, *, tm, masked):
    bsz, seq, d = x.shape
    assert seq % tm == 0 and tm % CHUNK == 0 and (tm >= WINDOW or seq == tm)
    nt = seq // tm
    kern = functools.partial(_mixer_kernel, tm=tm, masked=masked)
    const2 = lambda b, t: (0, 0)
    const3 = lambda b, t: (0, 0, 0)
    per_b = lambda b, t: (b, 0, 0)
    return pl.pallas_call(
        kern,
        grid=(bsz, nt),
        in_specs=[pl.BlockSpec((1, tm, d), lambda b, t: (b, t, 0)),
                  pl.BlockSpec((1, 6, d), per_b),
                  pl.BlockSpec((d, IN_WIDTH), const2),
                  pl.BlockSpec((WINDOW_CHUNKS + 1, N_HEADS // 2, CHUNK, 2 * KEYS), lambda b, t: (0, 0, 0, 0)),
                  pl.BlockSpec(memory_space=pltpu.SMEM),
                  pl.BlockSpec((CONV_K, SUBLANES, CONV_CH), const3),
                  pl.BlockSpec((1, CONV_CH), const2),
                  pl.BlockSpec((1, CONV_CH), const2),
                  pl.BlockSpec((1, CONV_CH), const2),
                  pl.BlockSpec((1, WINDOW, KV_WIDTH), per_b),
                  pl.BlockSpec((1, WINDOW, KV_WIDTH), per_b),
                  pl.BlockSpec((1, CONV_PAD, CONV_CH), per_b)],
        out_specs=[pl.BlockSpec((1, tm, MIX_WIDTH), lambda b, t: (b, t, 0)),
                   pl.BlockSpec((1, WINDOW, KV_WIDTH), per_b),
                   pl.BlockSpec((1, WINDOW, KV_WIDTH), per_b),
                   pl.BlockSpec((1, CONV_HIST, CONV_CH), per_b)],
        out_shape=[jax.ShapeDtypeStruct((bsz, seq, MIX_WIDTH), BF16),
                   jax.ShapeDtypeStruct((bsz, WINDOW, KV_WIDTH), F32),
                   jax.ShapeDtypeStruct((bsz, WINDOW, KV_WIDTH), F32),
                   jax.ShapeDtypeStruct((bsz, CONV_HIST, CONV_CH), F32)],
        scratch_shapes=[pltpu.VMEM((2 * N_KV_HEADS, WINDOW + tm, KV_WIDTH), BF16),
                        pltpu.VMEM((2 * N_KV_HEADS, WINDOW + tm, 2 * KV_WIDTH), BF16),
                        pltpu.VMEM((SUBLANES, CONV_PAD + tm, CONV_CH), F32),
                        pltpu.VMEM((tm, CONV_CH), F32),
                        pltpu.VMEM((tm // CHUNK, N_HEADS // 2, CHUNK, 2 * KEYS), F32),
                        pltpu.VMEM((tm // CHUNK, N_HEADS // 2, CHUNK, 2 * KEYS), BF16),
                        pltpu.VMEM((tm // CHUNK, N_HEADS // 2, CHUNK, 2 * HEAD_DIM), F32)],
        compiler_params=_cparams("parallel", "arbitrary"),
        name="mixer",
    )(x, mod, w_in, bias, sink, w_dw, b_dw, cln_g, cln_b, k0, v0, u0)


def _post_kernel(x_ref, mix_ref, mod_ref, wout_ref, g_ref, b_ref, wr_split_ref, wr_hi_ref, br_ref,
                 x1_ref, h2e_ref, *, bb, tt):
    rows = bb * tt
    mix = jnp.dot(mix_ref[...].reshape(rows, MIX_WIDTH), wout_ref[...], preferred_element_type=F32)
    g1 = mod_ref[:, 2:3, :]
    sh2 = mod_ref[:, 3:4, :]
    sc2 = mod_ref[:, 4:5, :]
    r = ALPHA * x_ref[...] + (1.0 + g1) * mix.reshape(bb, tt, D_MODEL)
    x1 = _layer_norm_rows(r, g_ref[...], b_ref[...])
    x1_ref[...] = x1
    h2 = (x1 * (1.0 + sc2) + sh2).reshape(rows, D_MODEL)
    h2_hi = h2.astype(BF16)
    h2e_ref[:, :, 0:D_MODEL] = h2_hi.reshape(bb, tt, D_MODEL)

    h2_lo = (h2 - h2_hi.astype(F32)).astype(BF16)
    hi_terms = jnp.dot(h2_hi, wr_split_ref[...], preferred_element_type=F32)
    logits = (hi_terms[:, 0:ROUTE_LANES] + hi_terms[:, ROUTE_LANES:]
              + jnp.dot(h2_lo, wr_hi_ref[...], preferred_element_type=F32) + br_ref[...])
    lane = lax.broadcasted_iota(jnp.int32, (rows, ROUTE_LANES), 1).astype(F32)
    far = float(ROUTE_LANES)
    is_group = (lane >= N_EXPERTS) & (lane < N_EXPERTS + N_GROUPS)
    gl = jnp.where(is_group, logits, NEG_INF)
    gmax = jnp.max(gl, -1, keepdims=True)
    gidx = jnp.min(jnp.where(is_group & (gl == gmax), lane, far), -1, keepdims=True) - N_EXPERTS
    pg = 1.0 / jnp.sum(jnp.where(is_group, jnp.exp(gl - gmax), 0.0), -1, keepdims=True)
    in_group = (lane >= gidx * EXPERTS_PER_GROUP) & (lane < (gidx + 1) * EXPERTS_PER_GROUP)
    el = jnp.where(in_group, logits, NEG_INF)
    v1 = jnp.max(el, -1, keepdims=True)
    i1 = jnp.min(jnp.where(in_group & (el == v1), lane, far), -1, keepdims=True)
    rest = in_group & (lane != i1)
    el2 = jnp.where(rest, logits, NEG_INF)
    v2 = jnp.max(el2, -1, keepdims=True)
    i2 = jnp.min(jnp.where(rest & (el2 == v2), lane, far), -1, keepdims=True)
    e2 = jnp.exp(v2 - v1)
    w1 = pg / (1.0 + e2)
    w2 = pg * e2 / (1.0 + e2)
    comb = jnp.where(lane == i1, w1, jnp.where(lane == i2, w2, 0.0))

    c_hi = comb.astype(BF16).astype(F32)
    c_lo = (comb - c_hi).astype(BF16).astype(F32)
    rec = c_hi + pltpu.roll(c_lo, COMB_LO_LANE, 1) + jnp.where(lane == gidx + GROUP_LANE, 1.0, 0.0)
    h2e_ref[:, :, D_MODEL:] = rec.astype(BF16).reshape(bb, tt, ROUTE_LANES)


def _moe_rows(rows):
    return rows + N_GROUPS * MOE_ALIGN + MOE_BLK


def _moe_experts(h2e_ref, tri_ref, wg_ref, wu_ref, wd_ref, xs_ref, cw_ref, ys_ref, scatter_ref, other_stream,
                 *, bb, tt):
    rows = bb * tt
    srows = _moe_rows(rows)
    ext = h2e_ref[...].reshape(rows, D_MODEL + ROUTE_LANES)
    rec = ext[:, D_MODEL:]
    cum = jnp.dot(tri_ref[...], rec, preferred_element_type=F32)
    lane1 = lax.broadcasted_iota(jnp.int32, (1, ROUTE_LANES), 1)
    cnt = cum[rows - 1:rows, :]
    off = jnp.int32(0)
    offs, counts = [], []
    off_vec = jnp.zeros((1, ROUTE_LANES), F32)
    for g in range(N_GROUPS):
        n_g = jnp.sum(jnp.where(lane1 == GROUP_LANE + g, cnt, 0.0)).astype(jnp.int32)
        offs.append(off)
        counts.append(n_g)
        off_vec = jnp.where(lane1 == GROUP_LANE + g, off.astype(F32), off_vec)
        off = off + ((n_g + (MOE_ALIGN - 1)) // MOE_ALIGN) * MOE_ALIGN

    lane = lax.broadcasted_iota(jnp.int32, (rows, ROUTE_LANES), 1)
    mine = (lane >= GROUP_LANE) & (lane < GROUP_LANE + N_GROUPS) & (rec.astype(F32) > 0.0)
    pos = jnp.sum(jnp.where(mine, cum - 1.0 + off_vec, 0.0), -1, keepdims=True)
    scatter_ref[...] = jnp.where(pos == lax.broadcasted_iota(jnp.int32, (rows, srows), 1).astype(F32),
                                 1.0, 0.0).astype(BF16)
    pos_row = jnp.transpose(jnp.broadcast_to(pos, (rows, ROUTE_LANES)))[0:1, :]
    grows = rows + N_GROUPS * MOE_ALIGN
    gather = jnp.where(lax.broadcasted_iota(jnp.int32, (grows, rows), 0).astype(F32) == pos_row, 1.0, 0.0).astype(BF16)
    xs_ref[0:grows, :] = jnp.dot(gather, ext[:, 0:D_MODEL], preferred_element_type=F32).astype(BF16)
    rs = jnp.dot(gather, rec, preferred_element_type=F32)
    cw_ref[0:grows, :] = rs + pltpu.roll(rs, ROUTE_LANES - COMB_LO_LANE, 1)

    def experts(g, r0):
        xb = xs_ref[pl.ds(r0, MOE_BLK), :]
        cwb = cw_ref[pl.ds(r0, MOE_BLK), :]
        parts = []
        for j in range(EXPERTS_PER_GROUP):
            e = g * EXPERTS_PER_GROUP + j
            hg = jnp.dot(xb, wg_ref[e], preferred_element_type=F32)
            hu = jnp.dot(xb, wu_ref[e], preferred_element_type=F32)
            parts.append((hg * jax.nn.sigmoid(hg) * hu * cwb[:, e:e + 1]).astype(BF16))
        act = jnp.concatenate(parts, axis=1)
        return jnp.dot(act, wd_ref[g], preferred_element_type=F32)

    for g in range(N_GROUPS):
        r0 = pl.multiple_of(offs[g], MOE_ALIGN)
        ys_ref[pl.ds(r0, MOE_BLK), :] = experts(g, r0).astype(BF16)
        if g == 1:
            other_stream()

    def extra_block(g, i, carry):
        r0 = pl.multiple_of(offs[g] + i * MOE_BLK, MOE_ALIGN)
        row = r0 + lax.broadcasted_iota(jnp.int32, (MOE_BLK, 1), 0)
        old = ys_ref[pl.ds(r0, MOE_BLK), :].astype(F32)
        ys_ref[pl.ds(r0, MOE_BLK), :] = jnp.where(row < offs[g] + counts[g], experts(g, r0), old).astype(BF16)
        return carry

    def extra_blocks():
        for g in range(N_GROUPS):
            lax.fori_loop(1, (counts[g] + (MOE_BLK - 1)) // MOE_BLK, functools.partial(extra_block, g), 0)

    return extra_blocks


def _moe_finish(x1_ref, mod_ref, g_ref, b_ref, ys_ref, scatter_ref, y_ref, *, bb, tt):
    ff = jnp.dot(scatter_ref[...], ys_ref[...], preferred_element_type=F32)
    g2 = mod_ref[:, 5:6, :]
    r = ALPHA * x1_ref[...] + (1.0 + g2) * ff.reshape(bb, tt, D_MODEL)
    y_ref[...] = _layer_norm_rows(r, g_ref[...], b_ref[...])


def _ffn_kernel(x_ref, mix_ref, mod_cur_ref, mod_prev_ref, wout_ref, g1_ref, b1_ref, wr_split_ref, wr_hi_ref,
                br_ref, tri_ref, wg_ref, wu_ref, wd_ref, g2_ref, b2_ref, y_ref,
                x1_buf, h2e_buf, xs_ref, cw_ref, ys_ref, scatter_ref, *, bb, tt):
    s = pl.program_id(0)
    cur = s % 2
    prev = 1 - cur

    @pl.when(s == 0)
    def _():
        ys_ref[...] = jnp.zeros_like(ys_ref)
        xs_ref[...] = jnp.zeros_like(xs_ref)
        cw_ref[...] = jnp.zeros_like(cw_ref)
        x1_buf[1] = jnp.zeros(x1_buf.shape[1:], F32)
        h2e_buf[...] = jnp.zeros(h2e_buf.shape, BF16)

    post = functools.partial(_post_kernel, x_ref, mix_ref, mod_cur_ref, wout_ref, g1_ref, b1_ref, wr_split_ref,
                             wr_hi_ref, br_ref, x1_buf.at[cur], h2e_buf, bb=bb, tt=tt)
    extra_blocks = _moe_experts(h2e_buf, tri_ref, wg_ref, wu_ref, wd_ref, xs_ref, cw_ref, ys_ref,
                                scatter_ref, post, bb=bb, tt=tt)
    extra_blocks()
    _moe_finish(x1_buf.at[prev], mod_prev_ref, g2_ref, b2_ref, ys_ref, scatter_ref, y_ref, bb=bb, tt=tt)


def _ffn(x, mixin, mod, p, *, bb, tt):
    bsz, seq, d = x.shape
    assert bsz % bb == 0 and seq % tt == 0
    rows = bb * tt
    srows = _moe_rows(rows)
    n_j = seq // tt
    n_tiles = (bsz // bb) * n_j
    tri = jnp.tril(jnp.ones((rows, rows), BF16))
    kern = functools.partial(_ffn_kernel, bb=bb, tt=tt)
    cur = lambda s: jnp.minimum(s, n_tiles - 1)
    prev = lambda s: jnp.maximum(s - 1, 0)
    const2 = lambda s: (0, 0)
    const3 = lambda s: (0, 0, 0)
    resident = pl.Buffered(1)
    return pl.pallas_call(
        kern,
        grid=(n_tiles + 1,),
        in_specs=[pl.BlockSpec((bb, tt, d), lambda s: (cur(s) // n_j, cur(s) % n_j, 0)),
                  pl.BlockSpec((bb, tt, MIX_WIDTH), lambda s: (cur(s) // n_j, cur(s) % n_j, 0)),
                  pl.BlockSpec((bb, 6, d), lambda s: (cur(s) // n_j, 0, 0)),
                  pl.BlockSpec((bb, 6, d), lambda s: (prev(s) // n_j, 0, 0)),
                  pl.BlockSpec((MIX_WIDTH, d), const2, pipeline_mode=resident),
                  pl.BlockSpec((1, d), const2),
                  pl.BlockSpec((1, d), const2),
                  pl.BlockSpec((d, 2 * ROUTE_LANES), const2, pipeline_mode=resident),
                  pl.BlockSpec((d, ROUTE_LANES), const2, pipeline_mode=resident),
                  pl.BlockSpec((1, ROUTE_LANES), const2),
                  pl.BlockSpec((rows, rows), const2, pipeline_mode=resident),
                  pl.BlockSpec(p["w_gate"].shape, const3, pipeline_mode=resident),
                  pl.BlockSpec(p["w_up"].shape, const3, pipeline_mode=resident),
                  pl.BlockSpec(p["w_down"].shape, const3, pipeline_mode=resident),
                  pl.BlockSpec((1, d), const2),
                  pl.BlockSpec((1, d), const2)],
        out_specs=pl.BlockSpec((bb, tt, d), lambda s: (prev(s) // n_j, prev(s) % n_j, 0)),
        out_shape=jax.ShapeDtypeStruct((bsz, seq, d), F32),
        scratch_shapes=[pltpu.VMEM((2, bb, tt, d), F32),
                        pltpu.VMEM((bb, tt, d + ROUTE_LANES), BF16),
                        pltpu.VMEM((srows, d), BF16),
                        pltpu.VMEM((srows, ROUTE_LANES), F32),
                        pltpu.VMEM((srows, d), BF16),
                        pltpu.VMEM((rows, srows), BF16)],
        compiler_params=_cparams("arbitrary"),
        name="ffn",
    )(x, mixin, mod, mod, p["w_out"], p["ln1_g"], p["ln1_b"], p["w_route_split"], p["w_route_hi"], p["b_route"],
      tri, p["w_gate"], p["w_up"], p["w_down"], p["ln2_g"], p["ln2_b"])


def _stream(x, mod, k0, v0, u0, p, *, mixer_tm, masked, bb, tt):
    mixin, nk, nv, nu = _mixer(x, mod, p["w_in"], p["bias"], p["sink"], p["w_dw"], p["b_dw"],
                               p["cln_g"], p["cln_b"], k0, v0, u0, tm=mixer_tm, masked=masked)
    y = _ffn(x, mixin, mod, p, bb=bb, tt=tt)
    bsz = x.shape[0]
    cache_shape = (1, bsz, WINDOW, N_KV_HEADS, HEAD_DIM)
    return y, nk.reshape(cache_shape), nv.reshape(cache_shape), nu[None]


def kernel(x_prompt, x_sample, cache_attn_k, cache_attn_v, state_conv, c_prompt, c_sample, rel_bias, w_ada, b_ada, w_in, attn_sinks, w_dw, b_dw, conv_ln_g, conv_ln_b, w_out, ln1_g, ln1_b, w_group, b_group, w_erouter, b_erouter, w_gate, w_up, w_down, ln2_g, ln2_b):
    bp = x_prompt.shape[0]
    bs = x_sample.shape[0]
    mod = _modulation(jnp.concatenate([c_prompt, c_sample], 0), w_ada[0], b_ada[0])
    mod = mod.reshape(bp + bs, 6, D_MODEL)

    w_route = jnp.concatenate([w_erouter[0].reshape(D_MODEL, N_EXPERTS), w_group[0]], 1)
    w_route = jnp.pad(w_route, ((0, 0), (0, ROUTE_LANES - N_EXPERTS - N_GROUPS)))
    b_route = jnp.concatenate([b_erouter[0].reshape(N_EXPERTS), b_group[0]])
    b_route = jnp.pad(b_route, (0, ROUTE_LANES - N_EXPERTS - N_GROUPS)).reshape(1, ROUTE_LANES)
    w_route_hi = w_route.astype(BF16)
    w_route_lo = (w_route - w_route_hi.astype(F32)).astype(BF16)

    p = {
        "w_in": w_in[0].astype(BF16),
        "bias": _relative_bias(rel_bias),
        "sink": attn_sinks[0],
        "w_dw": jnp.broadcast_to(w_dw[0][:, None, :], (CONV_K, SUBLANES, CONV_CH)),
        "b_dw": b_dw[0].reshape(1, CONV_CH),
        "cln_g": conv_ln_g[0].reshape(1, CONV_CH), "cln_b": conv_ln_b[0].reshape(1, CONV_CH),
        "w_out": w_out[0].astype(BF16),
        "ln1_g": ln1_g[0].reshape(1, D_MODEL), "ln1_b": ln1_b[0].reshape(1, D_MODEL),
        "w_route_split": jnp.concatenate([w_route_hi, w_route_lo], 1), "w_route_hi": w_route_hi,
        "b_route": b_route,
        "w_gate": w_gate[0].astype(BF16), "w_up": w_up[0].astype(BF16),
        "w_down": w_down[0].astype(BF16).reshape(N_GROUPS, EXPERTS_PER_GROUP * D_EXPERT, D_MODEL),
        "ln2_g": ln2_g[0].reshape(1, D_MODEL), "ln2_b": ln2_b[0].reshape(1, D_MODEL),
    }

    zk = jnp.zeros((bp, WINDOW, KV_WIDTH), F32)
    zu = jnp.zeros((bp, CONV_PAD, CONV_CH), F32)
    yp, pk, pv, pc = _stream(x_prompt, mod[:bp], zk, zk, zu, p, mixer_tm=512, masked=True, bb=1, tt=512)

    k0 = cache_attn_k[0].reshape(bs, WINDOW, KV_WIDTH)
    v0 = cache_attn_v[0].reshape(bs, WINDOW, KV_WIDTH)
    u0 = jnp.pad(state_conv[0], ((0, 0), (CONV_PAD - CONV_HIST, 0), (0, 0)))
    ts = x_sample.shape[1]
    ys, sk, sv, sc = _stream(x_sample, mod[bp:], k0, v0, u0, p, mixer_tm=ts, masked=False, bb=8, tt=ts)
    return yp, ys, pk, pv, pc, sk, sv, sc
```

```python
import functools
import math

import jax
import jax.numpy as jnp
from jax import lax
from jax.experimental import pallas as pl
from jax.experimental.pallas import tpu as pltpu

D_MODEL = 1024
CHUNK = 64
N_HEADS = 8
N_KV_HEADS = 2
HEAD_DIM = 64
Q_GROUP = N_HEADS // N_KV_HEADS
ATTN_WIDTH = N_HEADS * HEAD_DIM
KV_WIDTH = N_KV_HEADS * HEAD_DIM
WINDOW = 128
WINDOW_CHUNKS = WINDOW // CHUNK
KEYS = WINDOW + CHUNK
CONV_CH = D_MODEL // 2
CONV_K = 31
CONV_HIST = CONV_K - 1
CONV_PAD = 32
SUBLANES = 8
CONV_ROWS = 256
CONV_LANES = 128
MIX_WIDTH = ATTN_WIDTH + CONV_CH
IN_WIDTH = ATTN_WIDTH + 2 * KV_WIDTH + 2 * CONV_CH
N_BUCKETS = 32
MAX_DISTANCE = 128
N_GROUPS = 4
EXPERTS_PER_GROUP = 4
N_EXPERTS = N_GROUPS * EXPERTS_PER_GROUP
D_EXPERT = D_MODEL // 4
DEPTH = 1
ALPHA = (2 * DEPTH) ** 0.25
LN_EPS = 1e-5
NEG_INF = -1e30
LOG2E = math.log2(math.e)
ROUTE_LANES = 128
COMB_LO_LANE = 16
GROUP_LANE = 32
MOE_BLK = 160
MOE_ALIGN = 16
TILE_PARTS = 2
VMEM_LIMIT = 56 * 1024 * 1024

BF16 = jnp.bfloat16
F32 = jnp.float32


def _cparams(*sem):
    return pltpu.CompilerParams(dimension_semantics=sem, vmem_limit_bytes=VMEM_LIMIT)


def _mod_kernel(c_ref, w_ref, b_ref, o_ref):
    c = c_ref[...]
    s = c * jax.nn.sigmoid(c)
    o_ref[...] = jnp.dot(s, w_ref[...], preferred_element_type=F32,
                         precision=lax.Precision.HIGHEST) + b_ref[...]


def _modulation(c, w_ada, b_ada):
    n, d = c.shape
    width = w_ada.shape[1]
    tn = 1536
    return pl.pallas_call(
        _mod_kernel,
        grid=(width // tn,),
        in_specs=[pl.BlockSpec((n, d), lambda j: (0, 0)),
                  pl.BlockSpec((d, tn), lambda j: (0, j)),
                  pl.BlockSpec((1, tn), lambda j: (0, j))],
        out_specs=pl.BlockSpec((n, tn), lambda j: (0, j)),
        out_shape=jax.ShapeDtypeStruct((n, width), F32),
        compiler_params=_cparams("arbitrary"),
        name="modulation",
    )(c, w_ada, b_ada.reshape(1, width))


def _t5_bucket(rel):
    nb = N_BUCKETS // 2
    max_exact = nb // 2
    ret = jnp.where(rel > 0, nb, 0)
    n = jnp.abs(rel)
    large = max_exact + (jnp.log(jnp.maximum(n, 1).astype(jnp.float32) / max_exact)
                         / math.log(MAX_DISTANCE / max_exact) * (nb - max_exact)).astype(jnp.int32)
    large = jnp.minimum(large, nb - 1)
    return ret + jnp.where(n < max_exact, n, large)


def _bias_kernel(table_ref, bucket_ref, o_ref):
    bucket = bucket_ref[...]
    col = lax.broadcasted_iota(jnp.int32, bucket.shape, 1)
    for h in range(N_HEADS):
        acc = jnp.zeros(bucket.shape, F32)
        for b in range(N_BUCKETS):
            acc = jnp.where(bucket == b, table_ref[b, h], acc)
        for v in range(WINDOW_CHUNKS + 1):
            o_ref[v, h] = jnp.where(col >= v * CHUNK, acc * LOG2E, NEG_INF)


def _relative_bias(table):
    rel = jnp.arange(KEYS)[None, :] - WINDOW - jnp.arange(CHUNK)[:, None]
    bucket = _t5_bucket(rel).astype(jnp.int32)
    nv = WINDOW_CHUNKS + 1
    bias = pl.pallas_call(
        _bias_kernel,
        in_specs=[pl.BlockSpec(memory_space=pltpu.SMEM),
                  pl.BlockSpec((CHUNK, KEYS), lambda: (0, 0))],
        out_specs=pl.BlockSpec((nv, N_HEADS, CHUNK, KEYS), lambda: (0, 0, 0, 0)),
        out_shape=jax.ShapeDtypeStruct((nv, N_HEADS, CHUNK, KEYS), F32),
        name="relative_bias",
    )(table, bucket)
    bias = bias.reshape(nv, N_HEADS // 2, 2, CHUNK, KEYS)
    return jnp.transpose(bias, (0, 1, 3, 2, 4)).reshape(nv, N_HEADS // 2, CHUNK, 2 * KEYS)


def _layer_norm_rows(x, g, b):
    mu = jnp.mean(x, -1, keepdims=True)
    xc = x - mu
    var = jnp.mean(xc * xc, -1, keepdims=True)
    return xc * lax.rsqrt(var + LN_EPS) * g + b


def _mixer_kernel(x_ref, mod_ref, win_ref, bias_ref, sink_ref, wdw_ref, bdw_ref, clg_ref, clb_ref,
                  k0_ref, v0_ref, u0_ref,
                  mix_ref, nk_ref, nv_ref, nu_ref,
                  kz, vz, ush, yconv, s_buf, p_buf, e_buf, *, tm, masked):
    t = pl.program_id(1)
    nt = pl.num_programs(1)
    n_chunks = tm // CHUNK
    ucat = ush.at[0]

    left = lax.broadcasted_iota(jnp.int32, (1, KV_WIDTH), 1) < HEAD_DIM

    def store_kv(row0, k, v):
        n = k.shape[0]
        for dst, val in ((kz, k), (vz, v)):
            swapped = pltpu.roll(val, HEAD_DIM, 1)
            dst[0, row0:row0 + n, 0:KV_WIDTH] = jnp.where(left, val, 0.0).astype(BF16)
            dst[1, row0:row0 + n, 0:KV_WIDTH] = jnp.where(left, 0.0, swapped).astype(BF16)
            dst[2, row0:row0 + n, 0:KV_WIDTH] = jnp.where(left, swapped, 0.0).astype(BF16)
            dst[3, row0:row0 + n, 0:KV_WIDTH] = jnp.where(left, 0.0, val).astype(BF16)

    @pl.when(t == 0)
    def _():
        ones_left = jnp.broadcast_to(jnp.where(left, 1.0, 0.0).astype(BF16), (WINDOW + tm, KV_WIDTH))
        ones_right = jnp.broadcast_to(jnp.where(left, 0.0, 1.0).astype(BF16), (WINDOW + tm, KV_WIDTH))
        for i in range(2 * N_KV_HEADS):
            vz[i, :, KV_WIDTH:] = ones_left if i % 2 == 0 else ones_right
        store_kv(0, k0_ref[0], v0_ref[0])
        ucat[0:CONV_PAD, :] = u0_ref[0]

    sh1 = mod_ref[0, 0:1, :]
    sc1 = mod_ref[0, 1:2, :]
    h = (x_ref[0] * (1.0 + sc1) + sh1).astype(BF16)
    o1 = ATTN_WIDTH
    o3 = o1 + 2 * KV_WIDTH
    q = (jnp.dot(h, win_ref[:, 0:o1], preferred_element_type=F32) * (HEAD_DIM ** -0.5 * LOG2E)).astype(BF16)
    kv = jnp.dot(h, win_ref[:, o1:o3], preferred_element_type=F32)
    ag = jnp.dot(h, win_ref[:, o3:IN_WIDTH], preferred_element_type=F32)
    u = ag[:, 0:CONV_CH] * jax.nn.sigmoid(ag[:, CONV_CH:])
    store_kv(WINDOW, kv[:, 0:KV_WIDTH], kv[:, KV_WIDTH:])
    ucat[CONV_PAD:CONV_PAD + tm, :] = u

    if tm >= WINDOW:
        nk_ref[0] = kv[tm - WINDOW:, 0:KV_WIDTH]
        nv_ref[0] = kv[tm - WINDOW:, KV_WIDTH:]
    else:
        nk_ref[0, 0:WINDOW - tm, :] = k0_ref[0, tm:WINDOW, :]
        nv_ref[0, 0:WINDOW - tm, :] = v0_ref[0, tm:WINDOW, :]
        nk_ref[0, WINDOW - tm:, :] = kv[:, 0:KV_WIDTH]
        nv_ref[0, WINDOW - tm:, :] = kv[:, KV_WIDTH:]
    nu_ref[0] = ucat[CONV_PAD + tm - CONV_HIST:CONV_PAD + tm, :]

    n_pairs = N_HEADS // 2
    pair_w = 2 * HEAD_DIM
    first_head = lax.broadcasted_iota(jnp.int32, (CHUNK, pair_w), 1) < HEAD_DIM
    nt_dims = (((1,), (1,)), ((), ()))

    def window(ref, j, c):
        kvh = (2 * j) // Q_GROUP
        rows = slice(c * CHUNK, c * CHUNK + KEYS)
        return jnp.concatenate([ref[2 * kvh, rows, :], ref[2 * kvh + 1, rows, :]], axis=0)

    for c in range(n_chunks):
        if masked and c < WINDOW_CHUNKS:
            variant = jnp.where(t == 0, WINDOW_CHUNKS - c, 0)
        else:
            variant = 0
        for j in range(n_pairs):
            qp = q[c * CHUNK:(c + 1) * CHUNK, j * pair_w:(j + 1) * pair_w]
            s_buf[c, j] = (lax.dot_general(qp, window(kz, j, c), nt_dims, preferred_element_type=F32)
                           + bias_ref[variant, j])

    for c in range(n_chunks):
        for j in range(n_pairs):
            s0 = s_buf[c, j, :, 0:pair_w]
            s1 = s_buf[c, j, :, pair_w:2 * pair_w]
            s2 = s_buf[c, j, :, 2 * pair_w:]
            sink_a = sink_ref[2 * j] * LOG2E
            sink_b = sink_ref[2 * j + 1] * LOG2E
            m_a = jnp.maximum(jnp.max(jnp.maximum(s0, jnp.where(first_head, s1, NEG_INF)), -1, keepdims=True), sink_a)
            m_b = jnp.maximum(jnp.max(jnp.maximum(jnp.where(first_head, NEG_INF, s1), s2), -1, keepdims=True), sink_b)
            p_buf[c, j, :, 0:pair_w] = jnp.exp2(s0 - m_a).astype(BF16)
            p_buf[c, j, :, pair_w:2 * pair_w] = jnp.exp2(s1 - jnp.where(first_head, m_a, m_b)).astype(BF16)
            p_buf[c, j, :, 2 * pair_w:] = jnp.exp2(s2 - m_b).astype(BF16)
            e_buf[c, j] = jnp.where(first_head, jnp.exp2(sink_a - m_a), jnp.exp2(sink_b - m_b))

    span = tm + CONV_PAD - SUBLANES
    for s in range(1, SUBLANES):
        ush[s, 0:span, :] = ucat[s:s + span, :]
    off = CONV_PAD - CONV_HIST
    rb = min(CONV_ROWS, tm)
    groups = rb // SUBLANES

    def conv_rows(lanes, r, carry):
        r0 = pl.multiple_of(r * rb, rb)
        partial = []
        for shift in range(SUBLANES):
            taps = [k for k in range(CONV_K) if (off + k) % SUBLANES == shift]
            top = max(off + k - shift for k in taps)
            slab = ush[shift, pl.ds(r0, rb + top), lanes].reshape(groups + top // SUBLANES, SUBLANES, CONV_LANES)
            acc = None
            for k in taps:
                g0 = (off + k - shift) // SUBLANES
                term = slab[g0:g0 + groups] * wdw_ref[k, :, lanes]
                acc = term if acc is None else acc + term
            partial.append(acc)
        while len(partial) > 1:
            partial = [a + b for a, b in zip(partial[0::2], partial[1::2])]
        yconv[pl.ds(r0, rb), lanes] = (partial[0] + bdw_ref[:, lanes]).reshape(rb, CONV_LANES)
        return carry

    for part in range(CONV_CH // CONV_LANES):
        lanes = slice(part * CONV_LANES, (part + 1) * CONV_LANES)
        lax.fori_loop(0, tm // rb, functools.partial(conv_rows, lanes), 0)
    y = _layer_norm_rows(yconv[...], clg_ref[...], clb_ref[...])
    mix_ref[0, :, ATTN_WIDTH:] = (y * jax.nn.sigmoid(y)).astype(BF16)

    for c in range(n_chunks):
        for j in range(n_pairs):
            res = jnp.dot(p_buf[c, j], window(vz, j, c), preferred_element_type=F32)
            out = res[:, 0:pair_w] * (1.0 / (res[:, pair_w:] + e_buf[c, j]))
            mix_ref[0, c * CHUNK:(c + 1) * CHUNK, j * pair_w:(j + 1) * pair_w] = out.astype(BF16)

    @pl.when(t < nt - 1)
    def _():
        if tm >= WINDOW:
            for i in range(2 * N_KV_HEADS):
                kz[i, 0:WINDOW, :] = kz[i, tm:tm + WINDOW, :]
                vz[i, 0:WINDOW, 0:KV_WIDTH] = vz[i, tm:tm + WINDOW, 0:KV_WIDTH]
            ucat[0:CONV_PAD, :] = ucat[tm:tm + CONV_PAD, :]


def _mixer(x, mod, w_in, bias, sink, w_dw, b_dw, cln_g, cln_b, k0, v0, u0, *, tm, masked):
    bsz, seq, d = x.shape
    assert seq % tm == 0 and tm % CHUNK == 0 and (tm >= WINDOW or seq == tm)
    nt = seq // tm
    kern = functools.partial(_mixer_kernel, tm=tm, masked=masked)
    const2 = lambda b, t: (0, 0)
    const3 = lambda b, t: (0, 0, 0)
    per_b = lambda b, t: (b, 0, 0)
    return pl.pallas_call(
        kern,
        grid=(bsz, nt),
        in_specs=[pl.BlockSpec((1, tm, d), lambda b, t: (b, t, 0)),
                  pl.BlockSpec((1, 6, d), per_b),
                  pl.BlockSpec((d, IN_WIDTH), const2),
                  pl.BlockSpec((WINDOW_CHUNKS + 1, N_HEADS // 2, CHUNK, 2 * KEYS), lambda b, t: (0, 0, 0, 0)),
                  pl.BlockSpec(memory_space=pltpu.SMEM),
                  pl.BlockSpec((CONV_K, SUBLANES, CONV_CH), const3),
                  pl.BlockSpec((1, CONV_CH), const2),
                  pl.BlockSpec((1, CONV_CH), const2),
                  pl.BlockSpec((1, CONV_CH), const2),
                  pl.BlockSpec((1, WINDOW, KV_WIDTH), per_b),
                  pl.BlockSpec((1, WINDOW, KV_WIDTH), per_b),
                  pl.BlockSpec((1, CONV_PAD, CONV_CH), per_b)],
        out_specs=[pl.BlockSpec((1, tm, MIX_WIDTH), lambda b, t: (b, t, 0)),
                   pl.BlockSpec((1, WINDOW, KV_WIDTH), per_b),
                   pl.BlockSpec((1, WINDOW, KV_WIDTH), per_b),
                   pl.BlockSpec((1, CONV_HIST, CONV_CH), per_b)],
        out_shape=[jax.ShapeDtypeStruct((bsz, seq, MIX_WIDTH), BF16),
                   jax.ShapeDtypeStruct((bsz, WINDOW, KV_WIDTH), F32),
                   jax.ShapeDtypeStruct((bsz, WINDOW, KV_WIDTH), F32),
                   jax.ShapeDtypeStruct((bsz, CONV_HIST, CONV_CH), F32)],
        scratch_shapes=[pltpu.VMEM((2 * N_KV_HEADS, WINDOW + tm, KV_WIDTH), BF16),
                        pltpu.VMEM((2 * N_KV_HEADS, WINDOW + tm, 2 * KV_WIDTH), BF16),
                        pltpu.VMEM((SUBLANES, CONV_PAD + tm, CONV_CH), F32),
                        pltpu.VMEM((tm, CONV_CH), F32),
                        pltpu.VMEM((tm // CHUNK, N_HEADS // 2, CHUNK, 2 * KEYS), F32),
                        pltpu.VMEM((tm // CHUNK, N_HEADS // 2, CHUNK, 2 * KEYS), BF16),
                        pltpu.VMEM((tm // CHUNK, N_HEADS // 2, CHUNK, 2 * HEAD_DIM), F32)],
        compiler_params=_cparams("parallel", "arbitrary"),
        name="mixer",
    )(x, mod, w_in, bias, sink, w_dw, b_dw, cln_g, cln_b, k0, v0, u0)


def _out_projection(mix_ref, wout_ref, *, bb, tt):
    return jnp.dot(mix_ref[...].reshape(bb * tt, MIX_WIDTH), wout_ref[...], preferred_element_type=F32)


def _post_kernel(mix, x_ref, mod_ref, g_ref, b_ref, wr_split_ref, wr_hi_ref, br_ref, x1_ref, h2e_ref, *, bb, tt):
    rows = bb * tt
    g1 = mod_ref[:, 2:3, :]
    sh2 = mod_ref[:, 3:4, :]
    sc2 = mod_ref[:, 4:5, :]
    r = ALPHA * x_ref[...] + (1.0 + g1) * mix.reshape(bb, tt, D_MODEL)
    x1 = _layer_norm_rows(r, g_ref[...], b_ref[...])
    x1_ref[...] = x1
    h2 = (x1 * (1.0 + sc2) + sh2).reshape(rows, D_MODEL)
    h2_hi = h2.astype(BF16)
    h2e_ref[:, :, 0:D_MODEL] = h2_hi.reshape(bb, tt, D_MODEL)

    h2_lo = (h2 - h2_hi.astype(F32)).astype(BF16)
    hi_terms = jnp.dot(h2_hi, wr_split_ref[...], preferred_element_type=F32)
    logits = (hi_terms[:, 0:ROUTE_LANES] + hi_terms[:, ROUTE_LANES:]
              + jnp.dot(h2_lo, wr_hi_ref[...], preferred_element_type=F32) + br_ref[...])
    lane = lax.broadcasted_iota(jnp.int32, (rows, ROUTE_LANES), 1).astype(F32)
    far = float(ROUTE_LANES)
    is_group = (lane >= N_EXPERTS) & (lane < N_EXPERTS + N_GROUPS)
    gl = jnp.where(is_group, logits, NEG_INF)
    gmax = jnp.max(gl, -1, keepdims=True)
    gidx = jnp.min(jnp.where(is_group & (gl == gmax), lane, far), -1, keepdims=True) - N_EXPERTS
    pg = 1.0 / jnp.sum(jnp.where(is_group, jnp.exp(gl - gmax), 0.0), -1, keepdims=True)
    in_group = (lane >= gidx * EXPERTS_PER_GROUP) & (lane < (gidx + 1) * EXPERTS_PER_GROUP)
    el = jnp.where(in_group, logits, NEG_INF)
    v1 = jnp.max(el, -1, keepdims=True)
    i1 = jnp.min(jnp.where(in_group & (el == v1), lane, far), -1, keepdims=True)
    rest = in_group & (lane != i1)
    el2 = jnp.where(rest, logits, NEG_INF)
    v2 = jnp.max(el2, -1, keepdims=True)
    i2 = jnp.min(jnp.where(rest & (el2 == v2), lane, far), -1, keepdims=True)
    e2 = jnp.exp(v2 - v1)
    w1 = pg / (1.0 + e2)
    w2 = pg * e2 / (1.0 + e2)
    comb = jnp.where(lane == i1, w1, jnp.where(lane == i2, w2, 0.0))

    c_hi = comb.astype(BF16).astype(F32)
    c_lo = (comb - c_hi).astype(BF16).astype(F32)
    rec = c_hi + pltpu.roll(c_lo, COMB_LO_LANE, 1) + jnp.where(lane == gidx + GROUP_LANE, 1.0, 0.0)
    h2e_ref[:, :, D_MODEL:] = rec.astype(BF16).reshape(bb, tt, ROUTE_LANES)


def _moe_rows(rows):
    return rows + N_GROUPS * MOE_ALIGN + MOE_BLK


def _moe_experts(h2e_ref, tri_ref, wg_ref, wu_ref, wd_ref, xs_ref, cw_ref, ys_ref, scatter_ref, other_stream,
                 *, bb, tt):
    rows = bb * tt
    srows = _moe_rows(rows)
    ext = h2e_ref[...].reshape(rows, D_MODEL + ROUTE_LANES)
    rec = ext[:, D_MODEL:]
    cum = jnp.dot(tri_ref[...], rec, preferred_element_type=F32)
    lane1 = lax.broadcasted_iota(jnp.int32, (1, ROUTE_LANES), 1)
    cnt = cum[rows - 1:rows, :]
    off = jnp.int32(0)
    offs, counts = [], []
    off_vec = jnp.zeros((1, ROUTE_LANES), F32)
    for g in range(N_GROUPS):
        n_g = jnp.sum(jnp.where(lane1 == GROUP_LANE + g, cnt, 0.0)).astype(jnp.int32)
        offs.append(off)
        counts.append(n_g)
        off_vec = jnp.where(lane1 == GROUP_LANE + g, off.astype(F32), off_vec)
        off = off + ((n_g + (MOE_ALIGN - 1)) // MOE_ALIGN) * MOE_ALIGN

    lane = lax.broadcasted_iota(jnp.int32, (rows, ROUTE_LANES), 1)
    mine = (lane >= GROUP_LANE) & (lane < GROUP_LANE + N_GROUPS) & (rec.astype(F32) > 0.0)
    pos = jnp.sum(jnp.where(mine, cum - 1.0 + off_vec, 0.0), -1, keepdims=True)
    scatter_ref[...] = jnp.where(pos == lax.broadcasted_iota(jnp.int32, (rows, srows), 1).astype(F32),
                                 1.0, 0.0).astype(BF16)
    pos_row = jnp.transpose(jnp.broadcast_to(pos, (rows, ROUTE_LANES)))[0:1, :]
    grows = rows + N_GROUPS * MOE_ALIGN
    gather = jnp.where(lax.broadcasted_iota(jnp.int32, (grows, rows), 0).astype(F32) == pos_row, 1.0, 0.0).astype(BF16)
    xs_ref[0:grows, :] = jnp.dot(gather, ext[:, 0:D_MODEL], preferred_element_type=F32).astype(BF16)
    rs = jnp.dot(gather, rec, preferred_element_type=F32)
    cw_ref[0:grows, :] = rs + pltpu.roll(rs, ROUTE_LANES - COMB_LO_LANE, 1)

    def experts(g, r0):
        xb = xs_ref[pl.ds(r0, MOE_BLK), :]
        cwb = cw_ref[pl.ds(r0, MOE_BLK), :]
        parts = []
        for j in range(EXPERTS_PER_GROUP):
            e = g * EXPERTS_PER_GROUP + j
            hg = jnp.dot(xb, wg_ref[e], preferred_element_type=F32)
            hu = jnp.dot(xb, wu_ref[e], preferred_element_type=F32)
            parts.append((hg * jax.nn.sigmoid(hg) * hu * cwb[:, e:e + 1]).astype(BF16))
        act = jnp.concatenate(parts, axis=1)
        return jnp.dot(act, wd_ref[g], preferred_element_type=F32)

    for g in range(N_GROUPS):
        r0 = pl.multiple_of(offs[g], MOE_ALIGN)
        ys_ref[pl.ds(r0, MOE_BLK), :] = experts(g, r0).astype(BF16)
        if g == 1:
            other_stream()

    def extra_block(g, i, carry):
        r0 = pl.multiple_of(offs[g] + i * MOE_BLK, MOE_ALIGN)
        row = r0 + lax.broadcasted_iota(jnp.int32, (MOE_BLK, 1), 0)
        old = ys_ref[pl.ds(r0, MOE_BLK), :].astype(F32)
        ys_ref[pl.ds(r0, MOE_BLK), :] = jnp.where(row < offs[g] + counts[g], experts(g, r0), old).astype(BF16)
        return carry

    def extra_blocks():
        for g in range(N_GROUPS):
            lax.fori_loop(1, (counts[g] + (MOE_BLK - 1)) // MOE_BLK, functools.partial(extra_block, g), 0)

    return extra_blocks


def _moe_finish(x1_ref, mod_ref, g_ref, b_ref, ys_ref, scatter_ref, y_ref, *, bb, tt):
    ff = jnp.dot(scatter_ref[...], ys_ref[...], preferred_element_type=F32)
    g2 = mod_ref[:, 5:6, :]
    r = ALPHA * x1_ref[...] + (1.0 + g2) * ff.reshape(bb, tt, D_MODEL)
    y_ref[...] = _layer_norm_rows(r, g_ref[...], b_ref[...])


def _row_parts(bb, tt, n):
    if bb == 1:
        pt = tt // n
        return [((slice(None), slice(i * pt, (i + 1) * pt)), 1, pt) for i in range(n)]
    pb = bb // n
    return [((slice(i * pb, (i + 1) * pb), slice(None)), pb, tt) for i in range(n)]


def _ffn_kernel(x_ref, mix_ref, mod_cur_ref, mod_prev_ref, wout_ref, g1_ref, b1_ref, wr_split_ref, wr_hi_ref,
                br_ref, tri_ref, wg_ref, wu_ref, wd_ref, g2_ref, b2_ref, y_ref,
                x1_buf, h2e_buf, xs_ref, cw_ref, ys_ref, scatter_ref, *, bb, tt):
    s = pl.program_id(0)
    cur = s % 2
    prev = 1 - cur

    @pl.when(s == 0)
    def _():
        ys_ref[...] = jnp.zeros_like(ys_ref)
        xs_ref[...] = jnp.zeros_like(xs_ref)
        cw_ref[...] = jnp.zeros_like(cw_ref)
        x1_buf[1] = jnp.zeros(x1_buf.shape[1:], F32)
        h2e_buf[...] = jnp.zeros(h2e_buf.shape, BF16)

    mix = _out_projection(mix_ref, wout_ref, bb=bb, tt=tt)
    post = functools.partial(_post_kernel, mix, x_ref, mod_cur_ref, g1_ref, b1_ref, wr_split_ref,
                             wr_hi_ref, br_ref, x1_buf.at[cur], h2e_buf, bb=bb, tt=tt)
    extra_blocks = _moe_experts(h2e_buf, tri_ref, wg_ref, wu_ref, wd_ref, xs_ref, cw_ref, ys_ref,
                                scatter_ref, post, bb=bb, tt=tt)
    extra_blocks()
    for i, (sl, pb, pt) in enumerate(_row_parts(bb, tt, TILE_PARTS)):
        part_rows = slice(i * pb * pt, (i + 1) * pb * pt)
        _moe_finish(x1_buf.at[prev].at[sl], mod_prev_ref.at[sl[0]], g2_ref, b2_ref, ys_ref, scatter_ref.at[part_rows],
                    y_ref.at[sl], bb=pb, tt=pt)


def _ffn(x, mixin, mod, p, *, bb, tt):
    bsz, seq, d = x.shape
    assert bsz % bb == 0 and seq % tt == 0
    rows = bb * tt
    srows = _moe_rows(rows)
    n_j = seq // tt
    n_tiles = (bsz // bb) * n_j
    tri = jnp.tril(jnp.ones((rows, rows), BF16))
    kern = functools.partial(_ffn_kernel, bb=bb, tt=tt)
    cur = lambda s: jnp.minimum(s, n_tiles - 1)
    prev = lambda s: jnp.maximum(s - 1, 0)
    const2 = lambda s: (0, 0)
    const3 = lambda s: (0, 0, 0)
    resident = pl.Buffered(1)
    return pl.pallas_call(
        kern,
        grid=(n_tiles + 1,),
        in_specs=[pl.BlockSpec((bb, tt, d), lambda s: (cur(s) // n_j, cur(s) % n_j, 0)),
                  pl.BlockSpec((bb, tt, MIX_WIDTH), lambda s: (cur(s) // n_j, cur(s) % n_j, 0)),
                  pl.BlockSpec((bb, 6, d), lambda s: (cur(s) // n_j, 0, 0)),
                  pl.BlockSpec((bb, 6, d), lambda s: (prev(s) // n_j, 0, 0)),
                  pl.BlockSpec((MIX_WIDTH, d), const2, pipeline_mode=resident),
                  pl.BlockSpec((1, d), const2),
                  pl.BlockSpec((1, d), const2),
                  pl.BlockSpec((d, 2 * ROUTE_LANES), const2, pipeline_mode=resident),
                  pl.BlockSpec((d, ROUTE_LANES), const2, pipeline_mode=resident),
                  pl.BlockSpec((1, ROUTE_LANES), const2),
                  pl.BlockSpec((rows, rows), const2, pipeline_mode=resident),
                  pl.BlockSpec(p["w_gate"].shape, const3, pipeline_mode=resident),
                  pl.BlockSpec(p["w_up"].shape, const3, pipeline_mode=resident),
                  pl.BlockSpec(p["w_down"].shape, const3, pipeline_mode=resident),
                  pl.BlockSpec((1, d), const2),
                  pl.BlockSpec((1, d), const2)],
        out_specs=pl.BlockSpec((bb, tt, d), lambda s: (prev(s) // n_j, prev(s) % n_j, 0)),
        out_shape=jax.ShapeDtypeStruct((bsz, seq, d), F32),
        scratch_shapes=[pltpu.VMEM((2, bb, tt, d), F32),
                        pltpu.VMEM((bb, tt, d + ROUTE_LANES), BF16),
                        pltpu.VMEM((srows, d), BF16),
                        pltpu.VMEM((srows, ROUTE_LANES), F32),
                        pltpu.VMEM((srows, d), BF16),
                        pltpu.VMEM((rows, srows), BF16)],
        compiler_params=_cparams("arbitrary"),
        name="ffn",
    )(x, mixin, mod, mod, p["w_out"], p["ln1_g"], p["ln1_b"], p["w_route_split"], p["w_route_hi"], p["b_route"],
      tri, p["w_gate"], p["w_up"], p["w_down"], p["ln2_g"], p["ln2_b"])


def _stream(x, mod, k0, v0, u0, p, *, mixer_tm, masked, bb, tt):
    mixin, nk, nv, nu = _mixer(x, mod, p["w_in"], p["bias"], p["sink"], p["w_dw"], p["b_dw"],
                               p["cln_g"], p["cln_b"], k0, v0, u0, tm=mixer_tm, masked=masked)
    y = _ffn(x, mixin, mod, p, bb=bb, tt=tt)
    bsz = x.shape[0]
    cache_shape = (1, bsz, WINDOW, N_KV_HEADS, HEAD_DIM)
    return y, nk.reshape(cache_shape), nv.reshape(cache_shape), nu[None]


def kernel(x_prompt, x_sample, cache_attn_k, cache_attn_v, state_conv, c_prompt, c_sample, rel_bias, w_ada, b_ada, w_in, attn_sinks, w_dw, b_dw, conv_ln_g, conv_ln_b, w_out, ln1_g, ln1_b, w_group, b_group, w_erouter, b_erouter, w_gate, w_up, w_down, ln2_g, ln2_b):
    bp = x_prompt.shape[0]
    bs = x_sample.shape[0]
    mod = _modulation(jnp.concatenate([c_prompt, c_sample], 0), w_ada[0], b_ada[0])
    mod = mod.reshape(bp + bs, 6, D_MODEL)

    w_route = jnp.concatenate([w_erouter[0].reshape(D_MODEL, N_EXPERTS), w_group[0]], 1)
    w_route = jnp.pad(w_route, ((0, 0), (0, ROUTE_LANES - N_EXPERTS - N_GROUPS)))
    b_route = jnp.concatenate([b_erouter[0].reshape(N_EXPERTS), b_group[0]])
    b_route = jnp.pad(b_route, (0, ROUTE_LANES - N_EXPERTS - N_GROUPS)).reshape(1, ROUTE_LANES)
    w_route_hi = w_route.astype(BF16)
    w_route_lo = (w_route - w_route_hi.astype(F32)).astype(BF16)

    p = {
        "w_in": w_in[0].astype(BF16),
        "bias": _relative_bias(rel_bias),
        "sink": attn_sinks[0],
        "w_dw": jnp.broadcast_to(w_dw[0][:, None, :], (CONV_K, SUBLANES, CONV_CH)),
        "b_dw": b_dw[0].reshape(1, CONV_CH),
        "cln_g": conv_ln_g[0].reshape(1, CONV_CH), "cln_b": conv_ln_b[0].reshape(1, CONV_CH),
        "w_out": w_out[0].astype(BF16),
        "ln1_g": ln1_g[0].reshape(1, D_MODEL), "ln1_b": ln1_b[0].reshape(1, D_MODEL),
        "w_route_split": jnp.concatenate([w_route_hi, w_route_lo], 1), "w_route_hi": w_route_hi,
        "b_route": b_route,
        "w_gate": w_gate[0].astype(BF16), "w_up": w_up[0].astype(BF16),
        "w_down": w_down[0].astype(BF16).reshape(N_GROUPS, EXPERTS_PER_GROUP * D_EXPERT, D_MODEL),
        "ln2_g": ln2_g[0].reshape(1, D_MODEL), "ln2_b": ln2_b[0].reshape(1, D_MODEL),
    }

    zk = jnp.zeros((bp, WINDOW, KV_WIDTH), F32)
    zu = jnp.zeros((bp, CONV_PAD, CONV_CH), F32)
    yp, pk, pv, pc = _stream(x_prompt, mod[:bp], zk, zk, zu, p, mixer_tm=512, masked=True, bb=1, tt=512)

    k0 = cache_attn_k[0].reshape(bs, WINDOW, KV_WIDTH)
    v0 = cache_attn_v[0].reshape(bs, WINDOW, KV_WIDTH)
    u0 = jnp.pad(state_conv[0], ((0, 0), (CONV_PAD - CONV_HIST, 0), (0, 0)))
    ts = x_sample.shape[1]
    ys, sk, sv, sc = _stream(x_sample, mod[bp:], k0, v0, u0, p, mixer_tm=ts, masked=False, bb=8, tt=ts)
    return yp, ys, pk, pv, pc, sk, sv, sc
```

```python
import functools
import math

import jax
import jax.numpy as jnp
from jax import lax
from jax.experimental import pallas as pl
from jax.experimental.pallas import tpu as pltpu

D_MODEL = 1024
CHUNK = 64
N_HEADS = 8
N_KV_HEADS = 2
HEAD_DIM = 64
Q_GROUP = N_HEADS // N_KV_HEADS
ATTN_WIDTH = N_HEADS * HEAD_DIM
KV_WIDTH = N_KV_HEADS * HEAD_DIM
WINDOW = 128
WINDOW_CHUNKS = WINDOW // CHUNK
KEYS = WINDOW + CHUNK
CONV_CH = D_MODEL // 2
CONV_K = 31
CONV_HIST = CONV_K - 1
CONV_PAD = 32
SUBLANES = 8
CONV_ROWS = 256
CONV_LANES = 128
MIX_WIDTH = ATTN_WIDTH + CONV_CH
IN_WIDTH = ATTN_WIDTH + 2 * KV_WIDTH + 2 * CONV_CH
N_BUCKETS = 32
MAX_DISTANCE = 128
N_GROUPS = 4
EXPERTS_PER_GROUP = 4
N_EXPERTS = N_GROUPS * EXPERTS_PER_GROUP
D_EXPERT = D_MODEL // 4
DEPTH = 1
ALPHA = (2 * DEPTH) ** 0.25
LN_EPS = 1e-5
NEG_INF = -1e30
LOG2E = math.log2(math.e)
ROUTE_LANES = 128
COMB_LO_LANE = 16
GROUP_LANE = 32
MOE_WINDOWS = (160, 144, 144, 128)
MOE_ALIGN = 16
TILE_PARTS = 2
VMEM_LIMIT = 56 * 1024 * 1024

BF16 = jnp.bfloat16
F32 = jnp.float32


def _cparams(*sem):
    return pltpu.CompilerParams(dimension_semantics=sem, vmem_limit_bytes=VMEM_LIMIT)


def _mod_kernel(c_ref, w_ref, b_ref, o_ref):
    c = c_ref[...]
    s = c * jax.nn.sigmoid(c)
    o_ref[...] = jnp.dot(s, w_ref[...], preferred_element_type=F32,
                         precision=lax.Precision.HIGHEST) + b_ref[...]


def _modulation(c, w_ada, b_ada):
    n, d = c.shape
    width = w_ada.shape[1]
    tn = 1536
    return pl.pallas_call(
        _mod_kernel,
        grid=(width // tn,),
        in_specs=[pl.BlockSpec((n, d), lambda j: (0, 0)),
                  pl.BlockSpec((d, tn), lambda j: (0, j)),
                  pl.BlockSpec((1, tn), lambda j: (0, j))],
        out_specs=pl.BlockSpec((n, tn), lambda j: (0, j)),
        out_shape=jax.ShapeDtypeStruct((n, width), F32),
        compiler_params=_cparams("arbitrary"),
        name="modulation",
    )(c, w_ada, b_ada.reshape(1, width))


def _t5_bucket(rel):
    nb = N_BUCKETS // 2
    max_exact = nb // 2
    ret = jnp.where(rel > 0, nb, 0)
    n = jnp.abs(rel)
    large = max_exact + (jnp.log(jnp.maximum(n, 1).astype(jnp.float32) / max_exact)
                         / math.log(MAX_DISTANCE / max_exact) * (nb - max_exact)).astype(jnp.int32)
    large = jnp.minimum(large, nb - 1)
    return ret + jnp.where(n < max_exact, n, large)


def _bias_kernel(table_ref, bucket_ref, o_ref):
    bucket = bucket_ref[...]
    col = lax.broadcasted_iota(jnp.int32, bucket.shape, 1)
    for h in range(N_HEADS):
        acc = jnp.zeros(bucket.shape, F32)
        for b in range(N_BUCKETS):
            acc = jnp.where(bucket == b, table_ref[b, h], acc)
        for v in range(WINDOW_CHUNKS + 1):
            o_ref[v, h] = jnp.where(col >= v * CHUNK, acc * LOG2E, NEG_INF)


def _relative_bias(table):
    rel = jnp.arange(KEYS)[None, :] - WINDOW - jnp.arange(CHUNK)[:, None]
    bucket = _t5_bucket(rel).astype(jnp.int32)
    nv = WINDOW_CHUNKS + 1
    bias = pl.pallas_call(
        _bias_kernel,
        in_specs=[pl.BlockSpec(memory_space=pltpu.SMEM),
                  pl.BlockSpec((CHUNK, KEYS), lambda: (0, 0))],
        out_specs=pl.BlockSpec((nv, N_HEADS, CHUNK, KEYS), lambda: (0, 0, 0, 0)),
        out_shape=jax.ShapeDtypeStruct((nv, N_HEADS, CHUNK, KEYS), F32),
        name="relative_bias",
    )(table, bucket)
    bias = bias.reshape(nv, N_HEADS // 2, 2, CHUNK, KEYS)
    return jnp.transpose(bias, (0, 1, 3, 2, 4)).reshape(nv, N_HEADS // 2, CHUNK, 2 * KEYS)


def _layer_norm_rows(x, g, b):
    mu = jnp.mean(x, -1, keepdims=True)
    xc = x - mu
    var = jnp.mean(xc * xc, -1, keepdims=True)
    return xc * lax.rsqrt(var + LN_EPS) * g + b


def _mixer_kernel(x_ref, mod_ref, win_ref, bias_ref, sink_ref, wdw_ref, bdw_ref, clg_ref, clb_ref,
                  k0_ref, v0_ref, u0_ref,
                  mix_ref, nk_ref, nv_ref, nu_ref,
                  kz, vz, ush, yconv, s_buf, p_buf, e_buf, *, tm, masked):
    t = pl.program_id(1)
    nt = pl.num_programs(1)
    n_chunks = tm // CHUNK
    ucat = ush.at[0]

    left = lax.broadcasted_iota(jnp.int32, (1, KV_WIDTH), 1) < HEAD_DIM

    def store_kv(row0, k, v):
        n = k.shape[0]
        for dst, val in ((kz, k), (vz, v)):
            swapped = pltpu.roll(val, HEAD_DIM, 1)
            dst[0, row0:row0 + n, 0:KV_WIDTH] = jnp.where(left, val, 0.0).astype(BF16)
            dst[1, row0:row0 + n, 0:KV_WIDTH] = jnp.where(left, 0.0, swapped).astype(BF16)
            dst[2, row0:row0 + n, 0:KV_WIDTH] = jnp.where(left, swapped, 0.0).astype(BF16)
            dst[3, row0:row0 + n, 0:KV_WIDTH] = jnp.where(left, 0.0, val).astype(BF16)

    @pl.when(t == 0)
    def _():
        ones_left = jnp.broadcast_to(jnp.where(left, 1.0, 0.0).astype(BF16), (WINDOW + tm, KV_WIDTH))
        ones_right = jnp.broadcast_to(jnp.where(left, 0.0, 1.0).astype(BF16), (WINDOW + tm, KV_WIDTH))
        for i in range(2 * N_KV_HEADS):
            vz[i, :, KV_WIDTH:] = ones_left if i % 2 == 0 else ones_right
        store_kv(0, k0_ref[0], v0_ref[0])
        ucat[0:CONV_PAD, :] = u0_ref[0]

    sh1 = mod_ref[0, 0:1, :]
    sc1 = mod_ref[0, 1:2, :]
    h = (x_ref[0] * (1.0 + sc1) + sh1).astype(BF16)
    o1 = ATTN_WIDTH
    o3 = o1 + 2 * KV_WIDTH
    q = (jnp.dot(h, win_ref[:, 0:o1], preferred_element_type=F32) * (HEAD_DIM ** -0.5 * LOG2E)).astype(BF16)
    kv = jnp.dot(h, win_ref[:, o1:o3], preferred_element_type=F32)
    ag = jnp.dot(h, win_ref[:, o3:IN_WIDTH], preferred_element_type=F32)
    u = ag[:, 0:CONV_CH] * jax.nn.sigmoid(ag[:, CONV_CH:])
    store_kv(WINDOW, kv[:, 0:KV_WIDTH], kv[:, KV_WIDTH:])
    ucat[CONV_PAD:CONV_PAD + tm, :] = u

    if tm >= WINDOW:
        nk_ref[0] = kv[tm - WINDOW:, 0:KV_WIDTH]
        nv_ref[0] = kv[tm - WINDOW:, KV_WIDTH:]
    else:
        nk_ref[0, 0:WINDOW - tm, :] = k0_ref[0, tm:WINDOW, :]
        nv_ref[0, 0:WINDOW - tm, :] = v0_ref[0, tm:WINDOW, :]
        nk_ref[0, WINDOW - tm:, :] = kv[:, 0:KV_WIDTH]
        nv_ref[0, WINDOW - tm:, :] = kv[:, KV_WIDTH:]
    nu_ref[0] = ucat[CONV_PAD + tm - CONV_HIST:CONV_PAD + tm, :]

    n_pairs = N_HEADS // 2
    pair_w = 2 * HEAD_DIM
    first_head = lax.broadcasted_iota(jnp.int32, (CHUNK, pair_w), 1) < HEAD_DIM
    nt_dims = (((1,), (1,)), ((), ()))

    def window(ref, j, c):
        kvh = (2 * j) // Q_GROUP
        rows = slice(c * CHUNK, c * CHUNK + KEYS)
        return jnp.concatenate([ref[2 * kvh, rows, :], ref[2 * kvh + 1, rows, :]], axis=0)

    for c in range(n_chunks):
        if masked and c < WINDOW_CHUNKS:
            variant = jnp.where(t == 0, WINDOW_CHUNKS - c, 0)
        else:
            variant = 0
        for j in range(n_pairs):
            qp = q[c * CHUNK:(c + 1) * CHUNK, j * pair_w:(j + 1) * pair_w]
            s_buf[c, j] = (lax.dot_general(qp, window(kz, j, c), nt_dims, preferred_element_type=F32)
                           + bias_ref[variant, j])

    for c in range(n_chunks):
        for j in range(n_pairs):
            s0 = s_buf[c, j, :, 0:pair_w]
            s1 = s_buf[c, j, :, pair_w:2 * pair_w]
            s2 = s_buf[c, j, :, 2 * pair_w:]
            sink_a = sink_ref[2 * j] * LOG2E
            sink_b = sink_ref[2 * j + 1] * LOG2E
            m_a = jnp.maximum(jnp.max(jnp.maximum(s0, jnp.where(first_head, s1, NEG_INF)), -1, keepdims=True), sink_a)
            m_b = jnp.maximum(jnp.max(jnp.maximum(jnp.where(first_head, NEG_INF, s1), s2), -1, keepdims=True), sink_b)
            p_buf[c, j, :, 0:pair_w] = jnp.exp2(s0 - m_a).astype(BF16)
            p_buf[c, j, :, pair_w:2 * pair_w] = jnp.exp2(s1 - jnp.where(first_head, m_a, m_b)).astype(BF16)
            p_buf[c, j, :, 2 * pair_w:] = jnp.exp2(s2 - m_b).astype(BF16)
            e_buf[c, j] = jnp.where(first_head, jnp.exp2(sink_a - m_a), jnp.exp2(sink_b - m_b))

    span = tm + CONV_PAD - SUBLANES
    for s in range(1, SUBLANES):
        ush[s, 0:span, :] = ucat[s:s + span, :]
    off = CONV_PAD - CONV_HIST
    rb = min(CONV_ROWS, tm)
    groups = rb // SUBLANES

    def conv_rows(lanes, r, carry):
        r0 = pl.multiple_of(r * rb, rb)
        partial = []
        for shift in range(SUBLANES):
            taps = [k for k in range(CONV_K) if (off + k) % SUBLANES == shift]
            top = max(off + k - shift for k in taps)
            slab = ush[shift, pl.ds(r0, rb + top), lanes].reshape(groups + top // SUBLANES, SUBLANES, CONV_LANES)
            acc = None
            for k in taps:
                g0 = (off + k - shift) // SUBLANES
                term = slab[g0:g0 + groups] * wdw_ref[k, :, lanes]
                acc = term if acc is None else acc + term
            partial.append(acc)
        while len(partial) > 1:
            partial = [a + b for a, b in zip(partial[0::2], partial[1::2])]
        yconv[pl.ds(r0, rb), lanes] = (partial[0] + bdw_ref[:, lanes]).reshape(rb, CONV_LANES)
        return carry

    for part in range(CONV_CH // CONV_LANES):
        lanes = slice(part * CONV_LANES, (part + 1) * CONV_LANES)
        lax.fori_loop(0, tm // rb, functools.partial(conv_rows, lanes), 0)
    y = _layer_norm_rows(yconv[...], clg_ref[...], clb_ref[...])
    mix_ref[0, :, ATTN_WIDTH:] = (y * jax.nn.sigmoid(y)).astype(BF16)

    for c in range(n_chunks):
        for j in range(n_pairs):
            res = jnp.dot(p_buf[c, j], window(vz, j, c), preferred_element_type=F32)
            out = res[:, 0:pair_w] * (1.0 / (res[:, pair_w:] + e_buf[c, j]))
            mix_ref[0, c * CHUNK:(c + 1) * CHUNK, j * pair_w:(j + 1) * pair_w] = out.astype(BF16)

    @pl.when(t < nt - 1)
    def _():
        if tm >= WINDOW:
            for i in range(2 * N_KV_HEADS):
                kz[i, 0:WINDOW, :] = kz[i, tm:tm + WINDOW, :]
                vz[i, 0:WINDOW, 0:KV_WIDTH] = vz[i, tm:tm + WINDOW, 0:KV_WIDTH]
            ucat[0:CONV_PAD, :] = ucat[tm:tm + CONV_PAD, :]


def _mixer(x, mod, w_in, bias, sink, w_dw, b_dw, cln_g, cln_b, k0, v0, u0, *, tm, masked):
    bsz, seq, d = x.shape
    assert seq % tm == 0 and tm % CHUNK == 0 and (tm >= WINDOW or seq == tm)
    nt = seq // tm
    kern = functools.partial(_mixer_kernel, tm=tm, masked=masked)
    const2 = lambda b, t: (0, 0)
    const3 = lambda b, t: (0, 0, 0)
    per_b = lambda b, t: (b, 0, 0)
    return pl.pallas_call(
        kern,
        grid=(bsz, nt),
        in_specs=[pl.BlockSpec((1, tm, d), lambda b, t: (b, t, 0)),
                  pl.BlockSpec((1, 6, d), per_b),
                  pl.BlockSpec((d, IN_WIDTH), const2),
                  pl.BlockSpec((WINDOW_CHUNKS + 1, N_HEADS // 2, CHUNK, 2 * KEYS), lambda b, t: (0, 0, 0, 0)),
                  pl.BlockSpec(memory_space=pltpu.SMEM),
                  pl.BlockSpec((CONV_K, SUBLANES, CONV_CH), const3),
                  pl.BlockSpec((1, CONV_CH), const2),
                  pl.BlockSpec((1, CONV_CH), const2),
                  pl.BlockSpec((1, CONV_CH), const2),
                  pl.BlockSpec((1, WINDOW, KV_WIDTH), per_b),
                  pl.BlockSpec((1, WINDOW, KV_WIDTH), per_b),
                  pl.BlockSpec((1, CONV_PAD, CONV_CH), per_b)],
        out_specs=[pl.BlockSpec((1, tm, MIX_WIDTH), lambda b, t: (b, t, 0)),
                   pl.BlockSpec((1, WINDOW, KV_WIDTH), per_b),
                   pl.BlockSpec((1, WINDOW, KV_WIDTH), per_b),
                   pl.BlockSpec((1, CONV_HIST, CONV_CH), per_b)],
        out_shape=[jax.ShapeDtypeStruct((bsz, seq, MIX_WIDTH), BF16),
                   jax.ShapeDtypeStruct((bsz, WINDOW, KV_WIDTH), F32),
                   jax.ShapeDtypeStruct((bsz, WINDOW, KV_WIDTH), F32),
                   jax.ShapeDtypeStruct((bsz, CONV_HIST, CONV_CH), F32)],
        scratch_shapes=[pltpu.VMEM((2 * N_KV_HEADS, WINDOW + tm, KV_WIDTH), BF16),
                        pltpu.VMEM((2 * N_KV_HEADS, WINDOW + tm, 2 * KV_WIDTH), BF16),
                        pltpu.VMEM((SUBLANES, CONV_PAD + tm, CONV_CH), F32),
                        pltpu.VMEM((tm, CONV_CH), F32),
                        pltpu.VMEM((tm // CHUNK, N_HEADS // 2, CHUNK, 2 * KEYS), F32),
                        pltpu.VMEM((tm // CHUNK, N_HEADS // 2, CHUNK, 2 * KEYS), BF16),
                        pltpu.VMEM((tm // CHUNK, N_HEADS // 2, CHUNK, 2 * HEAD_DIM), F32)],
        compiler_params=_cparams("parallel", "arbitrary"),
        name="mixer",
    )(x, mod, w_in, bias, sink, w_dw, b_dw, cln_g, cln_b, k0, v0, u0)


def _out_projection(mix_ref, wout_ref, *, bb, tt):
    return jnp.dot(mix_ref[...].reshape(bb * tt, MIX_WIDTH), wout_ref[...], preferred_element_type=F32)


def _post_kernel(mix, x_ref, mod_ref, g_ref, b_ref, wr_split_ref, wr_hi_ref, br_ref, x1_ref, h2e_ref, *, bb, tt):
    rows = bb * tt
    g1 = mod_ref[:, 2:3, :]
    sh2 = mod_ref[:, 3:4, :]
    sc2 = mod_ref[:, 4:5, :]
    r = ALPHA * x_ref[...] + (1.0 + g1) * mix.reshape(bb, tt, D_MODEL)
    x1 = _layer_norm_rows(r, g_ref[...], b_ref[...])
    x1_ref[...] = x1
    h2 = (x1 * (1.0 + sc2) + sh2).reshape(rows, D_MODEL)
    h2_hi = h2.astype(BF16)
    h2e_ref[:, :, 0:D_MODEL] = h2_hi.reshape(bb, tt, D_MODEL)

    h2_lo = (h2 - h2_hi.astype(F32)).astype(BF16)
    hi_terms = jnp.dot(h2_hi, wr_split_ref[...], preferred_element_type=F32)
    logits = (hi_terms[:, 0:ROUTE_LANES] + hi_terms[:, ROUTE_LANES:]
              + jnp.dot(h2_lo, wr_hi_ref[...], preferred_element_type=F32) + br_ref[...])
    lane = lax.broadcasted_iota(jnp.int32, (rows, ROUTE_LANES), 1).astype(F32)
    far = float(ROUTE_LANES)
    is_group = (lane >= N_EXPERTS) & (lane < N_EXPERTS + N_GROUPS)
    gl = jnp.where(is_group, logits, NEG_INF)
    gmax = jnp.max(gl, -1, keepdims=True)
    gidx = jnp.min(jnp.where(is_group & (gl == gmax), lane, far), -1, keepdims=True) - N_EXPERTS
    pg = 1.0 / jnp.sum(jnp.where(is_group, jnp.exp(gl - gmax), 0.0), -1, keepdims=True)
    in_group = (lane >= gidx * EXPERTS_PER_GROUP) & (lane < (gidx + 1) * EXPERTS_PER_GROUP)
    el = jnp.where(in_group, logits, NEG_INF)
    v1 = jnp.max(el, -1, keepdims=True)
    i1 = jnp.min(jnp.where(in_group & (el == v1), lane, far), -1, keepdims=True)
    rest = in_group & (lane != i1)
    el2 = jnp.where(rest, logits, NEG_INF)
    v2 = jnp.max(el2, -1, keepdims=True)
    i2 = jnp.min(jnp.where(rest & (el2 == v2), lane, far), -1, keepdims=True)
    e2 = jnp.exp(v2 - v1)
    w1 = pg / (1.0 + e2)
    w2 = pg * e2 / (1.0 + e2)
    comb = jnp.where(lane == i1, w1, jnp.where(lane == i2, w2, 0.0))

    c_hi = comb.astype(BF16).astype(F32)
    c_lo = (comb - c_hi).astype(BF16).astype(F32)
    rec = c_hi + pltpu.roll(c_lo, COMB_LO_LANE, 1) + jnp.where(lane == gidx + GROUP_LANE, 1.0, 0.0)
    h2e_ref[:, :, D_MODEL:] = rec.astype(BF16).reshape(bb, tt, ROUTE_LANES)


def _moe_rows(rows):
    return rows + N_GROUPS * MOE_ALIGN + max(MOE_WINDOWS)


def _moe_experts(h2e_ref, tri_ref, wg_ref, wu_ref, wd_ref, xs_ref, cw_ref, ys_ref, scatter_ref, other_stream,
                 *, bb, tt):
    rows = bb * tt
    srows = _moe_rows(rows)
    ext = h2e_ref[...].reshape(rows, D_MODEL + ROUTE_LANES)
    rec = ext[:, D_MODEL:]
    cum = jnp.dot(tri_ref[...], rec, preferred_element_type=F32)
    lane1 = lax.broadcasted_iota(jnp.int32, (1, ROUTE_LANES), 1)
    cnt = cum[rows - 1:rows, :]
    counts = [jnp.sum(jnp.where(lane1 == GROUP_LANE + g, cnt, 0.0)).astype(jnp.int32) for g in range(N_GROUPS)]
    pads = [((n + (MOE_ALIGN - 1)) // MOE_ALIGN) * MOE_ALIGN for n in counts]
    as_int = lambda flag: flag.astype(jnp.int32)
    ranks = []
    for g in range(N_GROUPS):
        ahead = [as_int(counts[h] >= counts[g]) if h < g else as_int(counts[h] > counts[g])
                 for h in range(N_GROUPS) if h != g]
        ranks.append(sum(ahead))
    offs = [sum(as_int(ranks[h] < ranks[g]) * pads[h] for h in range(N_GROUPS) if h != g) for g in range(N_GROUPS)]
    off_vec = jnp.zeros((1, ROUTE_LANES), F32)
    for g in range(N_GROUPS):
        off_vec = jnp.where(lane1 == GROUP_LANE + g, offs[g].astype(F32), off_vec)
    pick = lambda vals, r: sum(as_int(ranks[g] == r) * vals[g] for g in range(N_GROUPS))
    group_at = [pick(list(range(N_GROUPS)), r) for r in range(N_GROUPS)]
    off_at = [pick(offs, r) for r in range(N_GROUPS)]
    count_at = [pick(counts, r) for r in range(N_GROUPS)]

    lane = lax.broadcasted_iota(jnp.int32, (rows, ROUTE_LANES), 1)
    mine = (lane >= GROUP_LANE) & (lane < GROUP_LANE + N_GROUPS) & (rec.astype(F32) > 0.0)
    pos = jnp.sum(jnp.where(mine, cum - 1.0 + off_vec, 0.0), -1, keepdims=True)
    scatter_ref[...] = jnp.where(pos == lax.broadcasted_iota(jnp.int32, (rows, srows), 1).astype(F32),
                                 1.0, 0.0).astype(BF16)
    pos_row = jnp.transpose(jnp.broadcast_to(pos, (rows, ROUTE_LANES)))[0:1, :]
    grows = rows + N_GROUPS * MOE_ALIGN
    gather = jnp.where(lax.broadcasted_iota(jnp.int32, (grows, rows), 0).astype(F32) == pos_row, 1.0, 0.0).astype(BF16)
    xs_ref[0:grows, :] = jnp.dot(gather, ext[:, 0:D_MODEL], preferred_element_type=F32).astype(BF16)
    rs = jnp.dot(gather, rec, preferred_element_type=F32)
    cw_ref[0:grows, :] = rs + pltpu.roll(rs, ROUTE_LANES - COMB_LO_LANE, 1)

    def experts(r, r0):
        g, win = group_at[r], MOE_WINDOWS[r]
        xb = xs_ref[pl.ds(r0, win), :]
        cwb = cw_ref[pl.ds(r0, win), :]
        lane_b = lax.broadcasted_iota(jnp.int32, (win, ROUTE_LANES), 1)
        parts = []
        for j in range(EXPERTS_PER_GROUP):
            e = g * EXPERTS_PER_GROUP + j
            hg = jnp.dot(xb, wg_ref[e], preferred_element_type=F32)
            hu = jnp.dot(xb, wu_ref[e], preferred_element_type=F32)
            ce = jnp.sum(jnp.where(lane_b == e, cwb, 0.0), -1, keepdims=True)
            parts.append((hg * jax.nn.sigmoid(hg) * hu * ce).astype(BF16))
        act = jnp.concatenate(parts, axis=1)
        return jnp.dot(act, wd_ref[g], preferred_element_type=F32)

    for r in range(N_GROUPS):
        r0 = pl.multiple_of(off_at[r], MOE_ALIGN)
        ys_ref[pl.ds(r0, MOE_WINDOWS[r]), :] = experts(r, r0).astype(BF16)
        if r == 1:
            other_stream()

    def extra_block(r, i, carry):
        win = MOE_WINDOWS[r]
        r0 = pl.multiple_of(off_at[r] + i * win, MOE_ALIGN)
        row = r0 + lax.broadcasted_iota(jnp.int32, (win, 1), 0)
        old = ys_ref[pl.ds(r0, win), :].astype(F32)
        ys_ref[pl.ds(r0, win), :] = jnp.where(row < off_at[r] + count_at[r], experts(r, r0), old).astype(BF16)
        return carry

    def extra_blocks():
        for r in range(N_GROUPS):
            win = MOE_WINDOWS[r]
            lax.fori_loop(1, (count_at[r] + (win - 1)) // win, functools.partial(extra_block, r), 0)

    return extra_blocks


def _moe_finish(x1_ref, mod_ref, g_ref, b_ref, ys_ref, scatter_ref, y_ref, *, bb, tt):
    ff = jnp.dot(scatter_ref[...], ys_ref[...], preferred_element_type=F32)
    g2 = mod_ref[:, 5:6, :]
    r = ALPHA * x1_ref[...] + (1.0 + g2) * ff.reshape(bb, tt, D_MODEL)
    y_ref[...] = _layer_norm_rows(r, g_ref[...], b_ref[...])


def _row_parts(bb, tt, n):
    if bb == 1:
        pt = tt // n
        return [((slice(None), slice(i * pt, (i + 1) * pt)), 1, pt) for i in range(n)]
    pb = bb // n
    return [((slice(i * pb, (i + 1) * pb), slice(None)), pb, tt) for i in range(n)]


def _ffn_kernel(x_ref, mix_ref, mod_cur_ref, mod_prev_ref, wout_ref, g1_ref, b1_ref, wr_split_ref, wr_hi_ref,
                br_ref, tri_ref, wg_ref, wu_ref, wd_ref, g2_ref, b2_ref, y_ref,
                x1_buf, h2e_buf, xs_ref, cw_ref, ys_ref, scatter_ref, *, bb, tt):
    s = pl.program_id(0)
    cur = s % 2
    prev = 1 - cur

    @pl.when(s == 0)
    def _():
        ys_ref[...] = jnp.zeros_like(ys_ref)
        xs_ref[...] = jnp.zeros_like(xs_ref)
        cw_ref[...] = jnp.zeros_like(cw_ref)
        x1_buf[1] = jnp.zeros(x1_buf.shape[1:], F32)
        h2e_buf[...] = jnp.zeros(h2e_buf.shape, BF16)

    mix = _out_projection(mix_ref, wout_ref, bb=bb, tt=tt)
    post = functools.partial(_post_kernel, mix, x_ref, mod_cur_ref, g1_ref, b1_ref, wr_split_ref,
                             wr_hi_ref, br_ref, x1_buf.at[cur], h2e_buf, bb=bb, tt=tt)
    extra_blocks = _moe_experts(h2e_buf, tri_ref, wg_ref, wu_ref, wd_ref, xs_ref, cw_ref, ys_ref,
                                scatter_ref, post, bb=bb, tt=tt)
    extra_blocks()
    for i, (sl, pb, pt) in enumerate(_row_parts(bb, tt, TILE_PARTS)):
        part_rows = slice(i * pb * pt, (i + 1) * pb * pt)
        _moe_finish(x1_buf.at[prev].at[sl], mod_prev_ref.at[sl[0]], g2_ref, b2_ref, ys_ref, scatter_ref.at[part_rows],
                    y_ref.at[sl], bb=pb, tt=pt)


def _ffn(x, mixin, mod, p, *, bb, tt):
    bsz, seq, d = x.shape
    assert bsz % bb == 0 and seq % tt == 0
    rows = bb * tt
    srows = _moe_rows(rows)
    n_j = seq // tt
    n_tiles = (bsz // bb) * n_j
    tri = jnp.tril(jnp.ones((rows, rows), BF16))
    kern = functools.partial(_ffn_kernel, bb=bb, tt=tt)
    cur = lambda s: jnp.minimum(s, n_tiles - 1)
    prev = lambda s: jnp.maximum(s - 1, 0)
    const2 = lambda s: (0, 0)
    const3 = lambda s: (0, 0, 0)
    resident = pl.Buffered(1)
    return pl.pallas_call(
        kern,
        grid=(n_tiles + 1,),
        in_specs=[pl.BlockSpec((bb, tt, d), lambda s: (cur(s) // n_j, cur(s) % n_j, 0)),
                  pl.BlockSpec((bb, tt, MIX_WIDTH), lambda s: (cur(s) // n_j, cur(s) % n_j, 0)),
                  pl.BlockSpec((bb, 6, d), lambda s: (cur(s) // n_j, 0, 0)),
                  pl.BlockSpec((bb, 6, d), lambda s: (prev(s) // n_j, 0, 0)),
                  pl.BlockSpec((MIX_WIDTH, d), const2, pipeline_mode=resident),
                  pl.BlockSpec((1, d), const2),
                  pl.BlockSpec((1, d), const2),
                  pl.BlockSpec((d, 2 * ROUTE_LANES), const2, pipeline_mode=resident),
                  pl.BlockSpec((d, ROUTE_LANES), const2, pipeline_mode=resident),
                  pl.BlockSpec((1, ROUTE_LANES), const2),
                  pl.BlockSpec((rows, rows), const2, pipeline_mode=resident),
                  pl.BlockSpec(p["w_gate"].shape, const3, pipeline_mode=resident),
                  pl.BlockSpec(p["w_up"].shape, const3, pipeline_mode=resident),
                  pl.BlockSpec(p["w_down"].shape, const3, pipeline_mode=resident),
                  pl.BlockSpec((1, d), const2),
                  pl.BlockSpec((1, d), const2)],
        out_specs=pl.BlockSpec((bb, tt, d), lambda s: (prev(s) // n_j, prev(s) % n_j, 0)),
        out_shape=jax.ShapeDtypeStruct((bsz, seq, d), F32),
        scratch_shapes=[pltpu.VMEM((2, bb, tt, d), F32),
                        pltpu.VMEM((bb, tt, d + ROUTE_LANES), BF16),
                        pltpu.VMEM((srows, d), BF16),
                        pltpu.VMEM((srows, ROUTE_LANES), F32),
                        pltpu.VMEM((srows, d), BF16),
                        pltpu.VMEM((rows, srows), BF16)],
        compiler_params=_cparams("arbitrary"),
        name="ffn",
    )(x, mixin, mod, mod, p["w_out"], p["ln1_g"], p["ln1_b"], p["w_route_split"], p["w_route_hi"], p["b_route"],
      tri, p["w_gate"], p["w_up"], p["w_down"], p["ln2_g"], p["ln2_b"])


def _stream(x, mod, k0, v0, u0, p, *, mixer_tm, masked, bb, tt):
    mixin, nk, nv, nu = _mixer(x, mod, p["w_in"], p["bias"], p["sink"], p["w_dw"], p["b_dw"],
                               p["cln_g"], p["cln_b"], k0, v0, u0, tm=mixer_tm, masked=masked)
    y = _ffn(x, mixin, mod, p, bb=bb, tt=tt)
    bsz = x.shape[0]
    cache_shape = (1, bsz, WINDOW, N_KV_HEADS, HEAD_DIM)
    return y, nk.reshape(cache_shape), nv.reshape(cache_shape), nu[None]


def kernel(x_prompt, x_sample, cache_attn_k, cache_attn_v, state_conv, c_prompt, c_sample, rel_bias, w_ada, b_ada, w_in, attn_sinks, w_dw, b_dw, conv_ln_g, conv_ln_b, w_out, ln1_g, ln1_b, w_group, b_group, w_erouter, b_erouter, w_gate, w_up, w_down, ln2_g, ln2_b):
    bp = x_prompt.shape[0]
    bs = x_sample.shape[0]
    mod = _modulation(jnp.concatenate([c_prompt, c_sample], 0), w_ada[0], b_ada[0])
    mod = mod.reshape(bp + bs, 6, D_MODEL)

    w_route = jnp.concatenate([w_erouter[0].reshape(D_MODEL, N_EXPERTS), w_group[0]], 1)
    w_route = jnp.pad(w_route, ((0, 0), (0, ROUTE_LANES - N_EXPERTS - N_GROUPS)))
    b_route = jnp.concatenate([b_erouter[0].reshape(N_EXPERTS), b_group[0]])
    b_route = jnp.pad(b_route, (0, ROUTE_LANES - N_EXPERTS - N_GROUPS)).reshape(1, ROUTE_LANES)
    w_route_hi = w_route.astype(BF16)
    w_route_lo = (w_route - w_route_hi.astype(F32)).astype(BF16)

    p = {
        "w_in": w_in[0].astype(BF16),
        "bias": _relative_bias(rel_bias),
        "sink": attn_sinks[0],
        "w_dw": jnp.broadcast_to(w_dw[0][:, None, :], (CONV_K, SUBLANES, CONV_CH)),
        "b_dw": b_dw[0].reshape(1, CONV_CH),
        "cln_g": conv_ln_g[0].reshape(1, CONV_CH), "cln_b": conv_ln_b[0].reshape(1, CONV_CH),
        "w_out": w_out[0].astype(BF16),
        "ln1_g": ln1_g[0].reshape(1, D_MODEL), "ln1_b": ln1_b[0].reshape(1, D_MODEL),
        "w_route_split": jnp.concatenate([w_route_hi, w_route_lo], 1), "w_route_hi": w_route_hi,
        "b_route": b_route,
        "w_gate": w_gate[0].astype(BF16), "w_up": w_up[0].astype(BF16),
        "w_down": w_down[0].astype(BF16).reshape(N_GROUPS, EXPERTS_PER_GROUP * D_EXPERT, D_MODEL),
        "ln2_g": ln2_g[0].reshape(1, D_MODEL), "ln2_b": ln2_b[0].reshape(1, D_MODEL),
    }

    zk = jnp.zeros((bp, WINDOW, KV_WIDTH), F32)
    zu = jnp.zeros((bp, CONV_PAD, CONV_CH), F32)
    yp, pk, pv, pc = _stream(x_prompt, mod[:bp], zk, zk, zu, p, mixer_tm=512, masked=True, bb=1, tt=512)

    k0 = cache_attn_k[0].reshape(bs, WINDOW, KV_WIDTH)
    v0 = cache_attn_v[0].reshape(bs, WINDOW, KV_WIDTH)
    u0 = jnp.pad(state_conv[0], ((0, 0), (CONV_PAD - CONV_HIST, 0), (0, 0)))
    ts = x_sample.shape[1]
    ys, sk, sv, sc = _stream(x_sample, mod[bp:], k0, v0, u0, p, mixer_tm=ts, masked=False, bb=8, tt=ts)
    return yp, ys, pk, pv, pc, sk, sv, sc
```

```python
import functools
import math

import jax
import jax.numpy as jnp
from jax import lax
from jax.experimental import pallas as pl
from jax.experimental.pallas import tpu as pltpu

D_MODEL = 1024
CHUNK = 64
N_HEADS = 8
N_KV_HEADS = 2
HEAD_DIM = 64
Q_GROUP = N_HEADS // N_KV_HEADS
ATTN_WIDTH = N_HEADS * HEAD_DIM
KV_WIDTH = N_KV_HEADS * HEAD_DIM
WINDOW = 128
WINDOW_CHUNKS = WINDOW // CHUNK
KEYS = WINDOW + CHUNK
CONV_CH = D_MODEL // 2
CONV_K = 31
CONV_HIST = CONV_K - 1
CONV_PAD = 32
SUBLANES = 8
CONV_ROWS = 256
CONV_LANES = 128
MIX_WIDTH = ATTN_WIDTH + CONV_CH
IN_WIDTH = ATTN_WIDTH + 2 * KV_WIDTH + 2 * CONV_CH
N_BUCKETS = 32
MAX_DISTANCE = 128
N_GROUPS = 4
EXPERTS_PER_GROUP = 4
N_EXPERTS = N_GROUPS * EXPERTS_PER_GROUP
D_EXPERT = D_MODEL // 4
DEPTH = 1
ALPHA = (2 * DEPTH) ** 0.25
LN_EPS = 1e-5
NEG_INF = -1e30
LOG2E = math.log2(math.e)
ROUTE_LANES = 128
COMB_LO_LANE = 16
GROUP_LANE = 32
MOE_BLK = 160
MOE_EXTRA = 64
MOE_ALIGN = 16
TILE_PARTS = 2
VMEM_LIMIT = 56 * 1024 * 1024

BF16 = jnp.bfloat16
F32 = jnp.float32


def _cparams(*sem):
    return pltpu.CompilerParams(dimension_semantics=sem, vmem_limit_bytes=VMEM_LIMIT)


def _mod_kernel(c_ref, w_ref, b_ref, o_ref):
    c = c_ref[...]
    s = c * jax.nn.sigmoid(c)
    o_ref[...] = jnp.dot(s, w_ref[...], preferred_element_type=F32,
                         precision=lax.Precision.HIGHEST) + b_ref[...]


def _modulation(c, w_ada, b_ada):
    n, d = c.shape
    width = w_ada.shape[1]
    tn = 1536
    return pl.pallas_call(
        _mod_kernel,
        grid=(width // tn,),
        in_specs=[pl.BlockSpec((n, d), lambda j: (0, 0)),
                  pl.BlockSpec((d, tn), lambda j: (0, j)),
                  pl.BlockSpec((1, tn), lambda j: (0, j))],
        out_specs=pl.BlockSpec((n, tn), lambda j: (0, j)),
        out_shape=jax.ShapeDtypeStruct((n, width), F32),
        compiler_params=_cparams("arbitrary"),
        name="modulation",
    )(c, w_ada, b_ada.reshape(1, width))


def _t5_bucket(rel):
    nb = N_BUCKETS // 2
    max_exact = nb // 2
    ret = jnp.where(rel > 0, nb, 0)
    n = jnp.abs(rel)
    large = max_exact + (jnp.log(jnp.maximum(n, 1).astype(jnp.float32) / max_exact)
                         / math.log(MAX_DISTANCE / max_exact) * (nb - max_exact)).astype(jnp.int32)
    large = jnp.minimum(large, nb - 1)
    return ret + jnp.where(n < max_exact, n, large)


def _bias_kernel(table_ref, bucket_ref, o_ref):
    bucket = bucket_ref[...]
    col = lax.broadcasted_iota(jnp.int32, bucket.shape, 1)
    for h in range(N_HEADS):
        acc = jnp.zeros(bucket.shape, F32)
        for b in range(N_BUCKETS):
            acc = jnp.where(bucket == b, table_ref[b, h], acc)
        for v in range(WINDOW_CHUNKS + 1):
            o_ref[v, h] = jnp.where(col >= v * CHUNK, acc * LOG2E, NEG_INF)


def _relative_bias(table):
    rel = jnp.arange(KEYS)[None, :] - WINDOW - jnp.arange(CHUNK)[:, None]
    bucket = _t5_bucket(rel).astype(jnp.int32)
    nv = WINDOW_CHUNKS + 1
    bias = pl.pallas_call(
        _bias_kernel,
        in_specs=[pl.BlockSpec(memory_space=pltpu.SMEM),
                  pl.BlockSpec((CHUNK, KEYS), lambda: (0, 0))],
        out_specs=pl.BlockSpec((nv, N_HEADS, CHUNK, KEYS), lambda: (0, 0, 0, 0)),
        out_shape=jax.ShapeDtypeStruct((nv, N_HEADS, CHUNK, KEYS), F32),
        name="relative_bias",
    )(table, bucket)
    bias = bias.reshape(nv, N_HEADS // 2, 2, CHUNK, KEYS)
    return jnp.transpose(bias, (0, 1, 3, 2, 4)).reshape(nv, N_HEADS // 2, CHUNK, 2 * KEYS)


def _layer_norm_rows(x, g, b):
    mu = jnp.mean(x, -1, keepdims=True)
    xc = x - mu
    var = jnp.mean(xc * xc, -1, keepdims=True)
    return xc * lax.rsqrt(var + LN_EPS) * g + b


def _mixer_kernel(x_ref, mod_ref, win_ref, bias_ref, sink_ref, wdw_ref, bdw_ref, clg_ref, clb_ref,
                  k0_ref, v0_ref, u0_ref,
                  mix_ref, nk_ref, nv_ref, nu_ref,
                  kz, vz, ush, yconv, s_buf, p_buf, e_buf, *, tm, masked):
    t = pl.program_id(1)
    nt = pl.num_programs(1)
    n_chunks = tm // CHUNK
    ucat = ush.at[0]

    left = lax.broadcasted_iota(jnp.int32, (1, KV_WIDTH), 1) < HEAD_DIM

    def store_kv(row0, k, v):
        n = k.shape[0]
        for dst, val in ((kz, k), (vz, v)):
            swapped = pltpu.roll(val, HEAD_DIM, 1)
            dst[0, row0:row0 + n, 0:KV_WIDTH] = jnp.where(left, val, 0.0).astype(BF16)
            dst[1, row0:row0 + n, 0:KV_WIDTH] = jnp.where(left, 0.0, swapped).astype(BF16)
            dst[2, row0:row0 + n, 0:KV_WIDTH] = jnp.where(left, swapped, 0.0).astype(BF16)
            dst[3, row0:row0 + n, 0:KV_WIDTH] = jnp.where(left, 0.0, val).astype(BF16)

    @pl.when(t == 0)
    def _():
        ones_left = jnp.broadcast_to(jnp.where(left, 1.0, 0.0).astype(BF16), (WINDOW + tm, KV_WIDTH))
        ones_right = jnp.broadcast_to(jnp.where(left, 0.0, 1.0).astype(BF16), (WINDOW + tm, KV_WIDTH))
        for i in range(2 * N_KV_HEADS):
            vz[i, :, KV_WIDTH:] = ones_left if i % 2 == 0 else ones_right
        store_kv(0, k0_ref[0], v0_ref[0])
        ucat[0:CONV_PAD, :] = u0_ref[0]

    sh1 = mod_ref[0, 0:1, :]
    sc1 = mod_ref[0, 1:2, :]
    h = (x_ref[0] * (1.0 + sc1) + sh1).astype(BF16)
    o1 = ATTN_WIDTH
    o3 = o1 + 2 * KV_WIDTH
    q = (jnp.dot(h, win_ref[:, 0:o1], preferred_element_type=F32) * (HEAD_DIM ** -0.5 * LOG2E)).astype(BF16)
    kv = jnp.dot(h, win_ref[:, o1:o3], preferred_element_type=F32)
    ag = jnp.dot(h, win_ref[:, o3:IN_WIDTH], preferred_element_type=F32)
    u = ag[:, 0:CONV_CH] * jax.nn.sigmoid(ag[:, CONV_CH:])
    store_kv(WINDOW, kv[:, 0:KV_WIDTH], kv[:, KV_WIDTH:])
    ucat[CONV_PAD:CONV_PAD + tm, :] = u

    if tm >= WINDOW:
        nk_ref[0] = kv[tm - WINDOW:, 0:KV_WIDTH]
        nv_ref[0] = kv[tm - WINDOW:, KV_WIDTH:]
    else:
        nk_ref[0, 0:WINDOW - tm, :] = k0_ref[0, tm:WINDOW, :]
        nv_ref[0, 0:WINDOW - tm, :] = v0_ref[0, tm:WINDOW, :]
        nk_ref[0, WINDOW - tm:, :] = kv[:, 0:KV_WIDTH]
        nv_ref[0, WINDOW - tm:, :] = kv[:, KV_WIDTH:]
    nu_ref[0] = ucat[CONV_PAD + tm - CONV_HIST:CONV_PAD + tm, :]

    n_pairs = N_HEADS // 2
    pair_w = 2 * HEAD_DIM
    first_head = lax.broadcasted_iota(jnp.int32, (CHUNK, pair_w), 1) < HEAD_DIM
    nt_dims = (((1,), (1,)), ((), ()))

    def window(ref, j, c):
        kvh = (2 * j) // Q_GROUP
        rows = slice(c * CHUNK, c * CHUNK + KEYS)
        return jnp.concatenate([ref[2 * kvh, rows, :], ref[2 * kvh + 1, rows, :]], axis=0)

    for c in range(n_chunks):
        if masked and c < WINDOW_CHUNKS:
            variant = jnp.where(t == 0, WINDOW_CHUNKS - c, 0)
        else:
            variant = 0
        for j in range(n_pairs):
            qp = q[c * CHUNK:(c + 1) * CHUNK, j * pair_w:(j + 1) * pair_w]
            s_buf[c, j] = (lax.dot_general(qp, window(kz, j, c), nt_dims, preferred_element_type=F32)
                           + bias_ref[variant, j])

    for c in range(n_chunks):
        for j in range(n_pairs):
            s0 = s_buf[c, j, :, 0:pair_w]
            s1 = s_buf[c, j, :, pair_w:2 * pair_w]
            s2 = s_buf[c, j, :, 2 * pair_w:]
            sink_a = sink_ref[2 * j] * LOG2E
            sink_b = sink_ref[2 * j + 1] * LOG2E
            m_a = jnp.maximum(jnp.max(jnp.maximum(s0, jnp.where(first_head, s1, NEG_INF)), -1, keepdims=True), sink_a)
            m_b = jnp.maximum(jnp.max(jnp.maximum(jnp.where(first_head, NEG_INF, s1), s2), -1, keepdims=True), sink_b)
            p_buf[c, j, :, 0:pair_w] = jnp.exp2(s0 - m_a).astype(BF16)
            p_buf[c, j, :, pair_w:2 * pair_w] = jnp.exp2(s1 - jnp.where(first_head, m_a, m_b)).astype(BF16)
            p_buf[c, j, :, 2 * pair_w:] = jnp.exp2(s2 - m_b).astype(BF16)
            e_buf[c, j] = jnp.where(first_head, jnp.exp2(sink_a - m_a), jnp.exp2(sink_b - m_b))

    span = tm + CONV_PAD - SUBLANES
    for s in range(1, SUBLANES):
        ush[s, 0:span, :] = ucat[s:s + span, :]
    off = CONV_PAD - CONV_HIST
    rb = min(CONV_ROWS, tm)
    groups = rb // SUBLANES

    def conv_rows(lanes, r, carry):
        r0 = pl.multiple_of(r * rb, rb)
        partial = []
        for shift in range(SUBLANES):
            taps = [k for k in range(CONV_K) if (off + k) % SUBLANES == shift]
            top = max(off + k - shift for k in taps)
            slab = ush[shift, pl.ds(r0, rb + top), lanes].reshape(groups + top // SUBLANES, SUBLANES, CONV_LANES)
            acc = None
            for k in taps:
                g0 = (off + k - shift) // SUBLANES
                term = slab[g0:g0 + groups] * wdw_ref[k, :, lanes]
                acc = term if acc is None else acc + term
            partial.append(acc)
        while len(partial) > 1:
            partial = [a + b for a, b in zip(partial[0::2], partial[1::2])]
        yconv[pl.ds(r0, rb), lanes] = (partial[0] + bdw_ref[:, lanes]).reshape(rb, CONV_LANES)
        return carry

    for part in range(CONV_CH // CONV_LANES):
        lanes = slice(part * CONV_LANES, (part + 1) * CONV_LANES)
        lax.fori_loop(0, tm // rb, functools.partial(conv_rows, lanes), 0)
    y = _layer_norm_rows(yconv[...], clg_ref[...], clb_ref[...])
    mix_ref[0, :, ATTN_WIDTH:] = (y * jax.nn.sigmoid(y)).astype(BF16)

    for c in range(n_chunks):
        for j in range(n_pairs):
            res = jnp.dot(p_buf[c, j], window(vz, j, c), preferred_element_type=F32)
            out = res[:, 0:pair_w] * (1.0 / (res[:, pair_w:] + e_buf[c, j]))
            mix_ref[0, c * CHUNK:(c + 1) * CHUNK, j * pair_w:(j + 1) * pair_w] = out.astype(BF16)

    @pl.when(t < nt - 1)
    def _():
        if tm >= WINDOW:
            for i in range(2 * N_KV_HEADS):
                kz[i, 0:WINDOW, :] = kz[i, tm:tm + WINDOW, :]
                vz[i, 0:WINDOW, 0:KV_WIDTH] = vz[i, tm:tm + WINDOW, 0:KV_WIDTH]
            ucat[0:CONV_PAD, :] = ucat[tm:tm + CONV_PAD, :]


def _mixer(x, mod, w_in, bias, sink, w_dw, b_dw, cln_g, cln_b, k0, v0, u0, *, tm, masked):
    bsz, seq, d = x.shape
    assert seq % tm == 0 and tm % CHUNK == 0 and (tm >= WINDOW or seq == tm)
    nt = seq // tm
    kern = functools.partial(_mixer_kernel, tm=tm, masked=masked)
    const2 = lambda b, t: (0, 0)
    const3 = lambda b, t: (0, 0, 0)
    per_b = lambda b, t: (b, 0, 0)
    return pl.pallas_call(
        kern,
        grid=(bsz, nt),
        in_specs=[pl.BlockSpec((1, tm, d), lambda b, t: (b, t, 0)),
                  pl.BlockSpec((1, 6, d), per_b),
                  pl.BlockSpec((d, IN_WIDTH), const2),
                  pl.BlockSpec((WINDOW_CHUNKS + 1, N_HEADS // 2, CHUNK, 2 * KEYS), lambda b, t: (0, 0, 0, 0)),
                  pl.BlockSpec(memory_space=pltpu.SMEM),
                  pl.BlockSpec((CONV_K, SUBLANES, CONV_CH), const3),
                  pl.BlockSpec((1, CONV_CH), const2),
                  pl.BlockSpec((1, CONV_CH), const2),
                  pl.BlockSpec((1, CONV_CH), const2),
                  pl.BlockSpec((1, WINDOW, KV_WIDTH), per_b),
                  pl.BlockSpec((1, WINDOW, KV_WIDTH), per_b),
                  pl.BlockSpec((1, CONV_PAD, CONV_CH), per_b)],
        out_specs=[pl.BlockSpec((1, tm, MIX_WIDTH), lambda b, t: (b, t, 0)),
                   pl.BlockSpec((1, WINDOW, KV_WIDTH), per_b),
                   pl.BlockSpec((1, WINDOW, KV_WIDTH), per_b),
                   pl.BlockSpec((1, CONV_HIST, CONV_CH), per_b)],
        out_shape=[jax.ShapeDtypeStruct((bsz, seq, MIX_WIDTH), BF16),
                   jax.ShapeDtypeStruct((bsz, WINDOW, KV_WIDTH), F32),
                   jax.ShapeDtypeStruct((bsz, WINDOW, KV_WIDTH), F32),
                   jax.ShapeDtypeStruct((bsz, CONV_HIST, CONV_CH), F32)],
        scratch_shapes=[pltpu.VMEM((2 * N_KV_HEADS, WINDOW + tm, KV_WIDTH), BF16),
                        pltpu.VMEM((2 * N_KV_HEADS, WINDOW + tm, 2 * KV_WIDTH), BF16),
                        pltpu.VMEM((SUBLANES, CONV_PAD + tm, CONV_CH), F32),
                        pltpu.VMEM((tm, CONV_CH), F32),
                        pltpu.VMEM((tm // CHUNK, N_HEADS // 2, CHUNK, 2 * KEYS), F32),
                        pltpu.VMEM((tm // CHUNK, N_HEADS // 2, CHUNK, 2 * KEYS), BF16),
                        pltpu.VMEM((tm // CHUNK, N_HEADS // 2, CHUNK, 2 * HEAD_DIM), F32)],
        compiler_params=_cparams("parallel", "arbitrary"),
        name="mixer",
    )(x, mod, w_in, bias, sink, w_dw, b_dw, cln_g, cln_b, k0, v0, u0)


def _out_projection(mix_ref, wout_ref, *, bb, tt):
    return jnp.dot(mix_ref[...].reshape(bb * tt, MIX_WIDTH), wout_ref[...], preferred_element_type=F32)


def _post_kernel(mix, x_ref, mod_ref, g_ref, b_ref, wr_split_ref, wr_hi_ref, br_ref, x1_ref, h2e_ref, *, bb, tt):
    rows = bb * tt
    g1 = mod_ref[:, 2:3, :]
    sh2 = mod_ref[:, 3:4, :]
    sc2 = mod_ref[:, 4:5, :]
    r = ALPHA * x_ref[...] + (1.0 + g1) * mix.reshape(bb, tt, D_MODEL)
    x1 = _layer_norm_rows(r, g_ref[...], b_ref[...])
    x1_ref[...] = x1
    h2 = (x1 * (1.0 + sc2) + sh2).reshape(rows, D_MODEL)
    h2_hi = h2.astype(BF16)
    h2e_ref[:, :, 0:D_MODEL] = h2_hi.reshape(bb, tt, D_MODEL)

    h2_lo = (h2 - h2_hi.astype(F32)).astype(BF16)
    hi_terms = jnp.dot(h2_hi, wr_split_ref[...], preferred_element_type=F32)
    logits = (hi_terms[:, 0:ROUTE_LANES] + hi_terms[:, ROUTE_LANES:]
              + jnp.dot(h2_lo, wr_hi_ref[...], preferred_element_type=F32) + br_ref[...])
    lane = lax.broadcasted_iota(jnp.int32, (rows, ROUTE_LANES), 1).astype(F32)
    far = float(ROUTE_LANES)
    is_group = (lane >= N_EXPERTS) & (lane < N_EXPERTS + N_GROUPS)
    gl = jnp.where(is_group, logits, NEG_INF)
    gmax = jnp.max(gl, -1, keepdims=True)
    gidx = jnp.min(jnp.where(is_group & (gl == gmax), lane, far), -1, keepdims=True) - N_EXPERTS
    pg = 1.0 / jnp.sum(jnp.where(is_group, jnp.exp(gl - gmax), 0.0), -1, keepdims=True)
    in_group = (lane >= gidx * EXPERTS_PER_GROUP) & (lane < (gidx + 1) * EXPERTS_PER_GROUP)
    el = jnp.where(in_group, logits, NEG_INF)
    v1 = jnp.max(el, -1, keepdims=True)
    i1 = jnp.min(jnp.where(in_group & (el == v1), lane, far), -1, keepdims=True)
    rest = in_group & (lane != i1)
    el2 = jnp.where(rest, logits, NEG_INF)
    v2 = jnp.max(el2, -1, keepdims=True)
    i2 = jnp.min(jnp.where(rest & (el2 == v2), lane, far), -1, keepdims=True)
    e2 = jnp.exp(v2 - v1)
    w1 = pg / (1.0 + e2)
    w2 = pg * e2 / (1.0 + e2)
    comb = jnp.where(lane == i1, w1, jnp.where(lane == i2, w2, 0.0))

    c_hi = comb.astype(BF16).astype(F32)
    c_lo = (comb - c_hi).astype(BF16).astype(F32)
    rec = c_hi + pltpu.roll(c_lo, COMB_LO_LANE, 1) + jnp.where(lane == gidx + GROUP_LANE, 1.0, 0.0)
    h2e_ref[:, :, D_MODEL:] = rec.astype(BF16).reshape(bb, tt, ROUTE_LANES)


def _moe_rows(rows):
    return rows + N_GROUPS * MOE_ALIGN + MOE_BLK


def _moe_experts(h2e_ref, tri_ref, wg_ref, wu_ref, wd_ref, xs_ref, cw_ref, ys_ref, scatter_ref, other_stream,
                 *, bb, tt):
    rows = bb * tt
    srows = _moe_rows(rows)
    ext = h2e_ref[...].reshape(rows, D_MODEL + ROUTE_LANES)
    rec = ext[:, D_MODEL:]
    cum = jnp.dot(tri_ref[...], rec, preferred_element_type=F32)
    lane1 = lax.broadcasted_iota(jnp.int32, (1, ROUTE_LANES), 1)
    cnt = cum[rows - 1:rows, :]
    off = jnp.int32(0)
    offs, counts = [], []
    off_vec = jnp.zeros((1, ROUTE_LANES), F32)
    for g in range(N_GROUPS):
        n_g = jnp.sum(jnp.where(lane1 == GROUP_LANE + g, cnt, 0.0)).astype(jnp.int32)
        offs.append(off)
        counts.append(n_g)
        off_vec = jnp.where(lane1 == GROUP_LANE + g, off.astype(F32), off_vec)
        off = off + ((n_g + (MOE_ALIGN - 1)) // MOE_ALIGN) * MOE_ALIGN

    lane = lax.broadcasted_iota(jnp.int32, (rows, ROUTE_LANES), 1)
    mine = (lane >= GROUP_LANE) & (lane < GROUP_LANE + N_GROUPS) & (rec.astype(F32) > 0.0)
    pos = jnp.sum(jnp.where(mine, cum - 1.0 + off_vec, 0.0), -1, keepdims=True)
    scatter_ref[...] = jnp.where(pos == lax.broadcasted_iota(jnp.int32, (rows, srows), 1).astype(F32),
                                 1.0, 0.0).astype(BF16)
    pos_row = jnp.transpose(jnp.broadcast_to(pos, (rows, ROUTE_LANES)))[0:1, :]
    grows = rows + N_GROUPS * MOE_ALIGN
    gather = jnp.where(lax.broadcasted_iota(jnp.int32, (grows, rows), 0).astype(F32) == pos_row, 1.0, 0.0).astype(BF16)
    xs_ref[0:grows, :] = jnp.dot(gather, ext[:, 0:D_MODEL], preferred_element_type=F32).astype(BF16)
    rs = jnp.dot(gather, rec, preferred_element_type=F32)
    cw_ref[0:grows, :] = rs + pltpu.roll(rs, ROUTE_LANES - COMB_LO_LANE, 1)

    def experts(g, r0, win):
        xb = xs_ref[pl.ds(r0, win), :]
        cwb = cw_ref[pl.ds(r0, win), :]
        parts = []
        for j in range(EXPERTS_PER_GROUP):
            e = g * EXPERTS_PER_GROUP + j
            hg = jnp.dot(xb, wg_ref[e], preferred_element_type=F32)
            hu = jnp.dot(xb, wu_ref[e], preferred_element_type=F32)
            parts.append((hg * jax.nn.sigmoid(hg) * hu * cwb[:, e:e + 1]).astype(BF16))
        act = jnp.concatenate(parts, axis=1)
        return jnp.dot(act, wd_ref[g], preferred_element_type=F32)

    for g in range(N_GROUPS):
        r0 = pl.multiple_of(offs[g], MOE_ALIGN)
        ys_ref[pl.ds(r0, MOE_BLK), :] = experts(g, r0, MOE_BLK).astype(BF16)
        if g == 1:
            other_stream()

    def extra_block(g, i, carry):
        r0 = pl.multiple_of(offs[g] + MOE_BLK + i * MOE_EXTRA, MOE_ALIGN)
        row = r0 + lax.broadcasted_iota(jnp.int32, (MOE_EXTRA, 1), 0)
        old = ys_ref[pl.ds(r0, MOE_EXTRA), :].astype(F32)
        new = experts(g, r0, MOE_EXTRA)
        ys_ref[pl.ds(r0, MOE_EXTRA), :] = jnp.where(row < offs[g] + counts[g], new, old).astype(BF16)
        return carry

    def extra_blocks():
        for g in range(N_GROUPS):
            n_extra = (jnp.maximum(counts[g] - MOE_BLK, 0) + (MOE_EXTRA - 1)) // MOE_EXTRA
            lax.fori_loop(0, n_extra, functools.partial(extra_block, g), 0)

    return extra_blocks


def _moe_finish(x1_ref, mod_ref, g_ref, b_ref, ys_ref, scatter_ref, y_ref, *, bb, tt):
    ff = jnp.dot(scatter_ref[...], ys_ref[...], preferred_element_type=F32)
    g2 = mod_ref[:, 5:6, :]
    r = ALPHA * x1_ref[...] + (1.0 + g2) * ff.reshape(bb, tt, D_MODEL)
    y_ref[...] = _layer_norm_rows(r, g_ref[...], b_ref[...])


def _row_parts(bb, tt, n):
    if bb == 1:
        pt = tt // n
        return [((slice(None), slice(i * pt, (i + 1) * pt)), 1, pt) for i in range(n)]
    pb = bb // n
    return [((slice(i * pb, (i + 1) * pb), slice(None)), pb, tt) for i in range(n)]


def _ffn_kernel(x_ref, mix_ref, mod_cur_ref, mod_prev_ref, wout_ref, g1_ref, b1_ref, wr_split_ref, wr_hi_ref,
                br_ref, tri_ref, wg_ref, wu_ref, wd_ref, g2_ref, b2_ref, y_ref,
                x1_buf, h2e_buf, xs_ref, cw_ref, ys_ref, scatter_ref, *, bb, tt):
    s = pl.program_id(0)
    cur = s % 2
    prev = 1 - cur

    @pl.when(s == 0)
    def _():
        ys_ref[...] = jnp.zeros_like(ys_ref)
        xs_ref[...] = jnp.zeros_like(xs_ref)
        cw_ref[...] = jnp.zeros_like(cw_ref)
        x1_buf[1] = jnp.zeros(x1_buf.shape[1:], F32)
        h2e_buf[...] = jnp.zeros(h2e_buf.shape, BF16)

    mix = _out_projection(mix_ref, wout_ref, bb=bb, tt=tt)
    post = functools.partial(_post_kernel, mix, x_ref, mod_cur_ref, g1_ref, b1_ref, wr_split_ref,
                             wr_hi_ref, br_ref, x1_buf.at[cur], h2e_buf, bb=bb, tt=tt)
    extra_blocks = _moe_experts(h2e_buf, tri_ref, wg_ref, wu_ref, wd_ref, xs_ref, cw_ref, ys_ref,
                                scatter_ref, post, bb=bb, tt=tt)
    extra_blocks()
    for i, (sl, pb, pt) in enumerate(_row_parts(bb, tt, TILE_PARTS)):
        part_rows = slice(i * pb * pt, (i + 1) * pb * pt)
        _moe_finish(x1_buf.at[prev].at[sl], mod_prev_ref.at[sl[0]], g2_ref, b2_ref, ys_ref, scatter_ref.at[part_rows],
                    y_ref.at[sl], bb=pb, tt=pt)


def _ffn(x, mixin, mod, p, *, bb, tt):
    bsz, seq, d = x.shape
    assert bsz % bb == 0 and seq % tt == 0
    rows = bb * tt
    srows = _moe_rows(rows)
    n_j = seq // tt
    n_tiles = (bsz // bb) * n_j
    tri = jnp.tril(jnp.ones((rows, rows), BF16))
    kern = functools.partial(_ffn_kernel, bb=bb, tt=tt)
    cur = lambda s: jnp.minimum(s, n_tiles - 1)
    prev = lambda s: jnp.maximum(s - 1, 0)
    const2 = lambda s: (0, 0)
    const3 = lambda s: (0, 0, 0)
    resident = pl.Buffered(1)
    return pl.pallas_call(
        kern,
        grid=(n_tiles + 1,),
        in_specs=[pl.BlockSpec((bb, tt, d), lambda s: (cur(s) // n_j, cur(s) % n_j, 0)),
                  pl.BlockSpec((bb, tt, MIX_WIDTH), lambda s: (cur(s) // n_j, cur(s) % n_j, 0)),
                  pl.BlockSpec((bb, 6, d), lambda s: (cur(s) // n_j, 0, 0)),
                  pl.BlockSpec((bb, 6, d), lambda s: (prev(s) // n_j, 0, 0)),
                  pl.BlockSpec((MIX_WIDTH, d), const2, pipeline_mode=resident),
                  pl.BlockSpec((1, d), const2),
                  pl.BlockSpec((1, d), const2),
                  pl.BlockSpec((d, 2 * ROUTE_LANES), const2, pipeline_mode=resident),
                  pl.BlockSpec((d, ROUTE_LANES), const2, pipeline_mode=resident),
                  pl.BlockSpec((1, ROUTE_LANES), const2),
                  pl.BlockSpec((rows, rows), const2, pipeline_mode=resident),
                  pl.BlockSpec(p["w_gate"].shape, const3, pipeline_mode=resident),
                  pl.BlockSpec(p["w_up"].shape, const3, pipeline_mode=resident),
                  pl.BlockSpec(p["w_down"].shape, const3, pipeline_mode=resident),
                  pl.BlockSpec((1, d), const2),
                  pl.BlockSpec((1, d), const2)],
        out_specs=pl.BlockSpec((bb, tt, d), lambda s: (prev(s) // n_j, prev(s) % n_j, 0)),
        out_shape=jax.ShapeDtypeStruct((bsz, seq, d), F32),
        scratch_shapes=[pltpu.VMEM((2, bb, tt, d), F32),
                        pltpu.VMEM((bb, tt, d + ROUTE_LANES), BF16),
                        pltpu.VMEM((srows, d), BF16),
                        pltpu.VMEM((srows, ROUTE_LANES), F32),
                        pltpu.VMEM((srows, d), BF16),
                        pltpu.VMEM((rows, srows), BF16)],
        compiler_params=_cparams("arbitrary"),
        name="ffn",
    )(x, mixin, mod, mod, p["w_out"], p["ln1_g"], p["ln1_b"], p["w_route_split"], p["w_route_hi"], p["b_route"],
      tri, p["w_gate"], p["w_up"], p["w_down"], p["ln2_g"], p["ln2_b"])


def _stream(x, mod, k0, v0, u0, p, *, mixer_tm, masked, bb, tt):
    mixin, nk, nv, nu = _mixer(x, mod, p["w_in"], p["bias"], p["sink"], p["w_dw"], p["b_dw"],
                               p["cln_g"], p["cln_b"], k0, v0, u0, tm=mixer_tm, masked=masked)
    y = _ffn(x, mixin, mod, p, bb=bb, tt=tt)
    bsz = x.shape[0]
    cache_shape = (1, bsz, WINDOW, N_KV_HEADS, HEAD_DIM)
    return y, nk.reshape(cache_shape), nv.reshape(cache_shape), nu[None]


def kernel(x_prompt, x_sample, cache_attn_k, cache_attn_v, state_conv, c_prompt, c_sample, rel_bias, w_ada, b_ada, w_in, attn_sinks, w_dw, b_dw, conv_ln_g, conv_ln_b, w_out, ln1_g, ln1_b, w_group, b_group, w_erouter, b_erouter, w_gate, w_up, w_down, ln2_g, ln2_b):
    bp = x_prompt.shape[0]
    bs = x_sample.shape[0]
    mod = _modulation(jnp.concatenate([c_prompt, c_sample], 0), w_ada[0], b_ada[0])
    mod = mod.reshape(bp + bs, 6, D_MODEL)

    w_route = jnp.concatenate([w_erouter[0].reshape(D_MODEL, N_EXPERTS), w_group[0]], 1)
    w_route = jnp.pad(w_route, ((0, 0), (0, ROUTE_LANES - N_EXPERTS - N_GROUPS)))
    b_route = jnp.concatenate([b_erouter[0].reshape(N_EXPERTS), b_group[0]])
    b_route = jnp.pad(b_route, (0, ROUTE_LANES - N_EXPERTS - N_GROUPS)).reshape(1, ROUTE_LANES)
    w_route_hi = w_route.astype(BF16)
    w_route_lo = (w_route - w_route_hi.astype(F32)).astype(BF16)

    p = {
        "w_in": w_in[0].astype(BF16),
        "bias": _relative_bias(rel_bias),
        "sink": attn_sinks[0],
        "w_dw": jnp.broadcast_to(w_dw[0][:, None, :], (CONV_K, SUBLANES, CONV_CH)),
        "b_dw": b_dw[0].reshape(1, CONV_CH),
        "cln_g": conv_ln_g[0].reshape(1, CONV_CH), "cln_b": conv_ln_b[0].reshape(1, CONV_CH),
        "w_out": w_out[0].astype(BF16),
        "ln1_g": ln1_g[0].reshape(1, D_MODEL), "ln1_b": ln1_b[0].reshape(1, D_MODEL),
        "w_route_split": jnp.concatenate([w_route_hi, w_route_lo], 1), "w_route_hi": w_route_hi,
        "b_route": b_route,
        "w_gate": w_gate[0].astype(BF16), "w_up": w_up[0].astype(BF16),
        "w_down": w_down[0].astype(BF16).reshape(N_GROUPS, EXPERTS_PER_GROUP * D_EXPERT, D_MODEL),
        "ln2_g": ln2_g[0].reshape(1, D_MODEL), "ln2_b": ln2_b[0].reshape(1, D_MODEL),
    }

    zk = jnp.zeros((bp, WINDOW, KV_WIDTH), F32)
    zu = jnp.zeros((bp, CONV_PAD, CONV_CH), F32)
    yp, pk, pv, pc = _stream(x_prompt, mod[:bp], zk, zk, zu, p, mixer_tm=512, masked=True, bb=1, tt=512)

    k0 = cache_attn_k[0].reshape(bs, WINDOW, KV_WIDTH)
    v0 = cache_attn_v[0].reshape(bs, WINDOW, KV_WIDTH)
    u0 = jnp.pad(state_conv[0], ((0, 0), (CONV_PAD - CONV_HIST, 0), (0, 0)))
    ts = x_sample.shape[1]
    ys, sk, sv, sc = _stream(x_sample, mod[bp:], k0, v0, u0, p, mixer_tm=ts, masked=False, bb=8, tt=ts)
    return yp, ys, pk, pv, pc, sk, sv, sc
```

```python
import functools
import math

import jax
import jax.numpy as jnp
from jax import lax
from jax.experimental import pallas as pl
from jax.experimental.pallas import tpu as pltpu

D_MODEL = 1024
CHUNK = 64
N_HEADS = 8
N_KV_HEADS = 2
HEAD_DIM = 64
Q_GROUP = N_HEADS // N_KV_HEADS
ATTN_WIDTH = N_HEADS * HEAD_DIM
KV_WIDTH = N_KV_HEADS * HEAD_DIM
WINDOW = 128
WINDOW_CHUNKS = WINDOW // CHUNK
KEYS = WINDOW + CHUNK
CONV_CH = D_MODEL // 2
CONV_K = 31
CONV_HIST = CONV_K - 1
CONV_PAD = 32
SUBLANES = 8
CONV_ROWS = 256
CONV_LANES = 128
MIX_WIDTH = ATTN_WIDTH + CONV_CH
IN_WIDTH = ATTN_WIDTH + 2 * KV_WIDTH + 2 * CONV_CH
N_BUCKETS = 32
MAX_DISTANCE = 128
N_GROUPS = 4
EXPERTS_PER_GROUP = 4
N_EXPERTS = N_GROUPS * EXPERTS_PER_GROUP
D_EXPERT = D_MODEL // 4
DEPTH = 1
ALPHA = (2 * DEPTH) ** 0.25
LN_EPS = 1e-5
NEG_INF = -1e30
LOG2E = math.log2(math.e)
ROUTE_LANES = 128
COMB_LO_LANE = 16
GROUP_LANE = 32
MOE_BLK = 160
MOE_ALIGN = 16
TILE_PARTS = 2
VMEM_LIMIT = 56 * 1024 * 1024

BF16 = jnp.bfloat16
F32 = jnp.float32


def _cparams(*sem):
    return pltpu.CompilerParams(dimension_semantics=sem, vmem_limit_bytes=VMEM_LIMIT)


def _mod_kernel(c_ref, w_ref, b_ref, o_ref):
    c = c_ref[...]
    s = c * jax.nn.sigmoid(c)
    o_ref[...] = jnp.dot(s, w_ref[...], preferred_element_type=F32,
                         precision=lax.Precision.HIGHEST) + b_ref[...]


def _modulation(c, w_ada, b_ada):
    n, d = c.shape
    width = w_ada.shape[1]
    tn = 1536
    return pl.pallas_call(
        _mod_kernel,
        grid=(width // tn,),
        in_specs=[pl.BlockSpec((n, d), lambda j: (0, 0)),
                  pl.BlockSpec((d, tn), lambda j: (0, j)),
                  pl.BlockSpec((1, tn), lambda j: (0, j))],
        out_specs=pl.BlockSpec((n, tn), lambda j: (0, j)),
        out_shape=jax.ShapeDtypeStruct((n, width), F32),
        compiler_params=_cparams("arbitrary"),
        name="modulation",
    )(c, w_ada, b_ada.reshape(1, width))


def _t5_bucket(rel):
    nb = N_BUCKETS // 2
    max_exact = nb // 2
    ret = jnp.where(rel > 0, nb, 0)
    n = jnp.abs(rel)
    large = max_exact + (jnp.log(jnp.maximum(n, 1).astype(jnp.float32) / max_exact)
                         / math.log(MAX_DISTANCE / max_exact) * (nb - max_exact)).astype(jnp.int32)
    large = jnp.minimum(large, nb - 1)
    return ret + jnp.where(n < max_exact, n, large)


def _bias_kernel(table_ref, bucket_ref, o_ref):
    bucket = bucket_ref[...]
    col = lax.broadcasted_iota(jnp.int32, bucket.shape, 1)
    for h in range(N_HEADS):
        acc = jnp.zeros(bucket.shape, F32)
        for b in range(N_BUCKETS):
            acc = jnp.where(bucket == b, table_ref[b, h], acc)
        for v in range(WINDOW_CHUNKS + 1):
            o_ref[v, h] = jnp.where(col >= v * CHUNK, acc * LOG2E, NEG_INF)


def _relative_bias(table):
    rel = jnp.arange(KEYS)[None, :] - WINDOW - jnp.arange(CHUNK)[:, None]
    bucket = _t5_bucket(rel).astype(jnp.int32)
    nv = WINDOW_CHUNKS + 1
    bias = pl.pallas_call(
        _bias_kernel,
        in_specs=[pl.BlockSpec(memory_space=pltpu.SMEM),
                  pl.BlockSpec((CHUNK, KEYS), lambda: (0, 0))],
        out_specs=pl.BlockSpec((nv, N_HEADS, CHUNK, KEYS), lambda: (0, 0, 0, 0)),
        out_shape=jax.ShapeDtypeStruct((nv, N_HEADS, CHUNK, KEYS), F32),
        name="relative_bias",
    )(table, bucket)
    bias = bias.reshape(nv, N_HEADS // 2, 2, CHUNK, KEYS)
    return jnp.transpose(bias, (0, 1, 3, 2, 4)).reshape(nv, N_HEADS // 2, CHUNK, 2 * KEYS)


def _layer_norm_rows(x, g, b):
    mu = jnp.mean(x, -1, keepdims=True)
    xc = x - mu
    var = jnp.mean(xc * xc, -1, keepdims=True)
    return xc * lax.rsqrt(var + LN_EPS) * g + b


def _mixer_kernel(x_ref, mod_ref, win_ref, bias_ref, sink_ref, wdw_ref, bdw_ref, clg_ref, clb_ref,
                  k0_ref, v0_ref, u0_ref,
                  mix_ref, nk_ref, nv_ref, nu_ref,
                  kz, vz, ush, yconv, s_buf, p_buf, e_buf, *, tm, masked):
    t = pl.program_id(1)
    nt = pl.num_programs(1)
    n_chunks = tm // CHUNK
    ucat = ush.at[0]

    left = lax.broadcasted_iota(jnp.int32, (1, KV_WIDTH), 1) < HEAD_DIM

    def store_kv(row0, k, v):
        n = k.shape[0]
        for dst, val in ((kz, k), (vz, v)):
            swapped = pltpu.roll(val, HEAD_DIM, 1)
            dst[0, row0:row0 + n, 0:KV_WIDTH] = jnp.where(left, val, 0.0).astype(BF16)
            dst[1, row0:row0 + n, 0:KV_WIDTH] = jnp.where(left, 0.0, swapped).astype(BF16)
            dst[2, row0:row0 + n, 0:KV_WIDTH] = jnp.where(left, swapped, 0.0).astype(BF16)
            dst[3, row0:row0 + n, 0:KV_WIDTH] = jnp.where(left, 0.0, val).astype(BF16)

    @pl.when(t == 0)
    def _():
        ones_left = jnp.broadcast_to(jnp.where(left, 1.0, 0.0).astype(BF16), (WINDOW + tm, KV_WIDTH))
        ones_right = jnp.broadcast_to(jnp.where(left, 0.0, 1.0).astype(BF16), (WINDOW + tm, KV_WIDTH))
        for i in range(2 * N_KV_HEADS):
            vz[i, :, KV_WIDTH:] = ones_left if i % 2 == 0 else ones_right
        store_kv(0, k0_ref[0], v0_ref[0])
        ucat[0:CONV_PAD, :] = u0_ref[0]

    sh1 = mod_ref[0, 0:1, :]
    sc1 = mod_ref[0, 1:2, :]
    h = (x_ref[0] * (1.0 + sc1) + sh1).astype(BF16)
    o1 = ATTN_WIDTH
    o3 = o1 + 2 * KV_WIDTH
    q = (jnp.dot(h, win_ref[:, 0:o1], preferred_element_type=F32) * (HEAD_DIM ** -0.5 * LOG2E)).astype(BF16)
    kv = jnp.dot(h, win_ref[:, o1:o3], preferred_element_type=F32)
    ag = jnp.dot(h, win_ref[:, o3:IN_WIDTH], preferred_element_type=F32)
    u = ag[:, 0:CONV_CH] * jax.nn.sigmoid(ag[:, CONV_CH:])
    store_kv(WINDOW, kv[:, 0:KV_WIDTH], kv[:, KV_WIDTH:])
    ucat[CONV_PAD:CONV_PAD + tm, :] = u

    if tm >= WINDOW:
        nk_ref[0] = kv[tm - WINDOW:, 0:KV_WIDTH]
        nv_ref[0] = kv[tm - WINDOW:, KV_WIDTH:]
    else:
        nk_ref[0, 0:WINDOW - tm, :] = k0_ref[0, tm:WINDOW, :]
        nv_ref[0, 0:WINDOW - tm, :] = v0_ref[0, tm:WINDOW, :]
        nk_ref[0, WINDOW - tm:, :] = kv[:, 0:KV_WIDTH]
        nv_ref[0, WINDOW - tm:, :] = kv[:, KV_WIDTH:]
    nu_ref[0] = ucat[CONV_PAD + tm - CONV_HIST:CONV_PAD + tm, :]

    n_pairs = N_HEADS // 2
    pair_w = 2 * HEAD_DIM
    first_head = lax.broadcasted_iota(jnp.int32, (CHUNK, pair_w), 1) < HEAD_DIM
    nt_dims = (((1,), (1,)), ((), ()))

    def window(ref, j, c):
        kvh = (2 * j) // Q_GROUP
        rows = slice(c * CHUNK, c * CHUNK + KEYS)
        return jnp.concatenate([ref[2 * kvh, rows, :], ref[2 * kvh + 1, rows, :]], axis=0)

    for c in range(n_chunks):
        if masked and c < WINDOW_CHUNKS:
            variant = jnp.where(t == 0, WINDOW_CHUNKS - c, 0)
        else:
            variant = 0
        for j in range(n_pairs):
            qp = q[c * CHUNK:(c + 1) * CHUNK, j * pair_w:(j + 1) * pair_w]
            s_buf[c, j] = (lax.dot_general(qp, window(kz, j, c), nt_dims, preferred_element_type=F32)
                           + bias_ref[variant, j])

    for c in range(n_chunks):
        for j in range(n_pairs):
            s0 = s_buf[c, j, :, 0:pair_w]
            s1 = s_buf[c, j, :, pair_w:2 * pair_w]
            s2 = s_buf[c, j, :, 2 * pair_w:]
            sink_a = sink_ref[2 * j] * LOG2E
            sink_b = sink_ref[2 * j + 1] * LOG2E
            m_a = jnp.maximum(jnp.max(jnp.maximum(s0, jnp.where(first_head, s1, NEG_INF)), -1, keepdims=True), sink_a)
            m_b = jnp.maximum(jnp.max(jnp.maximum(jnp.where(first_head, NEG_INF, s1), s2), -1, keepdims=True), sink_b)
            p_buf[c, j, :, 0:pair_w] = jnp.exp2(s0 - m_a).astype(BF16)
            p_buf[c, j, :, pair_w:2 * pair_w] = jnp.exp2(s1 - jnp.where(first_head, m_a, m_b)).astype(BF16)
            p_buf[c, j, :, 2 * pair_w:] = jnp.exp2(s2 - m_b).astype(BF16)
            e_buf[c, j] = jnp.where(first_head, jnp.exp2(sink_a - m_a), jnp.exp2(sink_b - m_b))

    span = tm + CONV_PAD - SUBLANES
    for s in range(1, SUBLANES):
        ush[s, 0:span, :] = ucat[s:s + span, :]
    off = CONV_PAD - CONV_HIST
    rb = min(CONV_ROWS, tm)
    groups = rb // SUBLANES

    def conv_rows(lanes, r, carry):
        r0 = pl.multiple_of(r * rb, rb)
        partial = []
        for shift in range(SUBLANES):
            taps = [k for k in range(CONV_K) if (off + k) % SUBLANES == shift]
            top = max(off + k - shift for k in taps)
            slab = ush[shift, pl.ds(r0, rb + top), lanes].reshape(groups + top // SUBLANES, SUBLANES, CONV_LANES)
            acc = None
            for k in taps:
                g0 = (off + k - shift) // SUBLANES
                term = slab[g0:g0 + groups] * wdw_ref[k, :, lanes]
                acc = term if acc is None else acc + term
            partial.append(acc)
        while len(partial) > 1:
            partial = [a + b for a, b in zip(partial[0::2], partial[1::2])]
        yconv[pl.ds(r0, rb), lanes] = (partial[0] + bdw_ref[:, lanes]).reshape(rb, CONV_LANES)
        return carry

    for part in range(CONV_CH // CONV_LANES):
        lanes = slice(part * CONV_LANES, (part + 1) * CONV_LANES)
        lax.fori_loop(0, tm // rb, functools.partial(conv_rows, lanes), 0)
    y = _layer_norm_rows(yconv[...], clg_ref[...], clb_ref[...])
    mix_ref[0, :, ATTN_WIDTH:] = (y * jax.nn.sigmoid(y)).astype(BF16)

    for c in range(n_chunks):
        for j in range(n_pairs):
            res = jnp.dot(p_buf[c, j], window(vz, j, c), preferred_element_type=F32)
            out = res[:, 0:pair_w] * (1.0 / (res[:, pair_w:] + e_buf[c, j]))
            mix_ref[0, c * CHUNK:(c + 1) * CHUNK, j * pair_w:(j + 1) * pair_w] = out.astype(BF16)

    @pl.when(t < nt - 1)
    def _():
        if tm >= WINDOW:
            for i in range(2 * N_KV_HEADS):
                kz[i, 0:WINDOW, :] = kz[i, tm:tm + WINDOW, :]
                vz[i, 0:WINDOW, 0:KV_WIDTH] = vz[i, tm:tm + WINDOW, 0:KV_WIDTH]
            ucat[0:CONV_PAD, :] = ucat[tm:tm + CONV_PAD, :]


def _mixer(x, mod, w_in, bias, sink, w_dw, b_dw, cln_g, cln_b, k0, v0, u0, *, tm, masked):
    bsz, seq, d = x.shape
    assert seq % tm == 0 and tm % CHUNK == 0 and (tm >= WINDOW or seq == tm)
    nt = seq // tm
    kern = functools.partial(_mixer_kernel, tm=tm, masked=masked)
    const2 = lambda b, t: (0, 0)
    const3 = lambda b, t: (0, 0, 0)
    per_b = lambda b, t: (b, 0, 0)
    return pl.pallas_call(
        kern,
        grid=(bsz, nt),
        in_specs=[pl.BlockSpec((1, tm, d), lambda b, t: (b, t, 0)),
                  pl.BlockSpec((1, 6, d), per_b),
                  pl.BlockSpec((d, IN_WIDTH), const2),
                  pl.BlockSpec((WINDOW_CHUNKS + 1, N_HEADS // 2, CHUNK, 2 * KEYS), lambda b, t: (0, 0, 0, 0)),
                  pl.BlockSpec(memory_space=pltpu.SMEM),
                  pl.BlockSpec((CONV_K, SUBLANES, CONV_CH), const3),
                  pl.BlockSpec((1, CONV_CH), const2),
                  pl.BlockSpec((1, CONV_CH), const2),
                  pl.BlockSpec((1, CONV_CH), const2),
                  pl.BlockSpec((1, WINDOW, KV_WIDTH), per_b),
                  pl.BlockSpec((1, WINDOW, KV_WIDTH), per_b),
                  pl.BlockSpec((1, CONV_PAD, CONV_CH), per_b)],
        out_specs=[pl.BlockSpec((1, tm, MIX_WIDTH), lambda b, t: (b, t, 0)),
                   pl.BlockSpec((1, WINDOW, KV_WIDTH), per_b),
                   pl.BlockSpec((1, WINDOW, KV_WIDTH), per_b),
                   pl.BlockSpec((1, CONV_HIST, CONV_CH), per_b)],
        out_shape=[jax.ShapeDtypeStruct((bsz, seq, MIX_WIDTH), BF16),
                   jax.ShapeDtypeStruct((bsz, WINDOW, KV_WIDTH), F32),
                   jax.ShapeDtypeStruct((bsz, WINDOW, KV_WIDTH), F32),
                   jax.ShapeDtypeStruct((bsz, CONV_HIST, CONV_CH), F32)],
        scratch_shapes=[pltpu.VMEM((2 * N_KV_HEADS, WINDOW + tm, KV_WIDTH), BF16),
                        pltpu.VMEM((2 * N_KV_HEADS, WINDOW + tm, 2 * KV_WIDTH), BF16),
                        pltpu.VMEM((SUBLANES, CONV_PAD + tm, CONV_CH), F32),
                        pltpu.VMEM((tm, CONV_CH), F32),
                        pltpu.VMEM((tm // CHUNK, N_HEADS // 2, CHUNK, 2 * KEYS), F32),
                        pltpu.VMEM((tm // CHUNK, N_HEADS // 2, CHUNK, 2 * KEYS), BF16),
                        pltpu.VMEM((tm // CHUNK, N_HEADS // 2, CHUNK, 2 * HEAD_DIM), F32)],
        compiler_params=_cparams("parallel", "arbitrary"),
        name="mixer",
    )(x, mod, w_in, bias, sink, w_dw, b_dw, cln_g, cln_b, k0, v0, u0)


def _out_projection(mix_ref, wout_ref, *, bb, tt):
    return jnp.dot(mix_ref[...].reshape(bb * tt, MIX_WIDTH), wout_ref[...], preferred_element_type=F32)


def _post_kernel(mix, x_ref, mod_ref, g_ref, b_ref, wr_split_ref, br_ref, x1_ref, h2e_ref, *, bb, tt):
    rows = bb * tt
    g1 = mod_ref[:, 2:3, :]
    sh2 = mod_ref[:, 3:4, :]
    sc2 = mod_ref[:, 4:5, :]
    r = ALPHA * x_ref[...] + (1.0 + g1) * mix.reshape(bb, tt, D_MODEL)
    x1 = _layer_norm_rows(r, g_ref[...], b_ref[...])
    x1_ref[...] = x1
    h2 = (x1 * (1.0 + sc2) + sh2).reshape(rows, D_MODEL)
    h2_hi = h2.astype(BF16)
    h2e_ref[:, :, 0:D_MODEL] = h2_hi.reshape(bb, tt, D_MODEL)

    hi_terms = jnp.dot(h2_hi, wr_split_ref[...], preferred_element_type=F32)
    logits = hi_terms[:, 0:ROUTE_LANES] + hi_terms[:, ROUTE_LANES:] + br_ref[...]
    lane = lax.broadcasted_iota(jnp.int32, (rows, ROUTE_LANES), 1).astype(F32)
    far = float(ROUTE_LANES)
    is_group = (lane >= N_EXPERTS) & (lane < N_EXPERTS + N_GROUPS)
    gl = jnp.where(is_group, logits, NEG_INF)
    gmax = jnp.max(gl, -1, keepdims=True)
    gidx = jnp.min(jnp.where(is_group & (gl == gmax), lane, far), -1, keepdims=True) - N_EXPERTS
    pg = 1.0 / jnp.sum(jnp.where(is_group, jnp.exp(gl - gmax), 0.0), -1, keepdims=True)
    in_group = (lane >= gidx * EXPERTS_PER_GROUP) & (lane < (gidx + 1) * EXPERTS_PER_GROUP)
    el = jnp.where(in_group, logits, NEG_INF)
    v1 = jnp.max(el, -1, keepdims=True)
    i1 = jnp.min(jnp.where(in_group & (el == v1), lane, far), -1, keepdims=True)
    rest = in_group & (lane != i1)
    el2 = jnp.where(rest, logits, NEG_INF)
    v2 = jnp.max(el2, -1, keepdims=True)
    i2 = jnp.min(jnp.where(rest & (el2 == v2), lane, far), -1, keepdims=True)
    e2 = jnp.exp(v2 - v1)
    w1 = pg / (1.0 + e2)
    w2 = pg * e2 / (1.0 + e2)
    comb = jnp.where(lane == i1, w1, jnp.where(lane == i2, w2, 0.0))

    c_hi = comb.astype(BF16).astype(F32)
    c_lo = (comb - c_hi).astype(BF16).astype(F32)
    rec = c_hi + pltpu.roll(c_lo, COMB_LO_LANE, 1) + jnp.where(lane == gidx + GROUP_LANE, 1.0, 0.0)
    h2e_ref[:, :, D_MODEL:] = rec.astype(BF16).reshape(bb, tt, ROUTE_LANES)


def _moe_rows(rows):
    return rows + N_GROUPS * MOE_ALIGN + MOE_BLK


def _moe_experts(h2e_ref, tri_ref, wg_ref, wu_ref, wd_ref, xs_ref, cw_ref, ys_ref, scatter_ref, other_stream,
                 *, bb, tt):
    rows = bb * tt
    srows = _moe_rows(rows)
    ext = h2e_ref[...].reshape(rows, D_MODEL + ROUTE_LANES)
    rec = ext[:, D_MODEL:]
    cum = jnp.dot(tri_ref[...], rec, preferred_element_type=F32)
    lane1 = lax.broadcasted_iota(jnp.int32, (1, ROUTE_LANES), 1)
    cnt = cum[rows - 1:rows, :]
    off = jnp.int32(0)
    offs, counts = [], []
    off_vec = jnp.zeros((1, ROUTE_LANES), F32)
    for g in range(N_GROUPS):
        n_g = jnp.sum(jnp.where(lane1 == GROUP_LANE + g, cnt, 0.0)).astype(jnp.int32)
        offs.append(off)
        counts.append(n_g)
        off_vec = jnp.where(lane1 == GROUP_LANE + g, off.astype(F32), off_vec)
        off = off + ((n_g + (MOE_ALIGN - 1)) // MOE_ALIGN) * MOE_ALIGN

    lane = lax.broadcasted_iota(jnp.int32, (rows, ROUTE_LANES), 1)
    mine = (lane >= GROUP_LANE) & (lane < GROUP_LANE + N_GROUPS) & (rec.astype(F32) > 0.0)
    pos = jnp.sum(jnp.where(mine, cum - 1.0 + off_vec, 0.0), -1, keepdims=True)
    scatter_ref[...] = jnp.where(pos == lax.broadcasted_iota(jnp.int32, (rows, srows), 1).astype(F32),
                                 1.0, 0.0).astype(BF16)
    pos_row = jnp.transpose(jnp.broadcast_to(pos, (rows, ROUTE_LANES)))[0:1, :]
    grows = rows + N_GROUPS * MOE_ALIGN
    gather = jnp.where(lax.broadcasted_iota(jnp.int32, (grows, rows), 0).astype(F32) == pos_row, 1.0, 0.0).astype(BF16)
    xs_ref[0:grows, :] = jnp.dot(gather, ext[:, 0:D_MODEL], preferred_element_type=F32).astype(BF16)
    rs = jnp.dot(gather, rec, preferred_element_type=F32)
    cw_ref[0:grows, :] = rs + pltpu.roll(rs, ROUTE_LANES - COMB_LO_LANE, 1)

    def experts(g, r0):
        xb = xs_ref[pl.ds(r0, MOE_BLK), :]
        cwb = cw_ref[pl.ds(r0, MOE_BLK), :]
        parts = []
        for j in range(EXPERTS_PER_GROUP):
            e = g * EXPERTS_PER_GROUP + j
            hg = jnp.dot(xb, wg_ref[e], preferred_element_type=F32)
            hu = jnp.dot(xb, wu_ref[e], preferred_element_type=F32)
            parts.append((hg * jax.nn.sigmoid(hg) * hu * cwb[:, e:e + 1]).astype(BF16))
        act = jnp.concatenate(parts, axis=1)
        return jnp.dot(act, wd_ref[g], preferred_element_type=F32)

    for g in range(N_GROUPS):
        r0 = pl.multiple_of(offs[g], MOE_ALIGN)
        ys_ref[pl.ds(r0, MOE_BLK), :] = experts(g, r0).astype(BF16)
        if g == 1:
            other_stream()

    def extra_block(g, i, carry):
        r0 = pl.multiple_of(offs[g] + i * MOE_BLK, MOE_ALIGN)
        row = r0 + lax.broadcasted_iota(jnp.int32, (MOE_BLK, 1), 0)
        old = ys_ref[pl.ds(r0, MOE_BLK), :].astype(F32)
        ys_ref[pl.ds(r0, MOE_BLK), :] = jnp.where(row < offs[g] + counts[g], experts(g, r0), old).astype(BF16)
        return carry

    def extra_blocks():
        for g in range(N_GROUPS):
            lax.fori_loop(1, (counts[g] + (MOE_BLK - 1)) // MOE_BLK, functools.partial(extra_block, g), 0)

    return extra_blocks


def _moe_finish(x1_ref, mod_ref, g_ref, b_ref, ys_ref, scatter_ref, y_ref, *, bb, tt):
    ff = jnp.dot(scatter_ref[...], ys_ref[...], preferred_element_type=F32)
    g2 = mod_ref[:, 5:6, :]
    r = ALPHA * x1_ref[...] + (1.0 + g2) * ff.reshape(bb, tt, D_MODEL)
    y_ref[...] = _layer_norm_rows(r, g_ref[...], b_ref[...])


def _row_parts(bb, tt, n):
    if bb == 1:
        pt = tt // n
        return [((slice(None), slice(i * pt, (i + 1) * pt)), 1, pt) for i in range(n)]
    pb = bb // n
    return [((slice(i * pb, (i + 1) * pb), slice(None)), pb, tt) for i in range(n)]


def _ffn_kernel(x_ref, mix_ref, mod_cur_ref, mod_prev_ref, wout_ref, g1_ref, b1_ref, wr_split_ref,
                br_ref, tri_ref, wg_ref, wu_ref, wd_ref, g2_ref, b2_ref, y_ref,
                x1_buf, h2e_buf, xs_ref, cw_ref, ys_ref, scatter_ref, *, bb, tt):
    s = pl.program_id(0)
    cur = s % 2
    prev = 1 - cur

    @pl.when(s == 0)
    def _():
        ys_ref[...] = jnp.zeros_like(ys_ref)
        xs_ref[...] = jnp.zeros_like(xs_ref)
        cw_ref[...] = jnp.zeros_like(cw_ref)
        x1_buf[1] = jnp.zeros(x1_buf.shape[1:], F32)
        h2e_buf[...] = jnp.zeros(h2e_buf.shape, BF16)

    mix = _out_projection(mix_ref, wout_ref, bb=bb, tt=tt)
    post = functools.partial(_post_kernel, mix, x_ref, mod_cur_ref, g1_ref, b1_ref, wr_split_ref,
                             br_ref, x1_buf.at[cur], h2e_buf, bb=bb, tt=tt)
    extra_blocks = _moe_experts(h2e_buf, tri_ref, wg_ref, wu_ref, wd_ref, xs_ref, cw_ref, ys_ref,
                                scatter_ref, post, bb=bb, tt=tt)
    extra_blocks()
    for i, (sl, pb, pt) in enumerate(_row_parts(bb, tt, TILE_PARTS)):
        part_rows = slice(i * pb * pt, (i + 1) * pb * pt)
        _moe_finish(x1_buf.at[prev].at[sl], mod_prev_ref.at[sl[0]], g2_ref, b2_ref, ys_ref, scatter_ref.at[part_rows],
                    y_ref.at[sl], bb=pb, tt=pt)


def _ffn(x, mixin, mod, p, *, bb, tt):
    bsz, seq, d = x.shape
    assert bsz % bb == 0 and seq % tt == 0
    rows = bb * tt
    srows = _moe_rows(rows)
    n_j = seq // tt
    n_tiles = (bsz // bb) * n_j
    tri = jnp.tril(jnp.ones((rows, rows), BF16))
    kern = functools.partial(_ffn_kernel, bb=bb, tt=tt)
    cur = lambda s: jnp.minimum(s, n_tiles - 1)
    prev = lambda s: jnp.maximum(s - 1, 0)
    const2 = lambda s: (0, 0)
    const3 = lambda s: (0, 0, 0)
    resident = pl.Buffered(1)
    return pl.pallas_call(
        kern,
        grid=(n_tiles + 1,),
        in_specs=[pl.BlockSpec((bb, tt, d), lambda s: (cur(s) // n_j, cur(s) % n_j, 0)),
                  pl.BlockSpec((bb, tt, MIX_WIDTH), lambda s: (cur(s) // n_j, cur(s) % n_j, 0)),
                  pl.BlockSpec((bb, 6, d), lambda s: (cur(s) // n_j, 0, 0)),
                  pl.BlockSpec((bb, 6, d), lambda s: (prev(s) // n_j, 0, 0)),
                  pl.BlockSpec((MIX_WIDTH, d), const2, pipeline_mode=resident),
                  pl.BlockSpec((1, d), const2),
                  pl.BlockSpec((1, d), const2),
                  pl.BlockSpec((d, 2 * ROUTE_LANES), const2, pipeline_mode=resident),
                  pl.BlockSpec((1, ROUTE_LANES), const2),
                  pl.BlockSpec((rows, rows), const2, pipeline_mode=resident),
                  pl.BlockSpec(p["w_gate"].shape, const3, pipeline_mode=resident),
                  pl.BlockSpec(p["w_up"].shape, const3, pipeline_mode=resident),
                  pl.BlockSpec(p["w_down"].shape, const3, pipeline_mode=resident),
                  pl.BlockSpec((1, d), const2),
                  pl.BlockSpec((1, d), const2)],
        out_specs=pl.BlockSpec((bb, tt, d), lambda s: (prev(s) // n_j, prev(s) % n_j, 0)),
        out_shape=jax.ShapeDtypeStruct((bsz, seq, d), F32),
        scratch_shapes=[pltpu.VMEM((2, bb, tt, d), F32),
                        pltpu.VMEM((bb, tt, d + ROUTE_LANES), BF16),
                        pltpu.VMEM((srows, d), BF16),
                        pltpu.VMEM((srows, ROUTE_LANES), F32),
                        pltpu.VMEM((srows, d), BF16),
                        pltpu.VMEM((rows, srows), BF16)],
        compiler_params=_cparams("arbitrary"),
        name="ffn",
    )(x, mixin, mod, mod, p["w_out"], p["ln1_g"], p["ln1_b"], p["w_route_split"], p["b_route"],
      tri, p["w_gate"], p["w_up"], p["w_down"], p["ln2_g"], p["ln2_b"])


def _stream(x, mod, k0, v0, u0, p, *, mixer_tm, masked, bb, tt):
    mixin, nk, nv, nu = _mixer(x, mod, p["w_in"], p["bias"], p["sink"], p["w_dw"], p["b_dw"],
                               p["cln_g"], p["cln_b"], k0, v0, u0, tm=mixer_tm, masked=masked)
    y = _ffn(x, mixin, mod, p, bb=bb, tt=tt)
    bsz = x.shape[0]
    cache_shape = (1, bsz, WINDOW, N_KV_HEADS, HEAD_DIM)
    return y, nk.reshape(cache_shape), nv.reshape(cache_shape), nu[None]


def kernel(x_prompt, x_sample, cache_attn_k, cache_attn_v, state_conv, c_prompt, c_sample, rel_bias, w_ada, b_ada, w_in, attn_sinks, w_dw, b_dw, conv_ln_g, conv_ln_b, w_out, ln1_g, ln1_b, w_group, b_group, w_erouter, b_erouter, w_gate, w_up, w_down, ln2_g, ln2_b):
    bp = x_prompt.shape[0]
    bs = x_sample.shape[0]
    mod = _modulation(jnp.concatenate([c_prompt, c_sample], 0), w_ada[0], b_ada[0])
    mod = mod.reshape(bp + bs, 6, D_MODEL)

    w_route = jnp.concatenate([w_erouter[0].reshape(D_MODEL, N_EXPERTS), w_group[0]], 1)
    w_route = jnp.pad(w_route, ((0, 0), (0, ROUTE_LANES - N_EXPERTS - N_GROUPS)))
    b_route = jnp.concatenate([b_erouter[0].reshape(N_EXPERTS), b_group[0]])
    b_route = jnp.pad(b_route, (0, ROUTE_LANES - N_EXPERTS - N_GROUPS)).reshape(1, ROUTE_LANES)
    w_route_hi = w_route.astype(BF16)
    w_route_lo = (w_route - w_route_hi.astype(F32)).astype(BF16)

    p = {
        "w_in": w_in[0].astype(BF16),
        "bias": _relative_bias(rel_bias),
        "sink": attn_sinks[0],
        "w_dw": jnp.broadcast_to(w_dw[0][:, None, :], (CONV_K, SUBLANES, CONV_CH)),
        "b_dw": b_dw[0].reshape(1, CONV_CH),
        "cln_g": conv_ln_g[0].reshape(1, CONV_CH), "cln_b": conv_ln_b[0].reshape(1, CONV_CH),
        "w_out": w_out[0].astype(BF16),
        "ln1_g": ln1_g[0].reshape(1, D_MODEL), "ln1_b": ln1_b[0].reshape(1, D_MODEL),
        "w_route_split": jnp.concatenate([w_route_hi, w_route_lo], 1),
        "b_route": b_route,
        "w_gate": w_gate[0].astype(BF16), "w_up": w_up[0].astype(BF16),
        "w_down": w_down[0].astype(BF16).reshape(N_GROUPS, EXPERTS_PER_GROUP * D_EXPERT, D_MODEL),
        "ln2_g": ln2_g[0].reshape(1, D_MODEL), "ln2_b": ln2_b[0].reshape(1, D_MODEL),
    }

    zk = jnp.zeros((bp, WINDOW, KV_WIDTH), F32)
    zu = jnp.zeros((bp, CONV_PAD, CONV_CH), F32)
    yp, pk, pv, pc = _stream(x_prompt, mod[:bp], zk, zk, zu, p, mixer_tm=512, masked=True, bb=1, tt=512)

    k0 = cache_attn_k[0].reshape(bs, WINDOW, KV_WIDTH)
    v0 = cache_attn_v[0].reshape(bs, WINDOW, KV_WIDTH)
    u0 = jnp.pad(state_conv[0], ((0, 0), (CONV_PAD - CONV_HIST, 0), (0, 0)))
    ts = x_sample.shape[1]
    ys, sk, sv, sc = _stream(x_sample, mod[bp:], k0, v0, u0, p, mixer_tm=ts, masked=False, bb=8, tt=ts)
    return yp, ys, pk, pv, pc, sk, sv, sc
```

```python
import functools
import math

import jax
import jax.numpy as jnp
from jax import lax
from jax.experimental import pallas as pl
from jax.experimental.pallas import tpu as pltpu

D_MODEL = 1024
CHUNK = 64
N_HEADS = 8
N_KV_HEADS = 2
HEAD_DIM = 64
Q_GROUP = N_HEADS // N_KV_HEADS
ATTN_WIDTH = N_HEADS * HEAD_DIM
KV_WIDTH = N_KV_HEADS * HEAD_DIM
WINDOW = 128
WINDOW_CHUNKS = WINDOW // CHUNK
KEYS = WINDOW + CHUNK
CONV_CH = D_MODEL // 2
CONV_K = 31
CONV_HIST = CONV_K - 1
CONV_PAD = 32
SUBLANES = 8
CONV_ROWS = 256
CONV_LANES = 128
MIX_WIDTH = ATTN_WIDTH + CONV_CH
IN_WIDTH = ATTN_WIDTH + 2 * KV_WIDTH + 2 * CONV_CH
N_BUCKETS = 32
MAX_DISTANCE = 128
N_GROUPS = 4
EXPERTS_PER_GROUP = 4
N_EXPERTS = N_GROUPS * EXPERTS_PER_GROUP
D_EXPERT = D_MODEL // 4
DEPTH = 1
ALPHA = (2 * DEPTH) ** 0.25
LN_EPS = 1e-5
NEG_INF = -1e30
LOG2E = math.log2(math.e)
ROUTE_LANES = 128
COMB_LO_LANE = 16
GROUP_LANE = 32
X_SLOTS = 3
MOE_BLK = 160
MOE_ALIGN = 16
TILE_PARTS = 2
VMEM_LIMIT = 56 * 1024 * 1024

BF16 = jnp.bfloat16
F32 = jnp.float32


def _cparams(*sem):
    return pltpu.CompilerParams(dimension_semantics=sem, vmem_limit_bytes=VMEM_LIMIT)


def _mod_kernel(c_ref, w_ref, b_ref, o_ref):
    c = c_ref[...]
    s = c * jax.nn.sigmoid(c)
    o_ref[...] = jnp.dot(s, w_ref[...], preferred_element_type=F32,
                         precision=lax.Precision.HIGHEST) + b_ref[...]


def _modulation(c, w_ada, b_ada):
    n, d = c.shape
    width = w_ada.shape[1]
    tn = 1536
    return pl.pallas_call(
        _mod_kernel,
        grid=(width // tn,),
        in_specs=[pl.BlockSpec((n, d), lambda j: (0, 0)),
                  pl.BlockSpec((d, tn), lambda j: (0, j)),
                  pl.BlockSpec((1, tn), lambda j: (0, j))],
        out_specs=pl.BlockSpec((n, tn), lambda j: (0, j)),
        out_shape=jax.ShapeDtypeStruct((n, width), F32),
        compiler_params=_cparams("arbitrary"),
        name="modulation",
    )(c, w_ada, b_ada.reshape(1, width))


def _t5_bucket(rel):
    nb = N_BUCKETS // 2
    max_exact = nb // 2
    ret = jnp.where(rel > 0, nb, 0)
    n = jnp.abs(rel)
    large = max_exact + (jnp.log(jnp.maximum(n, 1).astype(jnp.float32) / max_exact)
                         / math.log(MAX_DISTANCE / max_exact) * (nb - max_exact)).astype(jnp.int32)
    large = jnp.minimum(large, nb - 1)
    return ret + jnp.where(n < max_exact, n, large)


def _bias_kernel(table_ref, bucket_ref, o_ref):
    bucket = bucket_ref[...]
    col = lax.broadcasted_iota(jnp.int32, bucket.shape, 1)
    for h in range(N_HEADS):
        acc = jnp.zeros(bucket.shape, F32)
        for b in range(N_BUCKETS):
            acc = jnp.where(bucket == b, table_ref[b, h], acc)
        for v in range(WINDOW_CHUNKS + 1):
            o_ref[v, h] = jnp.where(col >= v * CHUNK, acc * LOG2E, NEG_INF)


def _relative_bias(table):
    rel = jnp.arange(KEYS)[None, :] - WINDOW - jnp.arange(CHUNK)[:, None]
    bucket = _t5_bucket(rel).astype(jnp.int32)
    nv = WINDOW_CHUNKS + 1
    bias = pl.pallas_call(
        _bias_kernel,
        in_specs=[pl.BlockSpec(memory_space=pltpu.SMEM),
                  pl.BlockSpec((CHUNK, KEYS), lambda: (0, 0))],
        out_specs=pl.BlockSpec((nv, N_HEADS, CHUNK, KEYS), lambda: (0, 0, 0, 0)),
        out_shape=jax.ShapeDtypeStruct((nv, N_HEADS, CHUNK, KEYS), F32),
        name="relative_bias",
    )(table, bucket)
    bias = bias.reshape(nv, N_HEADS // 2, 2, CHUNK, KEYS)
    return jnp.transpose(bias, (0, 1, 3, 2, 4)).reshape(nv, N_HEADS // 2, CHUNK, 2 * KEYS)


def _layer_norm_rows(x, g, b):
    mu = jnp.mean(x, -1, keepdims=True)
    xc = x - mu
    var = jnp.mean(xc * xc, -1, keepdims=True)
    return xc * lax.rsqrt(var + LN_EPS) * g + b


def _mixer_kernel(x_ref, mod_ref, win_ref, bias_ref, sink_ref, wdw_ref, bdw_ref, clg_ref, clb_ref,
                  k0_ref, v0_ref, u0_ref,
                  mix_ref, nk_ref, nv_ref, nu_ref,
                  kz, vz, ush, yconv, s_buf, p_buf, e_buf, x_buf, x_sem, *, tm, nt, n_tiles, masked):
    t = pl.program_id(1)
    n_chunks = tm // CHUNK
    ucat = ush.at[0]

    step = pl.program_id(0) * nt + t

    def x_copy(tile, slot):
        rows = pl.ds(pl.multiple_of((tile % nt) * tm, tm), tm)
        return pltpu.make_async_copy(x_ref.at[tile // nt, rows, :], x_buf.at[slot], x_sem.at[slot])

    @pl.when(step == 0)
    def _():
        for first in range(min(X_SLOTS - 1, n_tiles)):
            x_copy(first, first).start()

    ahead = step + (X_SLOTS - 1)

    @pl.when(ahead < n_tiles)
    def _():
        x_copy(ahead, ahead % X_SLOTS).start()

    slot = step % X_SLOTS
    x_copy(step, slot).wait()

    left = lax.broadcasted_iota(jnp.int32, (1, KV_WIDTH), 1) < HEAD_DIM

    def store_kv(row0, k, v):
        n = k.shape[0]
        for dst, val in ((kz, k), (vz, v)):
            swapped = pltpu.roll(val, HEAD_DIM, 1)
            dst[0, row0:row0 + n, 0:KV_WIDTH] = jnp.where(left, val, 0.0).astype(BF16)
            dst[1, row0:row0 + n, 0:KV_WIDTH] = jnp.where(left, 0.0, swapped).astype(BF16)
            dst[2, row0:row0 + n, 0:KV_WIDTH] = jnp.where(left, swapped, 0.0).astype(BF16)
            dst[3, row0:row0 + n, 0:KV_WIDTH] = jnp.where(left, 0.0, val).astype(BF16)

    @pl.when(t == 0)
    def _():
        ones_left = jnp.broadcast_to(jnp.where(left, 1.0, 0.0).astype(BF16), (WINDOW + tm, KV_WIDTH))
        ones_right = jnp.broadcast_to(jnp.where(left, 0.0, 1.0).astype(BF16), (WINDOW + tm, KV_WIDTH))
        for i in range(2 * N_KV_HEADS):
            vz[i, :, KV_WIDTH:] = ones_left if i % 2 == 0 else ones_right
        store_kv(0, k0_ref[0], v0_ref[0])
        ucat[0:CONV_PAD, :] = u0_ref[0]

    sh1 = mod_ref[0, 0:1, :]
    sc1 = mod_ref[0, 1:2, :]
    h = (x_buf[slot] * (1.0 + sc1) + sh1).astype(BF16)
    o1 = ATTN_WIDTH
    o3 = o1 + 2 * KV_WIDTH
    q = (jnp.dot(h, win_ref[:, 0:o1], preferred_element_type=F32) * (HEAD_DIM ** -0.5 * LOG2E)).astype(BF16)
    kv = jnp.dot(h, win_ref[:, o1:o3], preferred_element_type=F32)
    ag = jnp.dot(h, win_ref[:, o3:IN_WIDTH], preferred_element_type=F32)
    u = ag[:, 0:CONV_CH] * jax.nn.sigmoid(ag[:, CONV_CH:])
    store_kv(WINDOW, kv[:, 0:KV_WIDTH], kv[:, KV_WIDTH:])
    ucat[CONV_PAD:CONV_PAD + tm, :] = u

    if tm >= WINDOW:
        nk_ref[0] = kv[tm - WINDOW:, 0:KV_WIDTH]
        nv_ref[0] = kv[tm - WINDOW:, KV_WIDTH:]
    else:
        nk_ref[0, 0:WINDOW - tm, :] = k0_ref[0, tm:WINDOW, :]
        nv_ref[0, 0:WINDOW - tm, :] = v0_ref[0, tm:WINDOW, :]
        nk_ref[0, WINDOW - tm:, :] = kv[:, 0:KV_WIDTH]
        nv_ref[0, WINDOW - tm:, :] = kv[:, KV_WIDTH:]
    nu_ref[0] = ucat[CONV_PAD + tm - CONV_HIST:CONV_PAD + tm, :]

    n_pairs = N_HEADS // 2
    pair_w = 2 * HEAD_DIM
    first_head = lax.broadcasted_iota(jnp.int32, (CHUNK, pair_w), 1) < HEAD_DIM
    nt_dims = (((1,), (1,)), ((), ()))

    def window(ref, j, c):
        kvh = (2 * j) // Q_GROUP
        rows = slice(c * CHUNK, c * CHUNK + KEYS)
        return jnp.concatenate([ref[2 * kvh, rows, :], ref[2 * kvh + 1, rows, :]], axis=0)

    for c in range(n_chunks):
        if masked and c < WINDOW_CHUNKS:
            variant = jnp.where(t == 0, WINDOW_CHUNKS - c, 0)
        else:
            variant = 0
        for j in range(n_pairs):
            qp = q[c * CHUNK:(c + 1) * CHUNK, j * pair_w:(j + 1) * pair_w]
            s_buf[c, j] = (lax.dot_general(qp, window(kz, j, c), nt_dims, preferred_element_type=F32)
                           + bias_ref[variant, j])

    for c in range(n_chunks):
        for j in range(n_pairs):
            s0 = s_buf[c, j, :, 0:pair_w]
            s1 = s_buf[c, j, :, pair_w:2 * pair_w]
            s2 = s_buf[c, j, :, 2 * pair_w:]
            sink_a = sink_ref[2 * j] * LOG2E
            sink_b = sink_ref[2 * j + 1] * LOG2E
            m_a = jnp.maximum(jnp.max(jnp.maximum(s0, jnp.where(first_head, s1, NEG_INF)), -1, keepdims=True), sink_a)
            m_b = jnp.maximum(jnp.max(jnp.maximum(jnp.where(first_head, NEG_INF, s1), s2), -1, keepdims=True), sink_b)
            p_buf[c, j, :, 0:pair_w] = jnp.exp2(s0 - m_a).astype(BF16)
            p_buf[c, j, :, pair_w:2 * pair_w] = jnp.exp2(s1 - jnp.where(first_head, m_a, m_b)).astype(BF16)
            p_buf[c, j, :, 2 * pair_w:] = jnp.exp2(s2 - m_b).astype(BF16)
            e_buf[c, j] = jnp.where(first_head, jnp.exp2(sink_a - m_a), jnp.exp2(sink_b - m_b))

    span = tm + CONV_PAD - SUBLANES
    for s in range(1, SUBLANES):
        ush[s, 0:span, :] = ucat[s:s + span, :]
    off = CONV_PAD - CONV_HIST
    rb = min(CONV_ROWS, tm)
    groups = rb // SUBLANES

    def conv_rows(lanes, r, carry):
        r0 = pl.multiple_of(r * rb, rb)
        partial = []
        for shift in range(SUBLANES):
            taps = [k for k in range(CONV_K) if (off + k) % SUBLANES == shift]
            top = max(off + k - shift for k in taps)
            slab = ush[shift, pl.ds(r0, rb + top), lanes].reshape(groups + top // SUBLANES, SUBLANES, CONV_LANES)
            acc = None
            for k in taps:
                g0 = (off + k - shift) // SUBLANES
                term = slab[g0:g0 + groups] * wdw_ref[k, :, lanes]
                acc = term if acc is None else acc + term
            partial.append(acc)
        while len(partial) > 1:
            partial = [a + b for a, b in zip(partial[0::2], partial[1::2])]
        yconv[pl.ds(r0, rb), lanes] = (partial[0] + bdw_ref[:, lanes]).reshape(rb, CONV_LANES)
        return carry

    for part in range(CONV_CH // CONV_LANES):
        lanes = slice(part * CONV_LANES, (part + 1) * CONV_LANES)
        lax.fori_loop(0, tm // rb, functools.partial(conv_rows, lanes), 0)
    y = _layer_norm_rows(yconv[...], clg_ref[...], clb_ref[...])
    mix_ref[0, :, ATTN_WIDTH:] = (y * jax.nn.sigmoid(y)).astype(BF16)

    for c in range(n_chunks):
        for j in range(n_pairs):
            res = jnp.dot(p_buf[c, j], window(vz, j, c), preferred_element_type=F32)
            out = res[:, 0:pair_w] * (1.0 / (res[:, pair_w:] + e_buf[c, j]))
            mix_ref[0, c * CHUNK:(c + 1) * CHUNK, j * pair_w:(j + 1) * pair_w] = out.astype(BF16)

    @pl.when(t < nt - 1)
    def _():
        if tm >= WINDOW:
            for i in range(2 * N_KV_HEADS):
                kz[i, 0:WINDOW, :] = kz[i, tm:tm + WINDOW, :]
                vz[i, 0:WINDOW, 0:KV_WIDTH] = vz[i, tm:tm + WINDOW, 0:KV_WIDTH]
            ucat[0:CONV_PAD, :] = ucat[tm:tm + CONV_PAD, :]


def _mixer(x, mod, w_in, bias, sink, w_dw, b_dw, cln_g, cln_b, k0, v0, u0, *, tm, masked):
    bsz, seq, d = x.shape
    assert seq % tm == 0 and tm % CHUNK == 0 and (tm >= WINDOW or seq == tm)
    nt = seq // tm
    kern = functools.partial(_mixer_kernel, tm=tm, nt=nt, n_tiles=bsz * nt, masked=masked)
    const2 = lambda b, t: (0, 0)
    const3 = lambda b, t: (0, 0, 0)
    per_b = lambda b, t: (b, 0, 0)
    return pl.pallas_call(
        kern,
        grid=(bsz, nt),
        in_specs=[pl.BlockSpec(memory_space=pl.ANY),
                  pl.BlockSpec((1, 6, d), per_b),
                  pl.BlockSpec((d, IN_WIDTH), const2),
                  pl.BlockSpec((WINDOW_CHUNKS + 1, N_HEADS // 2, CHUNK, 2 * KEYS), lambda b, t: (0, 0, 0, 0)),
                  pl.BlockSpec(memory_space=pltpu.SMEM),
                  pl.BlockSpec((CONV_K, SUBLANES, CONV_CH), const3),
                  pl.BlockSpec((1, CONV_CH), const2),
                  pl.BlockSpec((1, CONV_CH), const2),
                  pl.BlockSpec((1, CONV_CH), const2),
                  pl.BlockSpec((1, WINDOW, KV_WIDTH), per_b),
                  pl.BlockSpec((1, WINDOW, KV_WIDTH), per_b),
                  pl.BlockSpec((1, CONV_PAD, CONV_CH), per_b)],
        out_specs=[pl.BlockSpec((1, tm, MIX_WIDTH), lambda b, t: (b, t, 0)),
                   pl.BlockSpec((1, WINDOW, KV_WIDTH), per_b),
                   pl.BlockSpec((1, WINDOW, KV_WIDTH), per_b),
                   pl.BlockSpec((1, CONV_HIST, CONV_CH), per_b)],
        out_shape=[jax.ShapeDtypeStruct((bsz, seq, MIX_WIDTH), BF16),
                   jax.ShapeDtypeStruct((bsz, WINDOW, KV_WIDTH), F32),
                   jax.ShapeDtypeStruct((bsz, WINDOW, KV_WIDTH), F32),
                   jax.ShapeDtypeStruct((bsz, CONV_HIST, CONV_CH), F32)],
        scratch_shapes=[pltpu.VMEM((2 * N_KV_HEADS, WINDOW + tm, KV_WIDTH), BF16),
                        pltpu.VMEM((2 * N_KV_HEADS, WINDOW + tm, 2 * KV_WIDTH), BF16),
                        pltpu.VMEM((SUBLANES, CONV_PAD + tm, CONV_CH), F32),
                        pltpu.VMEM((tm, CONV_CH), F32),
                        pltpu.VMEM((tm // CHUNK, N_HEADS // 2, CHUNK, 2 * KEYS), F32),
                        pltpu.VMEM((tm // CHUNK, N_HEADS // 2, CHUNK, 2 * KEYS), BF16),
                        pltpu.VMEM((tm // CHUNK, N_HEADS // 2, CHUNK, 2 * HEAD_DIM), F32),
                        pltpu.VMEM((X_SLOTS, tm, d), F32),
                        pltpu.SemaphoreType.DMA((X_SLOTS,))],
        compiler_params=_cparams("arbitrary", "arbitrary"),
        name="mixer",
    )(x, mod, w_in, bias, sink, w_dw, b_dw, cln_g, cln_b, k0, v0, u0)


def _out_projection(mix_ref, wout_ref, *, bb, tt):
    return jnp.dot(mix_ref[...].reshape(bb * tt, MIX_WIDTH), wout_ref[...], preferred_element_type=F32)


def _post_kernel(mix, x_ref, mod_ref, g_ref, b_ref, wr_split_ref, br_ref, x1_ref, h2e_ref, *, bb, tt):
    rows = bb * tt
    g1 = mod_ref[:, 2:3, :]
    sh2 = mod_ref[:, 3:4, :]
    sc2 = mod_ref[:, 4:5, :]
    r = ALPHA * x_ref[...] + (1.0 + g1) * mix.reshape(bb, tt, D_MODEL)
    x1 = _layer_norm_rows(r, g_ref[...], b_ref[...])
    x1_ref[...] = x1
    h2 = (x1 * (1.0 + sc2) + sh2).reshape(rows, D_MODEL)
    h2_hi = h2.astype(BF16)
    h2e_ref[:, :, 0:D_MODEL] = h2_hi.reshape(bb, tt, D_MODEL)

    hi_terms = jnp.dot(h2_hi, wr_split_ref[...], preferred_element_type=F32)
    logits = hi_terms[:, 0:ROUTE_LANES] + hi_terms[:, ROUTE_LANES:] + br_ref[...]
    lane = lax.broadcasted_iota(jnp.int32, (rows, ROUTE_LANES), 1).astype(F32)
    far = float(ROUTE_LANES)
    is_group = (lane >= N_EXPERTS) & (lane < N_EXPERTS + N_GROUPS)
    gl = jnp.where(is_group, logits, NEG_INF)
    gmax = jnp.max(gl, -1, keepdims=True)
    gidx = jnp.min(jnp.where(is_group & (gl == gmax), lane, far), -1, keepdims=True) - N_EXPERTS
    pg = 1.0 / jnp.sum(jnp.where(is_group, jnp.exp(gl - gmax), 0.0), -1, keepdims=True)
    in_group = (lane >= gidx * EXPERTS_PER_GROUP) & (lane < (gidx + 1) * EXPERTS_PER_GROUP)
    el = jnp.where(in_group, logits, NEG_INF)
    v1 = jnp.max(el, -1, keepdims=True)
    i1 = jnp.min(jnp.where(in_group & (el == v1), lane, far), -1, keepdims=True)
    rest = in_group & (lane != i1)
    el2 = jnp.where(rest, logits, NEG_INF)
    v2 = jnp.max(el2, -1, keepdims=True)
    i2 = jnp.min(jnp.where(rest & (el2 == v2), lane, far), -1, keepdims=True)
    e2 = jnp.exp(v2 - v1)
    w1 = pg / (1.0 + e2)
    w2 = pg * e2 / (1.0 + e2)
    comb = jnp.where(lane == i1, w1, jnp.where(lane == i2, w2, 0.0))

    c_hi = comb.astype(BF16).astype(F32)
    c_lo = (comb - c_hi).astype(BF16).astype(F32)
    rec = c_hi + pltpu.roll(c_lo, COMB_LO_LANE, 1) + jnp.where(lane == gidx + GROUP_LANE, 1.0, 0.0)
    h2e_ref[:, :, D_MODEL:] = rec.astype(BF16).reshape(bb, tt, ROUTE_LANES)


def _moe_rows(rows):
    return rows + N_GROUPS * MOE_ALIGN + MOE_BLK


def _moe_experts(h2e_ref, tri_ref, wg_ref, wu_ref, wd_ref, xs_ref, cw_ref, ys_ref, scatter_ref, other_stream,
                 *, bb, tt):
    rows = bb * tt
    srows = _moe_rows(rows)
    ext = h2e_ref[...].reshape(rows, D_MODEL + ROUTE_LANES)
    rec = ext[:, D_MODEL:]
    cum = jnp.dot(tri_ref[...], rec, preferred_element_type=F32)
    lane1 = lax.broadcasted_iota(jnp.int32, (1, ROUTE_LANES), 1)
    cnt = cum[rows - 1:rows, :]
    off = jnp.int32(0)
    offs, counts = [], []
    off_vec = jnp.zeros((1, ROUTE_LANES), F32)
    for g in range(N_GROUPS):
        n_g = jnp.sum(jnp.where(lane1 == GROUP_LANE + g, cnt, 0.0)).astype(jnp.int32)
        offs.append(off)
        counts.append(n_g)
        off_vec = jnp.where(lane1 == GROUP_LANE + g, off.astype(F32), off_vec)
        off = off + ((n_g + (MOE_ALIGN - 1)) // MOE_ALIGN) * MOE_ALIGN

    lane = lax.broadcasted_iota(jnp.int32, (rows, ROUTE_LANES), 1)
    mine = (lane >= GROUP_LANE) & (lane < GROUP_LANE + N_GROUPS) & (rec.astype(F32) > 0.0)
    pos = jnp.sum(jnp.where(mine, cum - 1.0 + off_vec, 0.0), -1, keepdims=True)
    scatter_ref[...] = jnp.where(pos == lax.broadcasted_iota(jnp.int32, (rows, srows), 1).astype(F32),
                                 1.0, 0.0).astype(BF16)
    pos_row = jnp.transpose(jnp.broadcast_to(pos, (rows, ROUTE_LANES)))[0:1, :]
    grows = rows + N_GROUPS * MOE_ALIGN
    gather = jnp.where(lax.broadcasted_iota(jnp.int32, (grows, rows), 0).astype(F32) == pos_row, 1.0, 0.0).astype(BF16)
    xs_ref[0:grows, :] = jnp.dot(gather, ext[:, 0:D_MODEL], preferred_element_type=F32).astype(BF16)
    rs = jnp.dot(gather, rec, preferred_element_type=F32)
    cw_ref[0:grows, :] = rs + pltpu.roll(rs, ROUTE_LANES - COMB_LO_LANE, 1)

    def experts(g, r0):
        xb = xs_ref[pl.ds(r0, MOE_BLK), :]
        cwb = cw_ref[pl.ds(r0, MOE_BLK), :]
        parts = []
        for j in range(EXPERTS_PER_GROUP):
            e = g * EXPERTS_PER_GROUP + j
            hg = jnp.dot(xb, wg_ref[e], preferred_element_type=F32)
            hu = jnp.dot(xb, wu_ref[e], preferred_element_type=F32)
            parts.append((hg * jax.nn.sigmoid(hg) * hu * cwb[:, e:e + 1]).astype(BF16))
        act = jnp.concatenate(parts, axis=1)
        return jnp.dot(act, wd_ref[g], preferred_element_type=F32)

    for g in range(N_GROUPS):
        r0 = pl.multiple_of(offs[g], MOE_ALIGN)
        ys_ref[pl.ds(r0, MOE_BLK), :] = experts(g, r0).astype(BF16)
        if g == 1:
            other_stream()

    def extra_block(g, i, carry):
        r0 = pl.multiple_of(offs[g] + i * MOE_BLK, MOE_ALIGN)
        row = r0 + lax.broadcasted_iota(jnp.int32, (MOE_BLK, 1), 0)
        old = ys_ref[pl.ds(r0, MOE_BLK), :].astype(F32)
        ys_ref[pl.ds(r0, MOE_BLK), :] = jnp.where(row < offs[g] + counts[g], experts(g, r0), old).astype(BF16)
        return carry

    def extra_blocks():
        for g in range(N_GROUPS):
            lax.fori_loop(1, (counts[g] + (MOE_BLK - 1)) // MOE_BLK, functools.partial(extra_block, g), 0)

    return extra_blocks


def _moe_finish(x1_ref, mod_ref, g_ref, b_ref, ys_ref, scatter_ref, y_ref, *, bb, tt):
    ff = jnp.dot(scatter_ref[...], ys_ref[...], preferred_element_type=F32)
    g2 = mod_ref[:, 5:6, :]
    r = ALPHA * x1_ref[...] + (1.0 + g2) * ff.reshape(bb, tt, D_MODEL)
    y_ref[...] = _layer_norm_rows(r, g_ref[...], b_ref[...])


def _row_parts(bb, tt, n):
    if bb == 1:
        pt = tt // n
        return [((slice(None), slice(i * pt, (i + 1) * pt)), 1, pt) for i in range(n)]
    pb = bb // n
    return [((slice(i * pb, (i + 1) * pb), slice(None)), pb, tt) for i in range(n)]


def _ffn_kernel(x_ref, mix_ref, mod_cur_ref, mod_prev_ref, wout_ref, g1_ref, b1_ref, wr_split_ref,
                br_ref, tri_ref, wg_ref, wu_ref, wd_ref, g2_ref, b2_ref, y_ref,
                x1_buf, h2e_buf, xs_ref, cw_ref, ys_ref, scatter_ref, *, bb, tt):
    s = pl.program_id(0)
    cur = s % 2
    prev = 1 - cur

    @pl.when(s == 0)
    def _():
        ys_ref[...] = jnp.zeros_like(ys_ref)
        xs_ref[...] = jnp.zeros_like(xs_ref)
        cw_ref[...] = jnp.zeros_like(cw_ref)
        x1_buf[1] = jnp.zeros(x1_buf.shape[1:], F32)
        h2e_buf[...] = jnp.zeros(h2e_buf.shape, BF16)

    mix = _out_projection(mix_ref, wout_ref, bb=bb, tt=tt)
    post = functools.partial(_post_kernel, mix, x_ref, mod_cur_ref, g1_ref, b1_ref, wr_split_ref,
                             br_ref, x1_buf.at[cur], h2e_buf, bb=bb, tt=tt)
    extra_blocks = _moe_experts(h2e_buf, tri_ref, wg_ref, wu_ref, wd_ref, xs_ref, cw_ref, ys_ref,
                                scatter_ref, post, bb=bb, tt=tt)
    extra_blocks()
    for i, (sl, pb, pt) in enumerate(_row_parts(bb, tt, TILE_PARTS)):
        part_rows = slice(i * pb * pt, (i + 1) * pb * pt)
        _moe_finish(x1_buf.at[prev].at[sl], mod_prev_ref.at[sl[0]], g2_ref, b2_ref, ys_ref, scatter_ref.at[part_rows],
                    y_ref.at[sl], bb=pb, tt=pt)


def _ffn(x, mixin, mod, p, *, bb, tt):
    bsz, seq, d = x.shape
    assert bsz % bb == 0 and seq % tt == 0
    rows = bb * tt
    srows = _moe_rows(rows)
    n_j = seq // tt
    n_tiles = (bsz // bb) * n_j
    tri = jnp.tril(jnp.ones((rows, rows), BF16))
    kern = functools.partial(_ffn_kernel, bb=bb, tt=tt)
    cur = lambda s: jnp.minimum(s, n_tiles - 1)
    prev = lambda s: jnp.maximum(s - 1, 0)
    const2 = lambda s: (0, 0)
    const3 = lambda s: (0, 0, 0)
    resident = pl.Buffered(1)
    return pl.pallas_call(
        kern,
        grid=(n_tiles + 1,),
        in_specs=[pl.BlockSpec((bb, tt, d), lambda s: (cur(s) // n_j, cur(s) % n_j, 0)),
                  pl.BlockSpec((bb, tt, MIX_WIDTH), lambda s: (cur(s) // n_j, cur(s) % n_j, 0)),
                  pl.BlockSpec((bb, 6, d), lambda s: (cur(s) // n_j, 0, 0)),
                  pl.BlockSpec((bb, 6, d), lambda s: (prev(s) // n_j, 0, 0)),
                  pl.BlockSpec((MIX_WIDTH, d), const2, pipeline_mode=resident),
                  pl.BlockSpec((1, d), const2),
                  pl.BlockSpec((1, d), const2),
                  pl.BlockSpec((d, 2 * ROUTE_LANES), const2, pipeline_mode=resident),
                  pl.BlockSpec((1, ROUTE_LANES), const2),
                  pl.BlockSpec((rows, rows), const2, pipeline_mode=resident),
                  pl.BlockSpec(p["w_gate"].shape, const3, pipeline_mode=resident),
                  pl.BlockSpec(p["w_up"].shape, const3, pipeline_mode=resident),
                  pl.BlockSpec(p["w_down"].shape, const3, pipeline_mode=resident),
                  pl.BlockSpec((1, d), const2),
                  pl.BlockSpec((1, d), const2)],
        out_specs=pl.BlockSpec((bb, tt, d), lambda s: (prev(s) // n_j, prev(s) % n_j, 0)),
        out_shape=jax.ShapeDtypeStruct((bsz, seq, d), F32),
        scratch_shapes=[pltpu.VMEM((2, bb, tt, d), F32),
                        pltpu.VMEM((bb, tt, d + ROUTE_LANES), BF16),
                        pltpu.VMEM((srows, d), BF16),
                        pltpu.VMEM((srows, ROUTE_LANES), F32),
                        pltpu.VMEM((srows, d), BF16),
                        pltpu.VMEM((rows, srows), BF16)],
        compiler_params=_cparams("arbitrary"),
        name="ffn",
    )(x, mixin, mod, mod, p["w_out"], p["ln1_g"], p["ln1_b"], p["w_route_split"], p["b_route"],
      tri, p["w_gate"], p["w_up"], p["w_down"], p["ln2_g"], p["ln2_b"])


def _stream(x, mod, k0, v0, u0, p, *, mixer_tm, masked, bb, tt):
    mixin, nk, nv, nu = _mixer(x, mod, p["w_in"], p["bias"], p["sink"], p["w_dw"], p["b_dw"],
                               p["cln_g"], p["cln_b"], k0, v0, u0, tm=mixer_tm, masked=masked)
    y = _ffn(x, mixin, mod, p, bb=bb, tt=tt)
    bsz = x.shape[0]
    cache_shape = (1, bsz, WINDOW, N_KV_HEADS, HEAD_DIM)
    return y, nk.reshape(cache_shape), nv.reshape(cache_shape), nu[None]


def kernel(x_prompt, x_sample, cache_attn_k, cache_attn_v, state_conv, c_prompt, c_sample, rel_bias, w_ada, b_ada, w_in, attn_sinks, w_dw, b_dw, conv_ln_g, conv_ln_b, w_out, ln1_g, ln1_b, w_group, b_group, w_erouter, b_erouter, w_gate, w_up, w_down, ln2_g, ln2_b):
    bp = x_prompt.shape[0]
    bs = x_sample.shape[0]
    mod = _modulation(jnp.concatenate([c_prompt, c_sample], 0), w_ada[0], b_ada[0])
    mod = mod.reshape(bp + bs, 6, D_MODEL)

    w_route = jnp.concatenate([w_erouter[0].reshape(D_MODEL, N_EXPERTS), w_group[0]], 1)
    w_route = jnp.pad(w_route, ((0, 0), (0, ROUTE_LANES - N_EXPERTS - N_GROUPS)))
    b_route = jnp.concatenate([b_erouter[0].reshape(N_EXPERTS), b_group[0]])
    b_route = jnp.pad(b_route, (0, ROUTE_LANES - N_EXPERTS - N_GROUPS)).reshape(1, ROUTE_LANES)
    w_route_hi = w_route.astype(BF16)
    w_route_lo = (w_route - w_route_hi.astype(F32)).astype(BF16)

    p = {
        "w_in": w_in[0].astype(BF16),
        "bias": _relative_bias(rel_bias),
        "sink": attn_sinks[0],
        "w_dw": jnp.broadcast_to(w_dw[0][:, None, :], (CONV_K, SUBLANES, CONV_CH)),
        "b_dw": b_dw[0].reshape(1, CONV_CH),
        "cln_g": conv_ln_g[0].reshape(1, CONV_CH), "cln_b": conv_ln_b[0].reshape(1, CONV_CH),
        "w_out": w_out[0].astype(BF16),
        "ln1_g": ln1_g[0].reshape(1, D_MODEL), "ln1_b": ln1_b[0].reshape(1, D_MODEL),
        "w_route_split": jnp.concatenate([w_route_hi, w_route_lo], 1),
        "b_route": b_route,
        "w_gate": w_gate[0].astype(BF16), "w_up": w_up[0].astype(BF16),
        "w_down": w_down[0].astype(BF16).reshape(N_GROUPS, EXPERTS_PER_GROUP * D_EXPERT, D_MODEL),
        "ln2_g": ln2_g[0].reshape(1, D_MODEL), "ln2_b": ln2_b[0].reshape(1, D_MODEL),
    }

    zk = jnp.zeros((bp, WINDOW, KV_WIDTH), F32)
    zu = jnp.zeros((bp, CONV_PAD, CONV_CH), F32)
    yp, pk, pv, pc = _stream(x_prompt, mod[:bp], zk, zk, zu, p, mixer_tm=512, masked=True, bb=1, tt=512)

    k0 = cache_attn_k[0].reshape(bs, WINDOW, KV_WIDTH)
    v0 = cache_attn_v[0].reshape(bs, WINDOW, KV_WIDTH)
    u0 = jnp.pad(state_conv[0], ((0, 0), (CONV_PAD - CONV_HIST, 0), (0, 0)))
    ts = x_sample.shape[1]
    ys, sk, sv, sc = _stream(x_sample, mod[bp:], k0, v0, u0, p, mixer_tm=ts, masked=False, bb=8, tt=ts)
    return yp, ys, pk, pv, pc, sk, sv, sc
```

```python
import functools
import math

import jax
import jax.numpy as jnp
from jax import lax
from jax.experimental import pallas as pl
from jax.experimental.pallas import tpu as pltpu

D_MODEL = 1024
CHUNK = 64
N_HEADS = 8
N_KV_HEADS = 2
HEAD_DIM = 64
Q_GROUP = N_HEADS // N_KV_HEADS
ATTN_WIDTH = N_HEADS * HEAD_DIM
KV_WIDTH = N_KV_HEADS * HEAD_DIM
WINDOW = 128
WINDOW_CHUNKS = WINDOW // CHUNK
KEYS = WINDOW + CHUNK
CONV_CH = D_MODEL // 2
CONV_K = 31
CONV_HIST = CONV_K - 1
CONV_PAD = 32
SUBLANES = 8
CONV_ROWS = 256
CONV_LANES = 128
MIX_WIDTH = ATTN_WIDTH + CONV_CH
IN_WIDTH = ATTN_WIDTH + 2 * KV_WIDTH + 2 * CONV_CH
N_BUCKETS = 32
MAX_DISTANCE = 128
N_GROUPS = 4
EXPERTS_PER_GROUP = 4
N_EXPERTS = N_GROUPS * EXPERTS_PER_GROUP
D_EXPERT = D_MODEL // 4
DEPTH = 1
ALPHA = (2 * DEPTH) ** 0.25
LN_EPS = 1e-5
NEG_INF = -1e30
LOG2E = math.log2(math.e)
ROUTE_LANES = 128
COMB_LO_LANE = 16
GROUP_LANE = 32
MOE_BLK = 160
MOE_ALIGN = 16
TILE_PARTS = 2
VMEM_LIMIT = 56 * 1024 * 1024

BF16 = jnp.bfloat16
F32 = jnp.float32


def _cparams(*sem):
    return pltpu.CompilerParams(dimension_semantics=sem, vmem_limit_bytes=VMEM_LIMIT)


def _mod_kernel(c_ref, w_ref, b_ref, o_ref):
    c = c_ref[...]
    s = c * jax.nn.sigmoid(c)
    o_ref[...] = jnp.dot(s, w_ref[...], preferred_element_type=F32,
                         precision=lax.Precision.HIGHEST) + b_ref[...]


def _modulation(c, w_ada, b_ada):
    n, d = c.shape
    width = w_ada.shape[1]
    tn = 1536
    return pl.pallas_call(
        _mod_kernel,
        grid=(width // tn,),
        in_specs=[pl.BlockSpec((n, d), lambda j: (0, 0)),
                  pl.BlockSpec((d, tn), lambda j: (0, j)),
                  pl.BlockSpec((1, tn), lambda j: (0, j))],
        out_specs=pl.BlockSpec((n, tn), lambda j: (0, j)),
        out_shape=jax.ShapeDtypeStruct((n, width), F32),
        compiler_params=_cparams("arbitrary"),
        name="modulation",
    )(c, w_ada, b_ada.reshape(1, width))


def _t5_bucket(rel):
    nb = N_BUCKETS // 2
    max_exact = nb // 2
    ret = jnp.where(rel > 0, nb, 0)
    n = jnp.abs(rel)
    large = max_exact + (jnp.log(jnp.maximum(n, 1).astype(jnp.float32) / max_exact)
                         / math.log(MAX_DISTANCE / max_exact) * (nb - max_exact)).astype(jnp.int32)
    large = jnp.minimum(large, nb - 1)
    return ret + jnp.where(n < max_exact, n, large)


def _bias_kernel(table_ref, bucket_ref, o_ref):
    bucket = bucket_ref[...]
    col = lax.broadcasted_iota(jnp.int32, bucket.shape, 1)
    for h in range(N_HEADS):
        acc = jnp.zeros(bucket.shape, F32)
        for b in range(N_BUCKETS):
            acc = jnp.where(bucket == b, table_ref[b, h], acc)
        for v in range(WINDOW_CHUNKS + 1):
            o_ref[v, h] = jnp.where(col >= v * CHUNK, acc * LOG2E, NEG_INF)


def _relative_bias(table):
    rel = jnp.arange(KEYS)[None, :] - WINDOW - jnp.arange(CHUNK)[:, None]
    bucket = _t5_bucket(rel).astype(jnp.int32)
    nv = WINDOW_CHUNKS + 1
    bias = pl.pallas_call(
        _bias_kernel,
        in_specs=[pl.BlockSpec(memory_space=pltpu.SMEM),
                  pl.BlockSpec((CHUNK, KEYS), lambda: (0, 0))],
        out_specs=pl.BlockSpec((nv, N_HEADS, CHUNK, KEYS), lambda: (0, 0, 0, 0)),
        out_shape=jax.ShapeDtypeStruct((nv, N_HEADS, CHUNK, KEYS), F32),
        name="relative_bias",
    )(table, bucket)
    bias = bias.reshape(nv, N_HEADS // 2, 2, CHUNK, KEYS)
    return jnp.transpose(bias, (0, 1, 3, 2, 4)).reshape(nv, N_HEADS // 2, CHUNK, 2 * KEYS)


def _layer_norm_rows(x, g, b):
    mu = jnp.mean(x, -1, keepdims=True)
    xc = x - mu
    var = jnp.mean(xc * xc, -1, keepdims=True)
    return xc * lax.rsqrt(var + LN_EPS) * g + b


def _mixer_kernel(x_ref, mod_ref, win_ref, bias_ref, sink_ref, wdw_ref, bdw_ref, clg_ref, clb_ref,
                  k0_ref, v0_ref, u0_ref,
                  mix_ref, nk_ref, nv_ref, nu_ref,
                  kz, vz, ush, yconv, s_buf, p_buf, e_buf, *, tm, masked):
    t = pl.program_id(1)
    nt = pl.num_programs(1)
    n_chunks = tm // CHUNK
    ucat = ush.at[0]

    left = lax.broadcasted_iota(jnp.int32, (1, KV_WIDTH), 1) < HEAD_DIM

    def store_kv(row0, k, v):
        n = k.shape[0]
        for dst, val in ((kz, k), (vz, v)):
            swapped = pltpu.roll(val, HEAD_DIM, 1)
            dst[0, row0:row0 + n, 0:KV_WIDTH] = jnp.where(left, val, 0.0).astype(BF16)
            dst[1, row0:row0 + n, 0:KV_WIDTH] = jnp.where(left, 0.0, swapped).astype(BF16)
            dst[2, row0:row0 + n, 0:KV_WIDTH] = jnp.where(left, swapped, 0.0).astype(BF16)
            dst[3, row0:row0 + n, 0:KV_WIDTH] = jnp.where(left, 0.0, val).astype(BF16)

    @pl.when(t == 0)
    def _():
        ones_left = jnp.broadcast_to(jnp.where(left, 1.0, 0.0).astype(BF16), (WINDOW + tm, KV_WIDTH))
        ones_right = jnp.broadcast_to(jnp.where(left, 0.0, 1.0).astype(BF16), (WINDOW + tm, KV_WIDTH))
        for i in range(2 * N_KV_HEADS):
            vz[i, :, KV_WIDTH:] = ones_left if i % 2 == 0 else ones_right
        store_kv(0, k0_ref[0], v0_ref[0])
        ucat[0:CONV_PAD, :] = u0_ref[0]

    sh1 = mod_ref[0, 0:1, :]
    sc1 = mod_ref[0, 1:2, :]
    h = (x_ref[0] * (1.0 + sc1) + sh1).astype(BF16)
    o1 = ATTN_WIDTH
    o3 = o1 + 2 * KV_WIDTH
    q = (jnp.dot(h, win_ref[:, 0:o1], preferred_element_type=F32) * (HEAD_DIM ** -0.5 * LOG2E)).astype(BF16)
    kv = jnp.dot(h, win_ref[:, o1:o3], preferred_element_type=F32)
    ag = jnp.dot(h, win_ref[:, o3:IN_WIDTH], preferred_element_type=F32)
    u = ag[:, 0:CONV_CH] * jax.nn.sigmoid(ag[:, CONV_CH:])
    store_kv(WINDOW, kv[:, 0:KV_WIDTH], kv[:, KV_WIDTH:])
    ucat[CONV_PAD:CONV_PAD + tm, :] = u

    if tm >= WINDOW:
        nk_ref[0] = kv[tm - WINDOW:, 0:KV_WIDTH]
        nv_ref[0] = kv[tm - WINDOW:, KV_WIDTH:]
    else:
        nk_ref[0, 0:WINDOW - tm, :] = k0_ref[0, tm:WINDOW, :]
        nv_ref[0, 0:WINDOW - tm, :] = v0_ref[0, tm:WINDOW, :]
        nk_ref[0, WINDOW - tm:, :] = kv[:, 0:KV_WIDTH]
        nv_ref[0, WINDOW - tm:, :] = kv[:, KV_WIDTH:]
    nu_ref[0] = ucat[CONV_PAD + tm - CONV_HIST:CONV_PAD + tm, :]

    n_pairs = N_HEADS // 2
    pair_w = 2 * HEAD_DIM
    first_head = lax.broadcasted_iota(jnp.int32, (CHUNK, pair_w), 1) < HEAD_DIM
    nt_dims = (((1,), (1,)), ((), ()))

    def window(ref, j, c):
        kvh = (2 * j) // Q_GROUP
        rows = slice(c * CHUNK, c * CHUNK + KEYS)
        return jnp.concatenate([ref[2 * kvh, rows, :], ref[2 * kvh + 1, rows, :]], axis=0)

    for c in range(n_chunks):
        if masked and c < WINDOW_CHUNKS:
            variant = jnp.where(t == 0, WINDOW_CHUNKS - c, 0)
        else:
            variant = 0
        for j in range(n_pairs):
            qp = q[c * CHUNK:(c + 1) * CHUNK, j * pair_w:(j + 1) * pair_w]
            s_buf[c, j] = (lax.dot_general(qp, window(kz, j, c), nt_dims, preferred_element_type=F32)
                           + bias_ref[variant, j])

    for c in range(n_chunks):
        for j in range(n_pairs):
            s0 = s_buf[c, j, :, 0:pair_w]
            s1 = s_buf[c, j, :, pair_w:2 * pair_w]
            s2 = s_buf[c, j, :, 2 * pair_w:]
            sink_a = sink_ref[2 * j] * LOG2E
            sink_b = sink_ref[2 * j + 1] * LOG2E
            m_a = jnp.maximum(jnp.max(jnp.maximum(s0, jnp.where(first_head, s1, NEG_INF)), -1, keepdims=True), sink_a)
            m_b = jnp.maximum(jnp.max(jnp.maximum(jnp.where(first_head, NEG_INF, s1), s2), -1, keepdims=True), sink_b)
            p_buf[c, j, :, 0:pair_w] = jnp.exp2(s0 - m_a).astype(BF16)
            p_buf[c, j, :, pair_w:2 * pair_w] = jnp.exp2(s1 - jnp.where(first_head, m_a, m_b)).astype(BF16)
            p_buf[c, j, :, 2 * pair_w:] = jnp.exp2(s2 - m_b).astype(BF16)
            e_buf[c, j] = jnp.where(first_head, jnp.exp2(sink_a - m_a), jnp.exp2(sink_b - m_b))

    span = tm + CONV_PAD - SUBLANES
    for s in range(1, SUBLANES):
        ush[s, 0:span, :] = ucat[s:s + span, :]
    off = CONV_PAD - CONV_HIST
    rb = min(CONV_ROWS, tm)
    groups = rb // SUBLANES

    def conv_rows(lanes, r, carry):
        r0 = pl.multiple_of(r * rb, rb)
        partial = []
        for shift in range(SUBLANES):
            taps = [k for k in range(CONV_K) if (off + k) % SUBLANES == shift]
            top = max(off + k - shift for k in taps)
            slab = ush[shift, pl.ds(r0, rb + top), lanes].reshape(groups + top // SUBLANES, SUBLANES, CONV_LANES)
            acc = None
            for k in taps:
                g0 = (off + k - shift) // SUBLANES
                term = slab[g0:g0 + groups] * wdw_ref[k, :, lanes]
                acc = term if acc is None else acc + term
            partial.append(acc)
        while len(partial) > 1:
            partial = [a + b for a, b in zip(partial[0::2], partial[1::2])]
        yconv[pl.ds(r0, rb), lanes] = (partial[0] + bdw_ref[:, lanes]).reshape(rb, CONV_LANES)
        return carry

    for part in range(CONV_CH // CONV_LANES):
        lanes = slice(part * CONV_LANES, (part + 1) * CONV_LANES)
        lax.fori_loop(0, tm // rb, functools.partial(conv_rows, lanes), 0)
    y = _layer_norm_rows(yconv[...], clg_ref[...], clb_ref[...])
    mix_ref[0, :, ATTN_WIDTH:] = (y * jax.nn.sigmoid(y)).astype(BF16)

    for c in range(n_chunks):
        for j in range(n_pairs):
            res = jnp.dot(p_buf[c, j], window(vz, j, c), preferred_element_type=F32)
            out = res[:, 0:pair_w] * (1.0 / (res[:, pair_w:] + e_buf[c, j]))
            mix_ref[0, c * CHUNK:(c + 1) * CHUNK, j * pair_w:(j + 1) * pair_w] = out.astype(BF16)

    @pl.when(t < nt - 1)
    def _():
        if tm >= WINDOW:
            for i in range(2 * N_KV_HEADS):
                kz[i, 0:WINDOW, :] = kz[i, tm:tm + WINDOW, :]
                vz[i, 0:WINDOW, 0:KV_WIDTH] = vz[i, tm:tm + WINDOW, 0:KV_WIDTH]
            ucat[0:CONV_PAD, :] = ucat[tm:tm + CONV_PAD, :]


def _mixer(x, mod, w_in, bias, sink, w_dw, b_dw, cln_g, cln_b, k0, v0, u0, *, tm, masked):
    bsz, seq, d = x.shape
    assert seq % tm == 0 and tm % CHUNK == 0 and (tm >= WINDOW or seq == tm)
    nt = seq // tm
    kern = functools.partial(_mixer_kernel, tm=tm, masked=masked)
    const2 = lambda b, t: (0, 0)
    const3 = lambda b, t: (0, 0, 0)
    per_b = lambda b, t: (b, 0, 0)
    return pl.pallas_call(
        kern,
        grid=(bsz, nt),
        in_specs=[pl.BlockSpec((1, tm, d), lambda b, t: (b, t, 0)),
                  pl.BlockSpec((1, 6, d), per_b),
                  pl.BlockSpec((d, IN_WIDTH), const2),
                  pl.BlockSpec((WINDOW_CHUNKS + 1, N_HEADS // 2, CHUNK, 2 * KEYS), lambda b, t: (0, 0, 0, 0)),
                  pl.BlockSpec(memory_space=pltpu.SMEM),
                  pl.BlockSpec((CONV_K, SUBLANES, CONV_CH), const3),
                  pl.BlockSpec((1, CONV_CH), const2),
                  pl.BlockSpec((1, CONV_CH), const2),
                  pl.BlockSpec((1, CONV_CH), const2),
                  pl.BlockSpec((1, WINDOW, KV_WIDTH), per_b),
                  pl.BlockSpec((1, WINDOW, KV_WIDTH), per_b),
                  pl.BlockSpec((1, CONV_PAD, CONV_CH), per_b)],
        out_specs=[pl.BlockSpec((1, tm, MIX_WIDTH), lambda b, t: (b, t, 0)),
                   pl.BlockSpec((1, WINDOW, KV_WIDTH), per_b),
                   pl.BlockSpec((1, WINDOW, KV_WIDTH), per_b),
                   pl.BlockSpec((1, CONV_HIST, CONV_CH), per_b)],
        out_shape=[jax.ShapeDtypeStruct((bsz, seq, MIX_WIDTH), BF16),
                   jax.ShapeDtypeStruct((bsz, WINDOW, KV_WIDTH), F32),
                   jax.ShapeDtypeStruct((bsz, WINDOW, KV_WIDTH), F32),
                   jax.ShapeDtypeStruct((bsz, CONV_HIST, CONV_CH), F32)],
        scratch_shapes=[pltpu.VMEM((2 * N_KV_HEADS, WINDOW + tm, KV_WIDTH), BF16),
                        pltpu.VMEM((2 * N_KV_HEADS, WINDOW + tm, 2 * KV_WIDTH), BF16),
                        pltpu.VMEM((SUBLANES, CONV_PAD + tm, CONV_CH), F32),
                        pltpu.VMEM((tm, CONV_CH), F32),
                        pltpu.VMEM((tm // CHUNK, N_HEADS // 2, CHUNK, 2 * KEYS), F32),
                        pltpu.VMEM((tm // CHUNK, N_HEADS // 2, CHUNK, 2 * KEYS), BF16),
                        pltpu.VMEM((tm // CHUNK, N_HEADS // 2, CHUNK, 2 * HEAD_DIM), F32)],
        compiler_params=_cparams("parallel", "arbitrary"),
        name="mixer",
    )(x, mod, w_in, bias, sink, w_dw, b_dw, cln_g, cln_b, k0, v0, u0)


def _out_projection(mix_ref, wout_ref, *, bb, tt):
    return jnp.dot(mix_ref[...].reshape(bb * tt, MIX_WIDTH), wout_ref[...], preferred_element_type=F32)


def _post_kernel(mix, x_ref, mod_ref, g_ref, b_ref, wr_split_ref, br_ref, x1_ref, h2e_ref, *, bb, tt):
    rows = bb * tt
    g1 = mod_ref[:, 2:3, :]
    sh2 = mod_ref[:, 3:4, :]
    sc2 = mod_ref[:, 4:5, :]
    r = ALPHA * x_ref[...] + (1.0 + g1) * mix.reshape(bb, tt, D_MODEL)
    x1 = _layer_norm_rows(r, g_ref[...], b_ref[...])
    x1_ref[...] = x1
    h2 = (x1 * (1.0 + sc2) + sh2).reshape(rows, D_MODEL)
    h2_hi = h2.astype(BF16)
    h2e_ref[:, :, 0:D_MODEL] = h2_hi.reshape(bb, tt, D_MODEL)

    hi_terms = jnp.dot(h2_hi, wr_split_ref[...], preferred_element_type=F32)
    logits = hi_terms[:, 0:ROUTE_LANES] + hi_terms[:, ROUTE_LANES:] + br_ref[...]
    lane = lax.broadcasted_iota(jnp.int32, (rows, ROUTE_LANES), 1).astype(F32)
    far = float(ROUTE_LANES)
    is_group = (lane >= N_EXPERTS) & (lane < N_EXPERTS + N_GROUPS)
    gl = jnp.where(is_group, logits, NEG_INF)
    gmax = jnp.max(gl, -1, keepdims=True)
    gidx = jnp.min(jnp.where(is_group & (gl == gmax), lane, far), -1, keepdims=True) - N_EXPERTS
    pg = 1.0 / jnp.sum(jnp.where(is_group, jnp.exp(gl - gmax), 0.0), -1, keepdims=True)
    in_group = (lane >= gidx * EXPERTS_PER_GROUP) & (lane < (gidx + 1) * EXPERTS_PER_GROUP)
    el = jnp.where(in_group, logits, NEG_INF)
    v1 = jnp.max(el, -1, keepdims=True)
    i1 = jnp.min(jnp.where(in_group & (el == v1), lane, far), -1, keepdims=True)
    rest = in_group & (lane != i1)
    el2 = jnp.where(rest, logits, NEG_INF)
    v2 = jnp.max(el2, -1, keepdims=True)
    i2 = jnp.min(jnp.where(rest & (el2 == v2), lane, far), -1, keepdims=True)
    e2 = jnp.exp(v2 - v1)
    w1 = pg / (1.0 + e2)
    w2 = pg * e2 / (1.0 + e2)
    comb = jnp.where(lane == i1, w1, jnp.where(lane == i2, w2, 0.0))

    c_hi = comb.astype(BF16).astype(F32)
    c_lo = (comb - c_hi).astype(BF16).astype(F32)
    rec = c_hi + pltpu.roll(c_lo, COMB_LO_LANE, 1) + jnp.where(lane == gidx + GROUP_LANE, 1.0, 0.0)
    h2e_ref[:, :, D_MODEL:] = rec.astype(BF16).reshape(bb, tt, ROUTE_LANES)


def _moe_rows(rows):
    return rows + N_GROUPS * MOE_ALIGN + MOE_BLK


def _moe_experts(h2e_ref, tri_ref, wg_ref, wu_ref, wd_ref, xs_ref, cw_ref, ys_ref, scatter_ref, other_stream,
                 *, bb, tt):
    rows = bb * tt
    srows = _moe_rows(rows)
    ext = h2e_ref[...].reshape(rows, D_MODEL + ROUTE_LANES)
    rec = ext[:, D_MODEL:]
    cum = jnp.dot(tri_ref[...], rec, preferred_element_type=F32)
    lane1 = lax.broadcasted_iota(jnp.int32, (1, ROUTE_LANES), 1)
    cnt = cum[rows - 1:rows, :]
    off = jnp.int32(0)
    offs, counts = [], []
    off_vec = jnp.zeros((1, ROUTE_LANES), F32)
    for g in range(N_GROUPS):
        n_g = jnp.sum(jnp.where(lane1 == GROUP_LANE + g, cnt, 0.0)).astype(jnp.int32)
        offs.append(off)
        counts.append(n_g)
        off_vec = jnp.where(lane1 == GROUP_LANE + g, off.astype(F32), off_vec)
        off = off + ((n_g + (MOE_ALIGN - 1)) // MOE_ALIGN) * MOE_ALIGN

    lane = lax.broadcasted_iota(jnp.int32, (rows, ROUTE_LANES), 1)
    mine = (lane >= GROUP_LANE) & (lane < GROUP_LANE + N_GROUPS) & (rec.astype(F32) > 0.0)
    pos = jnp.sum(jnp.where(mine, cum - 1.0 + off_vec, 0.0), -1, keepdims=True)
    scatter_ref[...] = jnp.where(pos == lax.broadcasted_iota(jnp.int32, (rows, srows), 1).astype(F32),
                                 1.0, 0.0).astype(BF16)
    pos_row = jnp.transpose(jnp.broadcast_to(pos, (rows, ROUTE_LANES)))[0:1, :]
    grows = rows + N_GROUPS * MOE_ALIGN
    gather = jnp.where(lax.broadcasted_iota(jnp.int32, (grows, rows), 0).astype(F32) == pos_row, 1.0, 0.0).astype(BF16)
    xs_ref[0:grows, :] = jnp.dot(gather, ext[:, 0:D_MODEL], preferred_element_type=F32).astype(BF16)
    rs = jnp.dot(gather, rec, preferred_element_type=F32)
    cw_ref[0:grows, :] = rs + pltpu.roll(rs, ROUTE_LANES - COMB_LO_LANE, 1)

    def experts(g, r0):
        xb = xs_ref[pl.ds(r0, MOE_BLK), :]
        cwb = cw_ref[pl.ds(r0, MOE_BLK), :]
        parts = []
        for j in range(EXPERTS_PER_GROUP):
            e = g * EXPERTS_PER_GROUP + j
            hg = jnp.dot(xb, wg_ref[e], preferred_element_type=F32)
            hu = jnp.dot(xb, wu_ref[e], preferred_element_type=F32)
            parts.append((hg * jax.nn.sigmoid(hg) * hu * cwb[:, e:e + 1]).astype(BF16))
        act = jnp.concatenate(parts, axis=1)
        return jnp.dot(act, wd_ref[g], preferred_element_type=F32)

    for g in range(N_GROUPS):
        r0 = pl.multiple_of(offs[g], MOE_ALIGN)
        ys_ref[pl.ds(r0, MOE_BLK), :] = experts(g, r0).astype(BF16)
        if g == 1:
            other_stream()

    def extra_block(g, i, carry):
        r0 = pl.multiple_of(offs[g] + i * MOE_BLK, MOE_ALIGN)
        row = r0 + lax.broadcasted_iota(jnp.int32, (MOE_BLK, 1), 0)
        old = ys_ref[pl.ds(r0, MOE_BLK), :].astype(F32)
        ys_ref[pl.ds(r0, MOE_BLK), :] = jnp.where(row < offs[g] + counts[g], experts(g, r0), old).astype(BF16)
        return carry

    def extra_blocks():
        for g in range(N_GROUPS):
            lax.fori_loop(1, (counts[g] + (MOE_BLK - 1)) // MOE_BLK, functools.partial(extra_block, g), 0)

    return extra_blocks


def _moe_finish(x1_ref, mod_ref, g_ref, b_ref, ys_ref, scatter_ref, y_ref, *, bb, tt):
    ff = jnp.dot(scatter_ref[...], ys_ref[...], preferred_element_type=F32)
    g2 = mod_ref[:, 5:6, :]
    r = ALPHA * x1_ref[...] + (1.0 + g2) * ff.reshape(bb, tt, D_MODEL)
    y_ref[...] = _layer_norm_rows(r, g_ref[...], b_ref[...])


def _row_parts(bb, tt, n):
    if bb == 1:
        pt = tt // n
        return [((slice(None), slice(i * pt, (i + 1) * pt)), 1, pt) for i in range(n)]
    pb = bb // n
    return [((slice(i * pb, (i + 1) * pb), slice(None)), pb, tt) for i in range(n)]


def _ffn_kernel(x_ref, mix_ref, mod_cur_ref, mod_prev_ref, wout_ref, g1_ref, b1_ref, wr_split_ref,
                br_ref, tri_ref, wg_ref, wu_ref, wd_ref, g2_ref, b2_ref, y_ref,
                x1_buf, h2e_buf, xs_ref, cw_ref, ys_ref, scatter_ref, *, bb, tt):
    s = pl.program_id(0)
    cur = s % 2
    prev = 1 - cur

    @pl.when(s == 0)
    def _():
        ys_ref[...] = jnp.zeros_like(ys_ref)
        xs_ref[...] = jnp.zeros_like(xs_ref)
        cw_ref[...] = jnp.zeros_like(cw_ref)
        x1_buf[1] = jnp.zeros(x1_buf.shape[1:], F32)
        h2e_buf[...] = jnp.zeros(h2e_buf.shape, BF16)

    mix = _out_projection(mix_ref, wout_ref, bb=bb, tt=tt)
    post = functools.partial(_post_kernel, mix, x_ref, mod_cur_ref, g1_ref, b1_ref, wr_split_ref,
                             br_ref, x1_buf.at[cur], h2e_buf, bb=bb, tt=tt)
    extra_blocks = _moe_experts(h2e_buf, tri_ref, wg_ref, wu_ref, wd_ref, xs_ref, cw_ref, ys_ref,
                                scatter_ref, post, bb=bb, tt=tt)
    extra_blocks()
    for i, (sl, pb, pt) in enumerate(_row_parts(bb, tt, TILE_PARTS)):
        part_rows = slice(i * pb * pt, (i + 1) * pb * pt)
        _moe_finish(x1_buf.at[prev].at[sl], mod_prev_ref.at[sl[0]], g2_ref, b2_ref, ys_ref, scatter_ref.at[part_rows],
                    y_ref.at[sl], bb=pb, tt=pt)


def _ffn(x, mixin, mod, p, *, bb, tt):
    bsz, seq, d = x.shape
    assert bsz % bb == 0 and seq % tt == 0
    rows = bb * tt
    srows = _moe_rows(rows)
    n_j = seq // tt
    n_tiles = (bsz // bb) * n_j
    tri = jnp.tril(jnp.ones((rows, rows), BF16))
    kern = functools.partial(_ffn_kernel, bb=bb, tt=tt)
    cur = lambda s: jnp.minimum(s, n_tiles - 1)
    prev = lambda s: jnp.maximum(s - 1, 0)
    const2 = lambda s: (0, 0)
    const3 = lambda s: (0, 0, 0)
    resident = pl.Buffered(1)
    return pl.pallas_call(
        kern,
        grid=(n_tiles + 1,),
        in_specs=[pl.BlockSpec((bb, tt, d), lambda s: (cur(s) // n_j, cur(s) % n_j, 0)),
                  pl.BlockSpec((bb, tt, MIX_WIDTH), lambda s: (cur(s) // n_j, cur(s) % n_j, 0)),
                  pl.BlockSpec((bb, 6, d), lambda s: (cur(s) // n_j, 0, 0)),
                  pl.BlockSpec((bb, 6, d), lambda s: (prev(s) // n_j, 0, 0)),
                  pl.BlockSpec((MIX_WIDTH, d), const2, pipeline_mode=resident),
                  pl.BlockSpec((1, d), const2),
                  pl.BlockSpec((1, d), const2),
                  pl.BlockSpec((d, 2 * ROUTE_LANES), const2, pipeline_mode=resident),
                  pl.BlockSpec((1, ROUTE_LANES), const2),
                  pl.BlockSpec((rows, rows), const2, pipeline_mode=resident),
                  pl.BlockSpec(p["w_gate"].shape, const3, pipeline_mode=resident),
                  pl.BlockSpec(p["w_up"].shape, const3, pipeline_mode=resident),
                  pl.BlockSpec(p["w_down"].shape, const3, pipeline_mode=resident),
                  pl.BlockSpec((1, d), const2),
                  pl.BlockSpec((1, d), const2)],
        out_specs=pl.BlockSpec((bb, tt, d), lambda s: (prev(s) // n_j, prev(s) % n_j, 0)),
        out_shape=jax.ShapeDtypeStruct((bsz, seq, d), F32),
        scratch_shapes=[pltpu.VMEM((2, bb, tt, d), F32),
                        pltpu.VMEM((bb, tt, d + ROUTE_LANES), BF16),
                        pltpu.VMEM((srows, d), BF16),
                        pltpu.VMEM((srows, ROUTE_LANES), F32),
                        pltpu.VMEM((srows, d), BF16),
                        pltpu.VMEM((rows, srows), BF16)],
        compiler_params=pltpu.CompilerParams(
            dimension_semantics=("arbitrary",), vmem_limit_bytes=VMEM_LIMIT,
            allow_input_fusion=[i in (4, 10, 11, 12) for i in range(15)]),
        name="ffn",
    )(x, mixin, mod, mod, p["w_out"], p["ln1_g"], p["ln1_b"], p["w_route_split"], p["b_route"],
      tri, p["w_gate"], p["w_up"], p["w_down"], p["ln2_g"], p["ln2_b"])


def _stream(x, mod, k0, v0, u0, p, *, mixer_tm, masked, bb, tt):
    mixin, nk, nv, nu = _mixer(x, mod, p["w_in"], p["bias"], p["sink"], p["w_dw"], p["b_dw"],
                               p["cln_g"], p["cln_b"], k0, v0, u0, tm=mixer_tm, masked=masked)
    y = _ffn(x, mixin, mod, p, bb=bb, tt=tt)
    bsz = x.shape[0]
    cache_shape = (1, bsz, WINDOW, N_KV_HEADS, HEAD_DIM)
    return y, nk.reshape(cache_shape), nv.reshape(cache_shape), nu[None]


def kernel(x_prompt, x_sample, cache_attn_k, cache_attn_v, state_conv, c_prompt, c_sample, rel_bias, w_ada, b_ada, w_in, attn_sinks, w_dw, b_dw, conv_ln_g, conv_ln_b, w_out, ln1_g, ln1_b, w_group, b_group, w_erouter, b_erouter, w_gate, w_up, w_down, ln2_g, ln2_b):
    bp = x_prompt.shape[0]
    bs = x_sample.shape[0]
    mod = _modulation(jnp.concatenate([c_prompt, c_sample], 0), w_ada[0], b_ada[0])
    mod = mod.reshape(bp + bs, 6, D_MODEL)

    w_route = jnp.concatenate([w_erouter[0].reshape(D_MODEL, N_EXPERTS), w_group[0]], 1)
    w_route = jnp.pad(w_route, ((0, 0), (0, ROUTE_LANES - N_EXPERTS - N_GROUPS)))
    b_route = jnp.concatenate([b_erouter[0].reshape(N_EXPERTS), b_group[0]])
    b_route = jnp.pad(b_route, (0, ROUTE_LANES - N_EXPERTS - N_GROUPS)).reshape(1, ROUTE_LANES)
    w_route_hi = w_route.astype(BF16)
    w_route_lo = (w_route - w_route_hi.astype(F32)).astype(BF16)

    p = {
        "w_in": w_in[0].astype(BF16),
        "bias": _relative_bias(rel_bias),
        "sink": attn_sinks[0],
        "w_dw": jnp.broadcast_to(w_dw[0][:, None, :], (CONV_K, SUBLANES, CONV_CH)),
        "b_dw": b_dw[0].reshape(1, CONV_CH),
        "cln_g": conv_ln_g[0].reshape(1, CONV_CH), "cln_b": conv_ln_b[0].reshape(1, CONV_CH),
        "w_out": w_out[0].astype(BF16),
        "ln1_g": ln1_g[0].reshape(1, D_MODEL), "ln1_b": ln1_b[0].reshape(1, D_MODEL),
        "w_route_split": jnp.concatenate([w_route_hi, w_route_lo], 1),
        "b_route": b_route,
        "w_gate": w_gate[0].astype(BF16), "w_up": w_up[0].astype(BF16),
        "w_down": w_down[0].astype(BF16).reshape(N_GROUPS, EXPERTS_PER_GROUP * D_EXPERT, D_MODEL),
        "ln2_g": ln2_g[0].reshape(1, D_MODEL), "ln2_b": ln2_b[0].reshape(1, D_MODEL),
    }

    zk = jnp.zeros((bp, WINDOW, KV_WIDTH), F32)
    zu = jnp.zeros((bp, CONV_PAD, CONV_CH), F32)
    yp, pk, pv, pc = _stream(x_prompt, mod[:bp], zk, zk, zu, p, mixer_tm=512, masked=True, bb=1, tt=512)

    k0 = cache_attn_k[0].reshape(bs, WINDOW, KV_WIDTH)
    v0 = cache_attn_v[0].reshape(bs, WINDOW, KV_WIDTH)
    u0 = jnp.pad(state_conv[0], ((0, 0), (CONV_PAD - CONV_HIST, 0), (0, 0)))
    ts = x_sample.shape[1]
    ys, sk, sv, sc = _stream(x_sample, mod[bp:], k0, v0, u0, p, mixer_tm=ts, masked=False, bb=8, tt=ts)
    return yp, ys, pk, pv, pc, sk, sv, sc
```
